```python
import math
import jax, jax.numpy as jnp
from jax import lax
import numpy as np

D_MODEL = 4096
BATCH = 1
SEQ = 8192
DEPTH = 1
DEC_BATCH = 128
DEC_SEQ = 1
PAST_LEN = 2048
PAGE_SIZE = 128

MIX_W = D_MODEL
GLA_HEADS = 16
GLA_DV = MIX_W // 2 // GLA_HEADS
GLA_DK = GLA_DV // 2
GLA_KW = GLA_HEADS * GLA_DK
GLA_VW = GLA_HEADS * GLA_DV
GLA_LOWRANK = 16
GLA_GATE_TAU = 16.0
GLA_CHUNK = 64
MOBA_HEADS = 16
MOBA_HD = (MIX_W - GLA_VW) // MOBA_HEADS
MOBA_KV_HEADS = 4
MOBA_GROUP = MOBA_HEADS // MOBA_KV_HEADS
MOBA_QW = MOBA_HEADS * MOBA_HD
MOBA_KVW = MOBA_KV_HEADS * MOBA_HD
MOBA_BLOCK = 256
MOBA_TOPK = 3
MOBA_QCHUNK = 32
NORM_EPS = 1e-5
DEEPNORM_ALPHA = (2.0 * DEPTH) ** 0.25
DEEPNORM_BETA = (8.0 * DEPTH) ** -0.25
SPLIT_SIZES = (MOBA_QW, MOBA_KVW, MOBA_KVW, MOBA_QW, GLA_KW, GLA_KW, GLA_VW, GLA_VW, GLA_LOWRANK)
SPLIT_POINTS = tuple(int(c) for c in np.cumsum(SPLIT_SIZES)[:-1])
IN_W = sum(SPLIT_SIZES)

kernel_name = 'hymba_gla_moba_deepnorm_step'


def layer_norm(h, g, b):
    hf = h.astype(jnp.float32)
    mu = jnp.mean(hf, axis=-1, keepdims=True)
    var = jnp.mean(jnp.square(hf - mu), axis=-1, keepdims=True)
    return ((hf - mu) * lax.rsqrt(var + NORM_EPS) * g + b).astype(h.dtype)


def gla_scan(q, k, v, log_a, s0):
    B, T, H, DK = q.shape
    DV = v.shape[-1]
    L = math.gcd(T, GLA_CHUNK)
    nC = T // L

    def to_chunks(t):
        return t.astype(jnp.float32).reshape(B, nC, L, H, t.shape[-1]).transpose(1, 0, 3, 2, 4)

    causal = jnp.tril(jnp.ones((L, L), dtype=bool))

    def step(S, inp):
        qc, kc, vc, ac = inp
        b = jnp.cumsum(ac, axis=2)
        diff = b[:, :, :, None, :] - b[:, :, None, :, :]
        decay = jnp.exp(jnp.where(causal[None, None, :, :, None], diff, -jnp.inf))
        scores = jnp.einsum('bhid,bhjd,bhijd->bhij', qc, kc, decay)
        o = jnp.einsum('bhij,bhjv->bhiv', scores, vc) + jnp.einsum('bhid,bhdv->bhiv', qc * jnp.exp(b), S)
        b_last = b[:, :, -1, :]
        S = jnp.exp(b_last)[..., None] * S + jnp.einsum(
            'bhjd,bhjv->bhdv', kc * jnp.exp(b_last[:, :, None, :] - b), vc)
        return S, o

    S, o = lax.scan(step, s0.astype(jnp.float32), (to_chunks(q), to_chunks(k), to_chunks(v), to_chunks(log_a)))
    o = o.transpose(1, 0, 3, 2, 4).reshape(B, T, H, DV)
    return o, S


def moba_attend(q, k, v, q_pos):
    B, Tq, H, D = q.shape
    Tk = k.shape[1]
    n_blk = -(-Tk // MOBA_BLOCK)
    pad = n_blk * MOBA_BLOCK - Tk
    k = jnp.pad(k, ((0, 0), (0, pad), (0, 0), (0, 0)))
    v = jnp.pad(v, ((0, 0), (0, pad), (0, 0), (0, 0)))
    kb = k.reshape(B, n_blk, MOBA_BLOCK, MOBA_KV_HEADS, D).transpose(0, 3, 1, 2, 4)
    vb = v.reshape(B, n_blk, MOBA_BLOCK, MOBA_KV_HEADS, D).transpose(0, 3, 1, 2, 4)
    k_mean = jnp.mean(kb.astype(jnp.float32), axis=3)
    qg = q.reshape(B, Tq, MOBA_KV_HEADS, MOBA_GROUP, D)
    own = q_pos // MOBA_BLOCK
    gate = jnp.einsum('btkgd,bknd->btkgn', qg.astype(jnp.float32), k_mean)
    fully_past = jnp.arange(n_blk)[None, :] < own[:, None]
    gate = jnp.where(fully_past[None, :, None, None, :], gate, -jnp.inf)
    n_sel = min(MOBA_TOPK, n_blk)
    _, sel = lax.top_k(gate, n_sel)
    own_b = jnp.broadcast_to(own[None, :, None, None, None], (B, Tq, MOBA_KV_HEADS, MOBA_GROUP, 1))
    blocks = jnp.concatenate([sel.astype(jnp.int32), own_b.astype(jnp.int32)], axis=-1)
    n_slot = n_sel + 1
    valid = jnp.concatenate([jnp.arange(n_sel)[None, :] < own[:, None],
                             jnp.ones((Tq, 1), dtype=bool)], axis=1)
    QB = math.gcd(Tq, MOBA_QCHUNK)
    nC = Tq // QB
    pos_c = q_pos.reshape(nC, QB)
    valid_c = valid.reshape(nC, QB, n_slot)
    kv_idx = jnp.arange(MOBA_KV_HEADS)[None, :, None, None]
    blk_off = jnp.arange(MOBA_BLOCK, dtype=jnp.int32)
    scale = D ** -0.5

    def per_seq(args):
        q_s, kb_s, vb_s, blk_s = args

        def per_chunk(cargs):
            qc, bc, pc, vcm = cargs
            kg = kb_s[kv_idx, bc]
            vg = vb_s[kv_idx, bc]
            s = jnp.einsum('qkgd,qkgsld->qkgsl', qc, kg, preferred_element_type=jnp.float32) * scale
            key_pos = bc[..., None] * MOBA_BLOCK + blk_off
            mask = vcm[:, None, None, :, None] & (key_pos <= pc[:, None, None, None, None])
            s = jnp.where(mask, s, -jnp.inf)
            p = jax.nn.softmax(s.reshape(s.shape[:3] + (-1,)), axis=-1).reshape(s.shape)
            return jnp.einsum('qkgsl,qkgsld->qkgd', p.astype(vg.dtype), vg)

        o = lax.map(per_chunk, (q_s.reshape(nC, QB, MOBA_KV_HEADS, MOBA_GROUP, D),
                                blk_s.reshape(nC, QB, MOBA_KV_HEADS, MOBA_GROUP, n_slot),
                                pos_c, valid_c))
        return o.reshape(Tq, H, D)

    return lax.map(per_seq, (qg, kb, vb, blocks))


def hybrid_layer(x, q_pos, past_k, past_v, s0, w_in, w_a2, b_a, gla_gain, w_out, ln_g, ln_b):
    B, T, _ = x.shape
    proj = jnp.einsum('btd,de->bte', x, w_in)
    mq, mk, mv, mg, gq, gk, gv, gg, ga = jnp.split(proj, SPLIT_POINTS, axis=-1)
    mq = mq.reshape(B, T, MOBA_HEADS, MOBA_HD)
    mk = mk.reshape(B, T, MOBA_KV_HEADS, MOBA_HD)
    mv = mv.reshape(B, T, MOBA_KV_HEADS, MOBA_HD)
    if past_k is None:
        k_all, v_all = mk, mv
    else:
        k_all = jnp.concatenate([past_k.astype(mk.dtype), mk], axis=1)
        v_all = jnp.concatenate([past_v.astype(mv.dtype), mv], axis=1)
    o_m = moba_attend(mq, k_all, v_all, q_pos).reshape(B, T, MOBA_QW) * jax.nn.silu(mg)
    log_a = jax.nn.log_sigmoid((jnp.einsum('btr,rk->btk', ga, w_a2) + b_a).astype(jnp.float32)) / GLA_GATE_TAU
    o_g, s_new = gla_scan(gq.reshape(B, T, GLA_HEADS, GLA_DK) * (GLA_DK ** -0.5),
                          gk.reshape(B, T, GLA_HEADS, GLA_DK),
                          gv.reshape(B, T, GLA_HEADS, GLA_DV),
                          log_a.reshape(B, T, GLA_HEADS, GLA_DK), s0)
    o_g = o_g * lax.rsqrt(jnp.mean(o_g * o_g, axis=-1, keepdims=True) + NORM_EPS) * gla_gain
    o_g = o_g.reshape(B, T, GLA_VW).astype(x.dtype) * jax.nn.silu(gg)
    mixed = jnp.einsum('bte,ed->btd', jnp.concatenate([o_g, o_m], axis=-1), w_out)
    y = layer_norm(DEEPNORM_ALPHA * x + mixed, ln_g, ln_b)
    return y, mk, mv, s_new.astype(s0.dtype)


def setup_inputs(seed: int = 0) -> dict:
    key = jax.random.key(seed)
    ks = jax.random.split(key, 14)
    n_pages = PAST_LEN // PAGE_SIZE
    n_pool = (5 * DEC_BATCH * n_pages + 3) // 4
    nrm = jax.random.normal
    x_prompt = nrm(ks[0], (BATCH, SEQ, D_MODEL), jnp.float32)
    x_sample = nrm(ks[1], (DEC_BATCH, DEC_SEQ, D_MODEL), jnp.float32)
    cache_k = nrm(ks[2], (DEPTH, n_pool, PAGE_SIZE, MOBA_KV_HEADS, MOBA_HD), jnp.float32)
    cache_v = nrm(ks[3], (DEPTH, n_pool, PAGE_SIZE, MOBA_KV_HEADS, MOBA_HD), jnp.float32)
    state_gla = nrm(ks[4], (DEPTH, DEC_BATCH, GLA_HEADS, GLA_DK, GLA_DV), jnp.float32)
    page_table = jax.random.permutation(ks[5], n_pool)[:DEC_BATCH * n_pages].reshape(
        DEC_BATCH, n_pages).astype(jnp.int32)
    w_in = nrm(ks[6], (DEPTH, D_MODEL, IN_W), jnp.float32) * D_MODEL ** -0.5
    w_a2 = nrm(ks[7], (DEPTH, GLA_LOWRANK, GLA_KW), jnp.float32) * GLA_LOWRANK ** -0.5
    b_a = 0.1 * nrm(ks[8], (DEPTH, GLA_KW), jnp.float32)
    gla_gain = 1.0 + 0.02 * nrm(ks[9], (DEPTH, GLA_DV), jnp.float32)
    w_out = nrm(ks[10], (DEPTH, MIX_W, D_MODEL), jnp.float32) * (MIX_W ** -0.5 * DEEPNORM_BETA)
    ln_g = 1.0 + 0.02 * nrm(ks[11], (DEPTH, D_MODEL), jnp.float32)
    ln_b = 0.02 * nrm(ks[12], (DEPTH, D_MODEL), jnp.float32)
    return {'x_prompt': x_prompt, 'x_sample': x_sample, 'cache_k': cache_k, 'cache_v': cache_v,
            'state_gla': state_gla, 'page_table': page_table, 'w_in': w_in, 'w_a2': w_a2,
            'b_a': b_a, 'gla_gain': gla_gain, 'w_out': w_out, 'ln_g': ln_g, 'ln_b': ln_b}


def reference(x_prompt, x_sample, cache_k, cache_v, state_gla, page_table,
              w_in, w_a2, b_a, gla_gain, w_out, ln_g, ln_b):
    B, T, _ = x_prompt.shape
    DB, DT, _ = x_sample.shape
    n_pages = page_table.shape[1]
    past_len = n_pages * cache_k.shape[2]
    pos_prompt = jnp.arange(T, dtype=jnp.int32)
    pos_sample = past_len + jnp.arange(DT, dtype=jnp.int32)
    s0_prompt = jnp.zeros((B, GLA_HEADS, GLA_DK, GLA_DV), x_prompt.dtype)
    y_p, y_s = x_prompt, x_sample
    kp_l, vp_l, sp_l, ks_l, vs_l, ss_l = [], [], [], [], [], []
    for l in range(DEPTH):
        params = (w_in[l], w_a2[l], b_a[l], gla_gain[l], w_out[l], ln_g[l], ln_b[l])
        y_p, kp, vp, sp = hybrid_layer(y_p, pos_prompt, None, None, s0_prompt, *params)
        past_k = cache_k[l][page_table].reshape(DB, past_len, MOBA_KV_HEADS, MOBA_HD)
        past_v = cache_v[l][page_table].reshape(DB, past_len, MOBA_KV_HEADS, MOBA_HD)
        y_s, ksn, vsn, ssn = hybrid_layer(y_s, pos_sample, past_k, past_v, state_gla[l], *params)
        kp_l.append(kp); vp_l.append(vp); sp_l.append(sp)
        ks_l.append(ksn); vs_l.append(vsn); ss_l.append(ssn)
    new_k_prompt = jnp.stack(kp_l)
    new_v_prompt = jnp.stack(vp_l)
    new_gla_prompt = jnp.stack(sp_l)
    new_k_sample = jnp.stack(ks_l)
    new_v_sample = jnp.stack(vs_l)
    new_gla_sample = jnp.stack(ss_l)
    return (y_p, y_s, new_k_prompt, new_v_prompt, new_gla_prompt, new_k_sample, new_v_sample, new_gla_sample)
```

```python
import functools
import math

import jax
import jax.numpy as jnp
from jax import lax
from jax.experimental import pallas as pl
from jax.experimental.pallas import tpu as pltpu

F32 = jnp.float32
BF16 = jnp.bfloat16
NEG_INF = float("-inf")

MOBA_HEADS = 16
MOBA_KV_HEADS = 4
MOBA_GROUP = MOBA_HEADS // MOBA_KV_HEADS
MOBA_HD = 128
MOBA_BLOCK = 256
MOBA_TOPK = 3
GLA_HEADS = 16
GLA_DK = 64
GLA_DV = 128
GLA_LOWRANK = 16
GLA_GATE_TAU = 16.0
GLA_SUB = 16
GLA_PAIRS = GLA_HEADS // 2
NORM_EPS = 1e-5

MOBA_QW = MOBA_HEADS * MOBA_HD
MOBA_KVW = MOBA_KV_HEADS * MOBA_HD
GLA_KW = GLA_HEADS * GLA_DK
GLA_VW = GLA_HEADS * GLA_DV
OFF_MQ = 0
OFF_MK = OFF_MQ + MOBA_QW
OFF_MV = OFF_MK + MOBA_KVW
OFF_MG = OFF_MV + MOBA_KVW
OFF_GQ = OFF_MG + MOBA_QW
OFF_GK = OFF_GQ + GLA_KW
OFF_GV = OFF_GK + GLA_KW
OFF_GG = OFF_GV + GLA_VW
OFF_GA = OFF_GG + GLA_VW
MAIN_W = OFF_GA

LANE = 128
VMEM_LIMIT_BYTES = 56 * 1024 * 1024

_NT = (((1,), (1,)), ((), ()))
_TN = (((0,), (0,)), ((), ()))


def _params(*sem):
    return pltpu.CompilerParams(dimension_semantics=sem, vmem_limit_bytes=VMEM_LIMIT_BYTES)


def _silu(x):
    return x * (1.0 / (1.0 + jnp.exp(-x)))


def _matmul_kernel(x_ref, w_ref, o_ref):
    o_ref[...] = jnp.dot(x_ref[...], w_ref[...], preferred_element_type=F32)


def _matmul(x, w, tm, tn):
    m, k = x.shape
    n = w.shape[1]
    return pl.pallas_call(
        _matmul_kernel,
        grid=(n // tn, m // tm),
        in_specs=[pl.BlockSpec((tm, k), lambda j, i: (i, 0)),
                  pl.BlockSpec((k, tn), lambda j, i: (0, j))],
        out_specs=pl.BlockSpec((tm, tn), lambda j, i: (i, j)),
        out_shape=jax.ShapeDtypeStruct((m, n), F32),
        compiler_params=_params("arbitrary", "arbitrary"),
        name="in_proj",
    )(x, w)


def _log_decay_kernel(x_ref, wlr_ref, wa2_ref, ba_ref, o_ref):
    ga = jnp.dot(x_ref[...], wlr_ref[...], preferred_element_type=F32)
    z = jnp.dot(ga, wa2_ref[...], precision=lax.Precision.HIGHEST,
                preferred_element_type=F32) + ba_ref[...]
    log_sig = jnp.minimum(z, 0.0) - jnp.log(1.0 + jnp.exp(-jnp.abs(z)))
    o_ref[...] = log_sig * (1.0 / GLA_GATE_TAU)


def _log_decay(x, w_lr, w_a2p, b_a, tm):
    m, k = x.shape
    return pl.pallas_call(
        _log_decay_kernel,
        grid=(m // tm,),
        in_specs=[pl.BlockSpec((tm, k), lambda i: (i, 0)),
                  pl.BlockSpec((k, LANE), lambda i: (0, 0)),
                  pl.BlockSpec((LANE, GLA_KW), lambda i: (0, 0)),
                  pl.BlockSpec((1, GLA_KW), lambda i: (0, 0))],
        out_specs=pl.BlockSpec((tm, GLA_KW), lambda i: (i, 0)),
        out_shape=jax.ShapeDtypeStruct((m, GLA_KW), F32),
        compiler_params=_params("arbitrary"),
        name="gla_log_decay",
    )(x, w_lr, w_a2p, b_a)


def _topk_bias(gate, n_valid, n_rows):
    blk = lax.broadcasted_iota(jnp.int32, gate.shape, 0)
    cur = jnp.where(blk < n_valid, gate, NEG_INF)
    sel = jnp.zeros(gate.shape, jnp.bool_)
    for _ in range(MOBA_TOPK):
        mx = jnp.max(cur, axis=0, keepdims=True)
        hit = jnp.logical_and(cur == mx, mx > NEG_INF)
        first = jnp.min(jnp.where(hit, blk, n_rows), axis=0, keepdims=True)
        pick = blk == first
        sel = jnp.logical_or(sel, pick)
        cur = jnp.where(pick, NEG_INF, cur)
    return jnp.where(sel, 0.0, NEG_INF)


def _moba_prompt_kernel(q_ref, k_ref, v_ref, g_ref, o_ref,
                        kbf_ref, vt_ref, kmean_ref, selb_ref, qbf_ref, m_ref, l_ref, acc_ref,
                        *, n_blocks):
    qi = pl.program_id(1)
    blk_sz = MOBA_BLOCK
    rows = MOBA_GROUP * blk_sz

    @pl.when(qi == 0)
    def _prepare_kv_head():
        kmean_ref[...] = jnp.zeros(kmean_ref.shape, F32)

        def body(j, carry):
            r0 = pl.multiple_of(j * blk_sz, blk_sz)
            kb = k_ref[pl.ds(r0, blk_sz), :]
            kbf_ref[pl.ds(r0, blk_sz), :] = kb.astype(BF16)
            kmean_ref[pl.ds(j, 1), :] = jnp.sum(kb, axis=0, keepdims=True) * (1.0 / blk_sz)
            vt_ref[:, pl.ds(r0, blk_sz)] = v_ref[pl.ds(r0, blk_sz), :].T.astype(BF16)
            return carry

        lax.fori_loop(0, n_blocks, body, 0)

    q = q_ref[...]
    qs = jnp.concatenate([q[:, MOBA_HD * g:MOBA_HD * (g + 1)] for g in range(MOBA_GROUP)], axis=0)
    gate = lax.dot_general(kmean_ref[...], qs, _NT, precision=lax.Precision.HIGHEST,
                           preferred_element_type=F32)
    selb_ref[...] = _topk_bias(gate, qi, kmean_ref.shape[0])
    qbf_ref[...] = (qs * (MOBA_HD ** -0.5)).astype(BF16)

    r_own = pl.multiple_of(qi * blk_sz, blk_sz)
    s = lax.dot_general(kbf_ref[pl.ds(r_own, blk_sz), :], qbf_ref[...], _NT,
                        preferred_element_type=F32)
    key_t = lax.broadcasted_iota(jnp.int32, s.shape, 0)
    row_t = lax.broadcasted_iota(jnp.int32, s.shape, 1) % blk_sz
    s = jnp.where(key_t <= row_t, s, NEG_INF)
    m0 = jnp.max(s, axis=0, keepdims=True)
    p = jnp.exp(s - m0)
    m_ref[...] = m0
    l_ref[...] = jnp.sum(p, axis=0, keepdims=True)
    acc_ref[...] = jnp.dot(vt_ref[:, pl.ds(r_own, blk_sz)], p.astype(BF16),
                           preferred_element_type=F32)

    def past_block(j, carry):
        r0 = pl.multiple_of(j * blk_sz, blk_sz)
        s = lax.dot_general(kbf_ref[pl.ds(r0, blk_sz), :], qbf_ref[...], _NT,
                            preferred_element_type=F32)
        s = s + selb_ref[pl.ds(j, 1), :]
        m_prev = m_ref[...]
        m_new = jnp.maximum(m_prev, jnp.max(s, axis=0, keepdims=True))
        alpha = jnp.exp(m_prev - m_new)
        p = jnp.exp(s - m_new)
        l_ref[...] = alpha * l_ref[...] + jnp.sum(p, axis=0, keepdims=True)
        acc_ref[...] = alpha * acc_ref[...] + jnp.dot(vt_ref[:, pl.ds(r0, blk_sz)], p.astype(BF16),
                                                      preferred_element_type=F32)
        m_ref[...] = m_new
        return carry

    lax.fori_loop(0, qi, past_block, 0)

    o_t = acc_ref[...] * (1.0 / l_ref[...])
    o = o_t.T
    o = jnp.concatenate([o[blk_sz * g:blk_sz * (g + 1), :] for g in range(MOBA_GROUP)], axis=1)
    o_ref[...] = (o * _silu(g_ref[...])).astype(o_ref.dtype)


def _moba_prompt(proj):
    t = proj.shape[0]
    assert t % MOBA_BLOCK == 0
    nb = t // MOBA_BLOCK
    nb_pad = -(-nb // 8) * 8
    rows = MOBA_GROUP * MOBA_BLOCK
    gw = MOBA_GROUP * MOBA_HD
    return pl.pallas_call(
        functools.partial(_moba_prompt_kernel, n_blocks=nb),
        grid=(MOBA_KV_HEADS, nb),
        in_specs=[pl.BlockSpec((MOBA_BLOCK, gw), lambda h, i: (i, OFF_MQ // gw + h)),
                  pl.BlockSpec((t, MOBA_HD), lambda h, i: (0, OFF_MK // MOBA_HD + h)),
                  pl.BlockSpec((t, MOBA_HD), lambda h, i: (0, OFF_MV // MOBA_HD + h)),
                  pl.BlockSpec((MOBA_BLOCK, gw), lambda h, i: (i, OFF_MG // gw + h))],
        out_specs=pl.BlockSpec((MOBA_BLOCK, gw), lambda h, i: (i, h)),
        out_shape=jax.ShapeDtypeStruct((t, MOBA_QW), BF16),
        scratch_shapes=[pltpu.VMEM((t, MOBA_HD), BF16),
                        pltpu.VMEM((MOBA_HD, t), BF16),
                        pltpu.VMEM((nb_pad, MOBA_HD), F32),
                        pltpu.VMEM((nb_pad, rows), F32),
                        pltpu.VMEM((rows, MOBA_HD), BF16),
                        pltpu.VMEM((1, rows), F32),
                        pltpu.VMEM((1, rows), F32),
                        pltpu.VMEM((MOBA_HD, rows), F32)],
        compiler_params=_params("arbitrary", "arbitrary"),
        name="moba_prompt",
    )(proj, proj, proj, proj)


def _gla_prompt_kernel(q_ref, k_ref, la_ref, v_ref, gg_ref, gain_ref, o_ref, sout_ref,
                       s_ref, b_ref, oacc_ref, *, n_tblocks, tb):
    ti = pl.program_id(1)
    sub = GLA_SUB
    kw = 2 * GLA_DK
    vw = 2 * GLA_DV

    @pl.when(ti == 0)
    def _zero_state():
        s_ref[...] = jnp.zeros(s_ref.shape, F32)

    r_i = lax.broadcasted_iota(jnp.int32, (tb, tb), 0)
    c_i = lax.broadcasted_iota(jnp.int32, (tb, tb), 1)
    tril = jnp.where(jnp.logical_and(r_i // sub == c_i // sub, c_i <= r_i), 1.0, 0.0).astype(F32)
    b_ref[...] = jnp.dot(tril, la_ref[...], precision=lax.Precision.HIGHEST,
                         preferred_element_type=F32)

    row_h = lax.broadcasted_iota(jnp.int32, (kw, vw), 0) // GLA_DK
    col_h = lax.broadcasted_iota(jnp.int32, (kw, vw), 1) // GLA_DV
    same_head = row_h == col_h
    head_sum = jnp.where(same_head, 1.0, 0.0).astype(BF16)
    j_idx = lax.broadcasted_iota(jnp.int32, (sub, kw), 0)

    def sub_chunk(c, carry):
        r0 = pl.multiple_of(c * sub, sub)
        q = q_ref[pl.ds(r0, sub), :] * (GLA_DK ** -0.5)
        k = k_ref[pl.ds(r0, sub), :]
        b = b_ref[pl.ds(r0, sub), :]
        v = v_ref[pl.ds(r0, sub), :]
        b_last = b[sub - 1:sub, :]
        state = s_ref[...]
        o_inter = jnp.dot((q * jnp.exp(b)).astype(BF16), state.astype(BF16),
                          preferred_element_type=F32)
        w_rows = []
        for i in range(sub):
            d = b[i:i + 1, :] - b
            dec = jnp.exp(jnp.where(j_idx <= i, d, NEG_INF))
            w_rows.append(dec * k * q[i:i + 1, :])
        w = jnp.concatenate(w_rows, axis=0)
        sc = jnp.dot(w.astype(BF16), head_sum, preferred_element_type=F32)
        z = sc * jnp.concatenate([v] * sub, axis=0)
        o_diag = jnp.sum(z.reshape(sub, sub, vw), axis=1)
        oacc_ref[pl.ds(r0, sub), :] = o_inter + o_diag
        k_dec = (k * jnp.exp(b_last - b)).astype(BF16)
        upd = lax.dot_general(k_dec, v.astype(BF16), _TN, preferred_element_type=F32)
        e_col = jnp.broadcast_to(jnp.exp(b_last), (kw, kw)).T
        e_mat = jnp.concatenate([e_col, e_col], axis=1)
        s_ref[...] = state * e_mat + jnp.where(same_head, upd, 0.0)
        return carry

    lax.fori_loop(0, tb // sub, sub_chunk, 0)

    o = oacc_ref[...]
    gain = gain_ref[...]
    halves = []
    for h in range(2):
        oh = o[:, GLA_DV * h:GLA_DV * (h + 1)]
        ms = jnp.mean(oh * oh, axis=-1, keepdims=True)
        halves.append(oh * lax.rsqrt(ms + NORM_EPS) * gain)
    o = jnp.concatenate(halves, axis=1)
    o_ref[...] = (o * _silu(gg_ref[...])).astype(o_ref.dtype)

    @pl.when(ti == n_tblocks - 1)
    def _emit_state():
        st = s_ref[...]
        sout_ref[0] = st[0:GLA_DK, 0:GLA_DV]
        sout_ref[1] = st[GLA_DK:2 * GLA_DK, GLA_DV:2 * GLA_DV]


def _gla_prompt(proj, log_a, gain):
    t = proj.shape[0]
    tb = 256 if t % 256 == 0 else t
    assert t % tb == 0 and tb % GLA_SUB == 0
    nt = t // tb
    kw, vw = 2 * GLA_DK, 2 * GLA_DV
    return pl.pallas_call(
        functools.partial(_gla_prompt_kernel, n_tblocks=nt, tb=tb),
        grid=(GLA_PAIRS, nt),
        in_specs=[pl.BlockSpec((tb, kw), lambda p, i: (i, OFF_GQ // kw + p)),
                  pl.BlockSpec((tb, kw), lambda p, i: (i, OFF_GK // kw + p)),
                  pl.BlockSpec((tb, kw), lambda p, i: (i, p)),
                  pl.BlockSpec((tb, vw), lambda p, i: (i, OFF_GV // vw + p)),
                  pl.BlockSpec((tb, vw), lambda p, i: (i, OFF_GG // vw + p)),
                  pl.BlockSpec((1, GLA_DV), lambda p, i: (0, 0))],
        out_specs=[pl.BlockSpec((tb, vw), lambda p, i: (i, p)),
                   pl.BlockSpec((2, GLA_DK, GLA_DV), lambda p, i: (p, 0, 0))],
        out_shape=[jax.ShapeDtypeStruct((t, GLA_VW), BF16),
                   jax.ShapeDtypeStruct((GLA_HEADS, GLA_DK, GLA_DV), F32)],
        scratch_shapes=[pltpu.VMEM((kw, vw), F32),
                        pltpu.VMEM((tb, kw), F32),
                        pltpu.VMEM((tb, vw), F32)],
        compiler_params=_params("arbitrary", "arbitrary"),
        name="gla_prompt",
    )(proj, proj, log_a, proj, proj, gain)


def _out_proj_kernel(og_ref, om_ref, wg_ref, wm_ref, x_ref, lg_ref, lb_ref, o_ref,
                     *, n_ctiles, tn, alpha):
    j = pl.program_id(1)
    mixed = jnp.dot(og_ref[...], wg_ref[...], preferred_element_type=F32)
    mixed = mixed + jnp.dot(om_ref[...], wm_ref[...], preferred_element_type=F32)
    c0 = pl.multiple_of(j * tn, tn)
    o_ref[:, pl.ds(c0, tn)] = alpha * x_ref[:, pl.ds(c0, tn)] + mixed

    @pl.when(j == n_ctiles - 1)
    def _layer_norm():
        h = o_ref[...]
        mu = jnp.mean(h, axis=-1, keepdims=True)
        hc = h - mu
        var = jnp.mean(hc * hc, axis=-1, keepdims=True)
        o_ref[...] = hc * lax.rsqrt(var + NORM_EPS) * lg_ref[...] + lb_ref[...]


def _out_proj(og, om, w_g, w_m, x, ln_g, ln_b, alpha, tm, tn):
    m, d = x.shape
    kg, km = og.shape[1], om.shape[1]
    nct = d // tn
    return pl.pallas_call(
        functools.partial(_out_proj_kernel, n_ctiles=nct, tn=tn, alpha=alpha),
        grid=(m // tm, nct),
        in_specs=[pl.BlockSpec((tm, kg), lambda i, j: (i, 0)),
                  pl.BlockSpec((tm, km), lambda i, j: (i, 0)),
                  pl.BlockSpec((kg, tn), lambda i, j: (0, j)),
                  pl.BlockSpec((km, tn), lambda i, j: (0, j)),
                  pl.BlockSpec((tm, d), lambda i, j: (i, 0)),
                  pl.BlockSpec((1, d), lambda i, j: (0, 0)),
                  pl.BlockSpec((1, d), lambda i, j: (0, 0))],
        out_specs=pl.BlockSpec((tm, d), lambda i, j: (i, 0)),
        out_shape=jax.ShapeDtypeStruct((m, d), F32),
        compiler_params=_params("arbitrary", "arbitrary"),
        name="out_proj_ln",
    )(og, om, w_g, w_m, x, ln_g, ln_b)


def _moba_decode_kernel(pt_ref, q_ref, kn_ref, vn_ref, g_ref, *rest, n_pages, page):
    del pt_ref
    k_refs = rest[:n_pages]
    v_refs = rest[n_pages:2 * n_pages]
    o_ref = rest[2 * n_pages]
    n_keys = n_pages * page
    n_blk = n_keys // MOBA_BLOCK
    heads = MOBA_HEADS

    q = q_ref[0] * (MOBA_HD ** -0.5)
    head_kv = lax.broadcasted_iota(jnp.int32, (heads, MOBA_KVW), 0) // MOBA_GROUP
    lane_kv = lax.broadcasted_iota(jnp.int32, (heads, MOBA_KVW), 1) // MOBA_HD
    q_bd = jnp.where(head_kv == lane_kv, jnp.concatenate([q] * MOBA_KV_HEADS, axis=1), 0.0).astype(BF16)
    s = jnp.concatenate(
        [lax.dot_general(q_bd, k_refs[p][0].astype(BF16), _NT, preferred_element_type=F32)
         for p in range(n_pages)], axis=1)

    bm = jnp.concatenate(
        [jnp.broadcast_to(jnp.mean(s[:, MOBA_BLOCK * j:MOBA_BLOCK * (j + 1)], axis=1, keepdims=True),
                          (heads, MOBA_BLOCK)) for j in range(n_blk)], axis=1)
    blk = lax.broadcasted_iota(jnp.int32, s.shape, 1) // MOBA_BLOCK
    cur = bm
    sel = jnp.zeros(s.shape, jnp.bool_)
    for _ in range(min(MOBA_TOPK, n_blk)):
        mx = jnp.max(cur, axis=1, keepdims=True)
        first = jnp.min(jnp.where(cur == mx, blk, n_blk), axis=1, keepdims=True)
        pick = blk == first
        sel = jnp.logical_or(sel, pick)
        cur = jnp.where(pick, NEG_INF, cur)

    kn = kn_ref[0]
    vn = vn_ref[0]
    kn16 = jnp.concatenate([jnp.broadcast_to(kn[i:i + 1, :], (MOBA_GROUP, MOBA_HD))
                            for i in range(MOBA_KV_HEADS)], axis=0)
    vn16 = jnp.concatenate([jnp.broadcast_to(vn[i:i + 1, :], (MOBA_GROUP, MOBA_HD))
                            for i in range(MOBA_KV_HEADS)], axis=0)
    s_new = jnp.sum(q * kn16, axis=1, keepdims=True)
    sm = jnp.where(sel, s, NEG_INF)
    m = jnp.maximum(jnp.max(sm, axis=1, keepdims=True), s_new)
    p = jnp.exp(sm - m)
    p_new = jnp.exp(s_new - m)
    denom = jnp.sum(p, axis=1, keepdims=True) + p_new
    pb = p.astype(BF16)
    acc = jnp.zeros((heads, MOBA_KVW), F32)
    for pg in range(n_pages):
        acc = acc + jnp.dot(pb[:, page * pg:page * (pg + 1)], v_refs[pg][0].astype(BF16),
                            preferred_element_type=F32)
    head_kv1 = lax.broadcasted_iota(jnp.int32, (heads, MOBA_HD), 0) // MOBA_GROUP
    o = jnp.zeros((heads, MOBA_HD), F32)
    for kv in range(MOBA_KV_HEADS):
        o = o + jnp.where(head_kv1 == kv, acc[:, MOBA_HD * kv:MOBA_HD * (kv + 1)], 0.0)
    o = (o + p_new * vn16) * (1.0 / denom)
    o_ref[0] = (o * _silu(g_ref[0])).astype(o_ref.dtype)


def _moba_decode(page_table, q3, kn3, vn3, g3, cache_k, cache_v):
    db, n_pages = page_table.shape
    n_pool, page = cache_k.shape[0], cache_k.shape[1]
    assert (n_pages * page) % MOBA_BLOCK == 0
    ck = cache_k.reshape(n_pool, page, MOBA_KVW)
    cv = cache_v.reshape(n_pool, page, MOBA_KVW)

    def page_spec(pg):
        return pl.BlockSpec((1, page, MOBA_KVW), lambda b, pt: (pt[b, pg], 0, 0))

    def row_spec(r):
        return pl.BlockSpec((1, r, MOBA_HD), lambda b, pt: (b, 0, 0))

    grid_spec = pltpu.PrefetchScalarGridSpec(
        num_scalar_prefetch=1,
        grid=(db,),
        in_specs=[row_spec(MOBA_HEADS), row_spec(MOBA_KV_HEADS), row_spec(MOBA_KV_HEADS),
                  row_spec(MOBA_HEADS)]
                 + [page_spec(pg) for pg in range(n_pages)]
                 + [page_spec(pg) for pg in range(n_pages)],
        out_specs=pl.BlockSpec((1, MOBA_HEADS, MOBA_HD), lambda b, pt: (b, 0, 0)),
    )
    return pl.pallas_call(
        functools.partial(_moba_decode_kernel, n_pages=n_pages, page=page),
        grid_spec=grid_spec,
        out_shape=jax.ShapeDtypeStruct((db, MOBA_HEADS, MOBA_HD), BF16),
        compiler_params=_params("arbitrary"),
        name="moba_decode",
    )(page_table, q3, kn3, vn3, g3, *([ck] * n_pages), *([cv] * n_pages))


_GLA_DEC_GROUP = 8


def _gla_decode_kernel(q_ref, k_ref, la_ref, v_ref, gg_ref, gain_ref, s_ref, o_ref, sn_ref):
    grp = _GLA_DEC_GROUP

    def columns(x):
        xp = jnp.concatenate([x, jnp.zeros((LANE - grp, x.shape[1]), F32)], axis=0)
        return xp.T

    a_t = columns(jnp.exp(la_ref[...]))
    k_t = columns(k_ref[...])
    q_t = columns(q_ref[...] * (GLA_DK ** -0.5))
    gain = gain_ref[...]
    for i in range(grp):
        rows = []
        for h in range(GLA_HEADS):
            r = slice(GLA_DK * h, GLA_DK * (h + 1))
            v_h = jnp.broadcast_to(v_ref[i, h:h + 1, :], (GLA_DK, GLA_DV))
            s_new = a_t[r, i:i + 1] * s_ref[i, h] + k_t[r, i:i + 1] * v_h
            sn_ref[i, h] = s_new
            rows.append(jnp.sum(q_t[r, i:i + 1] * s_new, axis=0, keepdims=True))
        o = jnp.concatenate(rows, axis=0)
        ms = jnp.mean(o * o, axis=-1, keepdims=True)
        o = o * lax.rsqrt(ms + NORM_EPS) * gain
        o_ref[i] = (o * _silu(gg_ref[i])).astype(o_ref.dtype)


def _gla_decode(q2, k2, la2, v3, gg3, gain, state):
    db = q2.shape[0]
    grp = _GLA_DEC_GROUP
    assert db % grp == 0
    vec = pl.BlockSpec((grp, GLA_KW), lambda g: (g, 0))
    hd3 = pl.BlockSpec((grp, GLA_HEADS, GLA_DV), lambda g: (g, 0, 0))
    st = pl.BlockSpec((grp, GLA_HEADS, GLA_DK, GLA_DV), lambda g: (g, 0, 0, 0))
    return pl.pallas_call(
        _gla_decode_kernel,
        grid=(db // grp,),
        in_specs=[vec, vec, vec, hd3, hd3, pl.BlockSpec((1, GLA_DV), lambda g: (0, 0)), st],
        out_specs=[hd3, st],
        out_shape=[jax.ShapeDtypeStruct((db, GLA_HEADS, GLA_DV), BF16),
                   jax.ShapeDtypeStruct(state.shape, F32)],
        compiler_params=_params("arbitrary"),
        name="gla_decode",
    )(q2, k2, la2, v3, gg3, gain, state)


def _tile(n, pref):
    return pref if n % pref == 0 else n


def _layer_weights(w_in, w_a2, b_a, gla_gain, w_out, ln_g, ln_b):
    d = w_in.shape[0]
    w_main = w_in[:, :MAIN_W].astype(BF16)
    w_lr = jnp.zeros((d, LANE), BF16).at[:, :GLA_LOWRANK].set(w_in[:, OFF_GA:].astype(BF16))
    w_a2p = jnp.zeros((LANE, GLA_KW), F32).at[:GLA_LOWRANK, :].set(w_a2)
    w_out_bf = w_out.astype(BF16)
    return dict(w_main=w_main, w_lr=w_lr, w_a2p=w_a2p, b_a=b_a.reshape(1, GLA_KW),
                gain=gla_gain.reshape(1, GLA_DV), w_g=w_out_bf[:GLA_VW], w_m=w_out_bf[GLA_VW:],
                ln_g=ln_g.reshape(1, -1), ln_b=ln_b.reshape(1, -1))


def _project(x2, wts):
    m = x2.shape[0]
    xb = x2.astype(BF16)
    tm = _tile(m, 512)
    proj = _matmul(xb, wts["w_main"], tm, 1024)
    log_a = _log_decay(xb, wts["w_lr"], wts["w_a2p"], wts["b_a"], tm)
    return proj, log_a


def _finish(og, om, x2, wts, alpha):
    m = x2.shape[0]
    return _out_proj(og, om, wts["w_g"], wts["w_m"], x2, wts["ln_g"], wts["ln_b"], alpha,
                     _tile(m, 256), _tile(x2.shape[1], 1024))


def kernel(x_prompt, x_sample, cache_k, cache_v, state_gla, page_table,
           w_in, w_a2, b_a, gla_gain, w_out, ln_g, ln_b):
    bsz, t, d = x_prompt.shape
    db, dt, _ = x_sample.shape
    depth = w_in.shape[0]
    assert bsz == 1 and dt == 1
    alpha = (2.0 * depth) ** 0.25
    y_p = x_prompt.reshape(t, d)
    y_s = x_sample.reshape(db, d)
    kp_l, vp_l, sp_l, ks_l, vs_l, ss_l = [], [], [], [], [], []
    for l in range(depth):
        wts = _layer_weights(w_in[l], w_a2[l], b_a[l], gla_gain[l], w_out[l], ln_g[l], ln_b[l])
        proj, log_a = _project(y_p, wts)
        o_m = _moba_prompt(proj)
        o_g, s_p = _gla_prompt(proj, log_a, wts["gain"])
        kp_l.append(proj[:, OFF_MK:OFF_MV].reshape(bsz, t, MOBA_KV_HEADS, MOBA_HD))
        vp_l.append(proj[:, OFF_MV:OFF_MG].reshape(bsz, t, MOBA_KV_HEADS, MOBA_HD))
        sp_l.append(s_p.reshape(bsz, GLA_HEADS, GLA_DK, GLA_DV))
        y_p = _finish(o_g, o_m, y_p, wts, alpha)
        proj_s, log_a_s = _project(y_s, wts)
        k_new = proj_s[:, OFF_MK:OFF_MV].reshape(db, MOBA_KV_HEADS, MOBA_HD)
        v_new = proj_s[:, OFF_MV:OFF_MG].reshape(db, MOBA_KV_HEADS, MOBA_HD)
        o_m_s = _moba_decode(page_table,
                             proj_s[:, OFF_MQ:OFF_MK].reshape(db, MOBA_HEADS, MOBA_HD),
                             k_new, v_new,
                             proj_s[:, OFF_MG:OFF_GQ].reshape(db, MOBA_HEADS, MOBA_HD),
                             cache_k[l], cache_v[l])
        o_g_s, s_s = _gla_decode(proj_s[:, OFF_GQ:OFF_GK], proj_s[:, OFF_GK:OFF_GV], log_a_s,
                                 proj_s[:, OFF_GV:OFF_GG].reshape(db, GLA_HEADS, GLA_DV),
                                 proj_s[:, OFF_GG:OFF_GA].reshape(db, GLA_HEADS, GLA_DV),
                                 wts["gain"], state_gla[l])
        ks_l.append(k_new.reshape(db, dt, MOBA_KV_HEADS, MOBA_HD))
        vs_l.append(v_new.reshape(db, dt, MOBA_KV_HEADS, MOBA_HD))
        ss_l.append(s_s)
        y_s = _finish(o_g_s.reshape(db, GLA_VW), o_m_s.reshape(db, MOBA_QW), y_s, wts, alpha)
    return (y_p.reshape(bsz, t, d), y_s.reshape(db, dt, d),
            jnp.stack(kp_l), jnp.stack(vp_l), jnp.stack(sp_l),
            jnp.stack(ks_l), jnp.stack(vs_l), jnp.stack(ss_l))
```

```python
import functools
import math

import jax
import jax.numpy as jnp
from jax import lax
from jax.experimental import pallas as pl
from jax.experimental.pallas import tpu as pltpu

F32 = jnp.float32
BF16 = jnp.bfloat16
NEG_INF = float("-inf")
MASK_BIAS = -1e30
LOG2E = 1.4426950408889634

MOBA_HEADS = 16
MOBA_KV_HEADS = 4
MOBA_GROUP = MOBA_HEADS // MOBA_KV_HEADS
MOBA_HD = 128
MOBA_BLOCK = 256
MOBA_TOPK = 3
GLA_HEADS = 16
GLA_DK = 64
GLA_DV = 128
GLA_LOWRANK = 16
GLA_GATE_TAU = 16.0
GLA_SUB = 16
GLA_PAIRS = GLA_HEADS // 2
NORM_EPS = 1e-5

MOBA_QW = MOBA_HEADS * MOBA_HD
MOBA_KVW = MOBA_KV_HEADS * MOBA_HD
GLA_KW = GLA_HEADS * GLA_DK
GLA_VW = GLA_HEADS * GLA_DV
OFF_MQ = 0
OFF_MK = OFF_MQ + MOBA_QW
OFF_MV = OFF_MK + MOBA_KVW
OFF_MG = OFF_MV + MOBA_KVW
OFF_GQ = OFF_MG + MOBA_QW
OFF_GK = OFF_GQ + GLA_KW
OFF_GV = OFF_GK + GLA_KW
OFF_GG = OFF_GV + GLA_VW
OFF_GA = OFF_GG + GLA_VW
MAIN_W = OFF_GA

LANE = 128
BF16_SUBLANES = 16
VMEM_LIMIT_BYTES = 56 * 1024 * 1024

_NT = (((1,), (1,)), ((), ()))
_TN = (((0,), (0,)), ((), ()))


def _params(*sem):
    return pltpu.CompilerParams(dimension_semantics=sem, vmem_limit_bytes=VMEM_LIMIT_BYTES)


def _silu(x):
    return x * (1.0 / (1.0 + jnp.exp(-x)))


def _in_proj_kernel(x_ref, wt_ref, o_ref, wbf_ref):
    @pl.when(pl.program_id(1) == 0)
    def _cast_weight_tile():
        wbf_ref[...] = wt_ref[...].astype(BF16)

    o_ref[...] = lax.dot_general(x_ref[...], wbf_ref[...], _NT, preferred_element_type=F32)


def _in_proj(x, wt, n_out, tm, tn):
    m, k = x.shape
    assert n_out % tn == 0 and m % tm == 0 and wt.shape[0] >= n_out
    return pl.pallas_call(
        _in_proj_kernel,
        grid=(n_out // tn, m // tm),
        in_specs=[pl.BlockSpec((tm, k), lambda j, i: (i, 0)),
                  pl.BlockSpec((tn, k), lambda j, i: (j, 0))],
        out_specs=pl.BlockSpec((tm, tn), lambda j, i: (i, j)),
        out_shape=jax.ShapeDtypeStruct((m, n_out), F32),
        scratch_shapes=[pltpu.VMEM((tn, k), BF16)],
        compiler_params=_params("arbitrary", "arbitrary"),
        name="in_proj",
    )(x, wt)


def _log_decay_kernel(x_ref, wlr_ref, wa2_ref, ba_ref, o_ref, *, n_prompt_tiles, tm):
    ga = lax.dot_general(x_ref[...], wlr_ref[...], _NT, preferred_element_type=F32)
    z = jnp.dot(ga, wa2_ref[...], precision=lax.Precision.HIGHEST,
                preferred_element_type=F32) + ba_ref[...]
    log_a = (jnp.minimum(z, 0.0) - jnp.log(1.0 + jnp.exp(-jnp.abs(z)))) * (1.0 / GLA_GATE_TAU)
    r_i = lax.broadcasted_iota(jnp.int32, (tm, tm), 0)
    c_i = lax.broadcasted_iota(jnp.int32, (tm, tm), 1)
    in_chunk = jnp.logical_and(r_i // GLA_SUB == c_i // GLA_SUB, c_i <= r_i)
    prompt_tile = pl.program_id(0) < n_prompt_tiles
    mix = jnp.where(prompt_tile, jnp.where(in_chunk, 1.0, 0.0), jnp.where(r_i == c_i, 1.0, 0.0)).astype(F32)
    o_ref[...] = jnp.dot(mix, log_a, precision=lax.Precision.HIGHEST, preferred_element_type=F32)


def _log_decay(x, w_lr, w_a2p, b_a, n_prompt_rows, tm):
    m, k = x.shape
    assert m % tm == 0 and n_prompt_rows % tm == 0 and tm % GLA_SUB == 0
    return pl.pallas_call(
        functools.partial(_log_decay_kernel, n_prompt_tiles=n_prompt_rows // tm, tm=tm),
        grid=(m // tm,),
        in_specs=[pl.BlockSpec((tm, k), lambda i: (i, 0)),
                  pl.BlockSpec((LANE, k), lambda i: (0, 0)),
                  pl.BlockSpec((LANE, GLA_KW), lambda i: (0, 0)),
                  pl.BlockSpec((1, GLA_KW), lambda i: (0, 0))],
        out_specs=pl.BlockSpec((tm, GLA_KW), lambda i: (i, 0)),
        out_shape=jax.ShapeDtypeStruct((m, GLA_KW), F32),
        compiler_params=_params("arbitrary"),
        name="gla_log_decay",
    )(x, w_lr, w_a2p, b_a)


def _topk_bias(gate, n_valid, n_rows):
    blk = lax.broadcasted_iota(jnp.int32, gate.shape, 0)
    cur = jnp.where(blk < n_valid, gate, NEG_INF)
    sel = jnp.zeros(gate.shape, jnp.bool_)
    for _ in range(MOBA_TOPK):
        mx = jnp.max(cur, axis=0, keepdims=True)
        hit = jnp.logical_and(cur == mx, mx > NEG_INF)
        first = jnp.min(jnp.where(hit, blk, n_rows), axis=0, keepdims=True)
        pick = blk == first
        sel = jnp.logical_or(sel, pick)
        cur = jnp.where(pick, NEG_INF, cur)
    return jnp.where(sel, 0.0, MASK_BIAS)


def _moba_prompt_kernel(q_ref, k_ref, v_ref, g_ref, o_ref,
                        kaug_ref, vt_ref, kmean_ref, qaug_ref, m_ref, acc_ref, sa_ref, sb_ref,
                        *, n_blocks):
    qi = pl.program_id(1)
    blk_sz = MOBA_BLOCK
    hd = MOBA_HD
    rows = MOBA_GROUP * blk_sz

    @pl.when(qi == 0)
    def _prepare_kv_head():
        kmean_ref[...] = jnp.zeros(kmean_ref.shape, F32)
        vt_ref[hd:, :] = jnp.ones((vt_ref.shape[0] - hd, vt_ref.shape[1]), BF16)
        lane_blk = lax.broadcasted_iota(jnp.int32, (blk_sz, LANE), 1)

        def body(j, carry):
            r0 = pl.multiple_of(j * blk_sz, blk_sz)
            kb = k_ref[pl.ds(r0, blk_sz), :]
            kaug_ref[pl.ds(r0, blk_sz), 0:hd] = kb.astype(BF16)
            kaug_ref[pl.ds(r0, blk_sz), hd:hd + LANE] = jnp.where(lane_blk == j, 1.0, 0.0).astype(BF16)
            kmean_ref[pl.ds(j, 1), :] = jnp.sum(kb, axis=0, keepdims=True) * (1.0 / blk_sz)
            vt_ref[0:hd, pl.ds(r0, blk_sz)] = v_ref[pl.ds(r0, blk_sz), :].T.astype(BF16)
            return carry

        lax.fori_loop(0, n_blocks, body, 0)

    q = q_ref[...]
    qs = jnp.concatenate([q[:, hd * g:hd * (g + 1)] for g in range(MOBA_GROUP)], axis=0)
    gate = lax.dot_general(kmean_ref[...], qs, _NT, precision=lax.Precision.HIGHEST,
                           preferred_element_type=F32)
    nb_pad = kmean_ref.shape[0]
    bias = _topk_bias(gate, qi, nb_pad)
    bias = jnp.concatenate([bias, jnp.full((LANE - nb_pad, rows), MASK_BIAS, F32)], axis=0)
    qaug_ref[:, 0:hd] = (qs * (hd ** -0.5 * LOG2E)).astype(BF16)
    qaug_ref[:, hd:hd + LANE] = bias.T.astype(BF16)

    r_own = pl.multiple_of(qi * blk_sz, blk_sz)
    s = lax.dot_general(kaug_ref[pl.ds(r_own, blk_sz), 0:hd], qaug_ref[:, 0:hd], _NT,
                        preferred_element_type=F32)
    key_t = lax.broadcasted_iota(jnp.int32, s.shape, 0)
    row_t = lax.broadcasted_iota(jnp.int32, s.shape, 1) % blk_sz
    s = jnp.where(key_t <= row_t, s, NEG_INF)
    m0 = jnp.max(s, axis=0, keepdims=True)
    m_ref[...] = m0
    acc_ref[...] = jnp.dot(vt_ref[:, pl.ds(r_own, blk_sz)], jnp.exp2(s - m0).astype(BF16),
                           preferred_element_type=F32)

    def scores(r0):
        return lax.dot_general(kaug_ref[pl.ds(r0, blk_sz), :], qaug_ref[...], _NT,
                               preferred_element_type=F32)

    def accumulate(s, r0):
        m_prev = m_ref[...]
        m_new = jnp.maximum(m_prev, jnp.max(s, axis=0, keepdims=True))
        p = jnp.exp2(s - m_new).astype(BF16)
        pv = jnp.dot(vt_ref[:, pl.ds(r0, blk_sz)], p, preferred_element_type=F32)
        acc_ref[...] = jnp.exp2(m_prev - m_new) * acc_ref[...] + pv
        m_ref[...] = m_new

    n_pairs = (qi + 1) // 2

    @pl.when(n_pairs > 0)
    def _first_scores():
        sa_ref[...] = scores(0)

    def past_pair(jj, carry):
        r0 = pl.multiple_of(jj * (2 * blk_sz), 2 * blk_sz)
        r1 = pl.multiple_of(r0 + blk_sz, blk_sz)
        sb_ref[...] = scores(r1)
        accumulate(sa_ref[...], r0)
        r2 = pl.multiple_of(jnp.minimum(r0 + 2 * blk_sz, (n_blocks - 1) * blk_sz), blk_sz)
        sa_ref[...] = scores(r2)
        accumulate(sb_ref[...], r1)
        return carry

    lax.fori_loop(0, n_pairs, past_pair, 0)

    acc = acc_ref[...]
    o = (acc[0:hd, :] * (1.0 / acc[hd:hd + 1, :])).T
    o = jnp.concatenate([o[blk_sz * g:blk_sz * (g + 1), :] for g in range(MOBA_GROUP)], axis=1)
    o_ref[...] = (o * _silu(g_ref[...])).astype(o_ref.dtype)


def _moba_prompt(proj, t):
    assert t % (2 * MOBA_BLOCK) == 0
    nb = t // MOBA_BLOCK
    nb_pad = -(-nb // 8) * 8
    assert nb_pad <= LANE
    rows = MOBA_GROUP * MOBA_BLOCK
    gw = MOBA_GROUP * MOBA_HD
    return pl.pallas_call(
        functools.partial(_moba_prompt_kernel, n_blocks=nb),
        grid=(MOBA_KV_HEADS, nb),
        in_specs=[pl.BlockSpec((MOBA_BLOCK, gw), lambda h, i: (i, OFF_MQ // gw + h)),
                  pl.BlockSpec((t, MOBA_HD), lambda h, i: (0, OFF_MK // MOBA_HD + h)),
                  pl.BlockSpec((t, MOBA_HD), lambda h, i: (0, OFF_MV // MOBA_HD + h)),
                  pl.BlockSpec((MOBA_BLOCK, gw), lambda h, i: (i, OFF_MG // gw + h))],
        out_specs=pl.BlockSpec((MOBA_BLOCK, gw), lambda h, i: (i, h)),
        out_shape=jax.ShapeDtypeStruct((t, MOBA_QW), BF16),
        scratch_shapes=[pltpu.VMEM((t, MOBA_HD + LANE), BF16),
                        pltpu.VMEM((MOBA_HD + BF16_SUBLANES, t), BF16),
                        pltpu.VMEM((nb_pad, MOBA_HD), F32),
                        pltpu.VMEM((rows, MOBA_HD + LANE), BF16),
                        pltpu.VMEM((1, rows), F32),
                        pltpu.VMEM((MOBA_HD + BF16_SUBLANES, rows), F32),
                        pltpu.VMEM((MOBA_BLOCK, rows), F32),
                        pltpu.VMEM((MOBA_BLOCK, rows), F32)],
        compiler_params=_params("arbitrary", "arbitrary"),
        name="moba_prompt",
    )(proj, proj, proj, proj)


_GLA_GROUP = 4
_GLA_PAIRS_PER_STEP = 4


def _gla_prompt_kernel(q_ref, k_ref, b_ref, v_ref, gg_ref, gain_ref, o_ref, sout_ref,
                       s_ref, oacc_ref, *, n_tblocks, tb):
    ti = pl.program_id(1)
    sub = GLA_SUB
    kw = 2 * GLA_DK
    vw = 2 * GLA_DV
    npp = _GLA_PAIRS_PER_STEP

    @pl.when(ti == 0)
    def _zero_state():
        s_ref[...] = jnp.zeros(s_ref.shape, F32)

    row_h = lax.broadcasted_iota(jnp.int32, (kw, vw), 0) // GLA_DK
    col_h = lax.broadcasted_iota(jnp.int32, (kw, vw), 1) // GLA_DV
    same_head = row_h == col_h
    head_sum = jnp.where(same_head, 1.0, 0.0).astype(BF16)
    j_idx = lax.broadcasted_iota(jnp.int32, (sub, kw), 0)
    grp = _GLA_GROUP
    gt = grp * sub
    rs_i = lax.broadcasted_iota(jnp.int32, (gt, gt * sub), 0)
    rs_r = lax.broadcasted_iota(jnp.int32, (gt, gt * sub), 1) // sub
    row_sum = jnp.where(rs_i == rs_r, 1.0, 0.0).astype(BF16)

    def pair_group(q, k, b, v, state):
        w_rows, v_rows = [], []
        for s in range(grp):
            lo = s * sub
            b_s, k_s, v_s = b[lo:lo + sub, :], k[lo:lo + sub, :], v[lo:lo + sub, :]
            for i in range(sub):
                d = b_s[i:i + 1, :] - b_s
                dec = jnp.exp(jnp.where(j_idx <= i, d, NEG_INF))
                w_rows.append(dec * k_s * q[lo + i:lo + i + 1, :])
            v_rows.extend([v_s] * sub)
        w = jnp.concatenate(w_rows, axis=0).astype(BF16)
        sc = jnp.dot(w, head_sum, preferred_element_type=F32)
        z = (sc * jnp.concatenate(v_rows, axis=0)).astype(BF16)
        o_diag = jnp.dot(row_sum, z, preferred_element_type=F32)
        q_dec = (q * jnp.exp(b)).astype(BF16)
        upds, decays = [], []
        for s in range(grp):
            lo = s * sub
            b_s, k_s, v_s = b[lo:lo + sub, :], k[lo:lo + sub, :], v[lo:lo + sub, :]
            b_last = b_s[sub - 1:sub, :]
            k_dec = (k_s * jnp.exp(b_last - b_s)).astype(BF16)
            upd = lax.dot_general(k_dec, v_s.astype(BF16), _TN, preferred_element_type=F32)
            upds.append(jnp.where(same_head, upd, 0.0))
            e_col = jnp.broadcast_to(jnp.exp(b_last), (kw, kw)).T
            decays.append(jnp.concatenate([e_col, e_col], axis=1))
        o_inter = []
        for s in range(grp):
            lo = s * sub
            o_inter.append(jnp.dot(q_dec[lo:lo + sub, :], state.astype(BF16),
                                   preferred_element_type=F32))
            state = state * decays[s] + upds[s]
        return jnp.concatenate(o_inter, axis=0) + o_diag, state

    def group(g, carry):
        r0 = pl.multiple_of(g * gt, gt)
        for pp in range(npp):
            ks, vs = slice(kw * pp, kw * (pp + 1)), slice(vw * pp, vw * (pp + 1))
            o, state = pair_group(q_ref[pl.ds(r0, gt), ks] * (GLA_DK ** -0.5), k_ref[pl.ds(r0, gt), ks],
                                  b_ref[pl.ds(r0, gt), ks], v_ref[pl.ds(r0, gt), vs], s_ref[pp])
            s_ref[pp] = state
            oacc_ref[pl.ds(r0, gt), vs] = o
        return carry

    lax.fori_loop(0, tb // gt, group, 0)

    o = oacc_ref[...]
    gain = gain_ref[...]
    heads = []
    for h in range(2 * npp):
        oh = o[:, GLA_DV * h:GLA_DV * (h + 1)]
        ms = jnp.mean(oh * oh, axis=-1, keepdims=True)
        heads.append(oh * lax.rsqrt(ms + NORM_EPS) * gain)
    o = jnp.concatenate(heads, axis=1)
    o_ref[...] = (o * _silu(gg_ref[...])).astype(o_ref.dtype)

    @pl.when(ti == n_tblocks - 1)
    def _emit_state():
        for pp in range(npp):
            st = s_ref[pp]
            sout_ref[2 * pp] = st[0:GLA_DK, 0:GLA_DV]
            sout_ref[2 * pp + 1] = st[GLA_DK:2 * GLA_DK, GLA_DV:2 * GLA_DV]


def _gla_prompt(proj, decay, gain, t):
    tb = 256 if t % 256 == 0 else t
    assert t % tb == 0 and tb % (GLA_SUB * _GLA_GROUP) == 0
    nt = t // tb
    npp = _GLA_PAIRS_PER_STEP
    kw, vw = 2 * GLA_DK * npp, 2 * GLA_DV * npp
    return pl.pallas_call(
        functools.partial(_gla_prompt_kernel, n_tblocks=nt, tb=tb),
        grid=(GLA_PAIRS // npp, nt),
        in_specs=[pl.BlockSpec((tb, kw), lambda p, i: (i, OFF_GQ // kw + p)),
                  pl.BlockSpec((tb, kw), lambda p, i: (i, OFF_GK // kw + p)),
                  pl.BlockSpec((tb, kw), lambda p, i: (i, p)),
                  pl.BlockSpec((tb, vw), lambda p, i: (i, OFF_GV // vw + p)),
                  pl.BlockSpec((tb, vw), lambda p, i: (i, OFF_GG // vw + p)),
                  pl.BlockSpec((1, GLA_DV), lambda p, i: (0, 0))],
        out_specs=[pl.BlockSpec((tb, vw), lambda p, i: (i, p)),
                   pl.BlockSpec((2 * npp, GLA_DK, GLA_DV), lambda p, i: (p, 0, 0))],
        out_shape=[jax.ShapeDtypeStruct((t, GLA_VW), BF16),
                   jax.ShapeDtypeStruct((GLA_HEADS, GLA_DK, GLA_DV), F32)],
        scratch_shapes=[pltpu.VMEM((npp, 2 * GLA_DK, 2 * GLA_DV), F32),
                        pltpu.VMEM((tb, vw), F32)],
        compiler_params=_params("arbitrary", "arbitrary"),
        name="gla_prompt",
    )(proj, proj, decay, proj, proj, gain)


def _out_proj_kernel(og_ref, om_ref, wg_ref, wm_ref, x_ref, lg_ref, lb_ref, o_ref,
                     *, n_ctiles, tn, alpha):
    j = pl.program_id(1)
    mixed = jnp.dot(og_ref[...], wg_ref[...], preferred_element_type=F32)
    mixed = mixed + jnp.dot(om_ref[...], wm_ref[...], preferred_element_type=F32)
    c0 = pl.multiple_of(j * tn, tn)
    o_ref[:, pl.ds(c0, tn)] = alpha * x_ref[...] + mixed

    @pl.when(j == n_ctiles - 1)
    def _layer_norm():
        h = o_ref[...]
        mu = jnp.mean(h, axis=-1, keepdims=True)
        hc = h - mu
        var = jnp.mean(hc * hc, axis=-1, keepdims=True)
        o_ref[...] = hc * lax.rsqrt(var + NORM_EPS) * lg_ref[...] + lb_ref[...]


def _out_proj(og, om, w_out_bf, x, ln_g, ln_b, alpha, tm, tn):
    m, d = x.shape
    kg, km = og.shape[1], om.shape[1]
    assert kg == km and w_out_bf.shape[0] == kg + km
    nct = d // tn
    return pl.pallas_call(
        functools.partial(_out_proj_kernel, n_ctiles=nct, tn=tn, alpha=alpha),
        grid=(m // tm, nct),
        in_specs=[pl.BlockSpec((tm, kg), lambda i, j: (i, 0)),
                  pl.BlockSpec((tm, km), lambda i, j: (i, 0)),
                  pl.BlockSpec((kg, tn), lambda i, j: (0, j)),
                  pl.BlockSpec((km, tn), lambda i, j: (1, j)),
                  pl.BlockSpec((tm, tn), lambda i, j: (i, j)),
                  pl.BlockSpec((1, d), lambda i, j: (0, 0)),
                  pl.BlockSpec((1, d), lambda i, j: (0, 0))],
        out_specs=pl.BlockSpec((tm, d), lambda i, j: (i, 0)),
        out_shape=jax.ShapeDtypeStruct((m, d), F32),
        compiler_params=_params("arbitrary", "arbitrary"),
        name="out_proj_ln",
    )(og, om, w_out_bf, w_out_bf, x, ln_g, ln_b)


def _moba_decode_kernel(pt_ref, q_ref, kn_ref, vn_ref, g_ref, *rest, n_pages, page):
    del pt_ref
    k_refs = rest[:n_pages]
    v_refs = rest[n_pages:2 * n_pages]
    o_ref = rest[2 * n_pages]
    heads = MOBA_HEADS
    prow = page * MOBA_KV_HEADS
    brow = MOBA_BLOCK * MOBA_KV_HEADS
    n_blk = (n_pages * page) // MOBA_BLOCK

    q = q_ref[0] * (MOBA_HD ** -0.5)
    qb = q.astype(BF16)
    s = jnp.concatenate(
        [lax.dot_general(qb, k_refs[p][0, 0].astype(BF16), _NT, preferred_element_type=F32)
         for p in range(n_pages)], axis=1)
    lane = lax.broadcasted_iota(jnp.int32, s.shape, 1)
    head_kv = lax.broadcasted_iota(jnp.int32, s.shape, 0) // MOBA_GROUP
    own_kv = (lane % MOBA_KV_HEADS) == head_kv
    s_own = jnp.where(own_kv, s, 0.0)

    bm = jnp.concatenate(
        [jnp.broadcast_to(jnp.sum(s_own[:, brow * j:brow * (j + 1)], axis=1, keepdims=True)
                          * (1.0 / MOBA_BLOCK), (heads, brow)) for j in range(n_blk)], axis=1)
    blk = lane // brow
    cur = bm
    sel = jnp.zeros(s.shape, jnp.bool_)
    for _ in range(min(MOBA_TOPK, n_blk)):
        mx = jnp.max(cur, axis=1, keepdims=True)
        first = jnp.min(jnp.where(cur == mx, blk, n_blk), axis=1, keepdims=True)
        pick = blk == first
        sel = jnp.logical_or(sel, pick)
        cur = jnp.where(pick, NEG_INF, cur)

    kn = kn_ref[0]
    vn = vn_ref[0]
    kn16 = jnp.concatenate([jnp.broadcast_to(kn[i:i + 1, :], (MOBA_GROUP, MOBA_HD))
                            for i in range(MOBA_KV_HEADS)], axis=0)
    vn16 = jnp.concatenate([jnp.broadcast_to(vn[i:i + 1, :], (MOBA_GROUP, MOBA_HD))
                            for i in range(MOBA_KV_HEADS)], axis=0)
    s_new = jnp.sum(q * kn16, axis=1, keepdims=True)
    sm = jnp.where(jnp.logical_and(sel, own_kv), s, NEG_INF)
    m = jnp.maximum(jnp.max(sm, axis=1, keepdims=True), s_new)
    p = jnp.exp(sm - m)
    p_new = jnp.exp(s_new - m)
    denom = jnp.sum(p, axis=1, keepdims=True) + p_new
    pb = p.astype(BF16)
    o = p_new * vn16
    for pg in range(n_pages):
        o = o + jnp.dot(pb[:, prow * pg:prow * (pg + 1)], v_refs[pg][0, 0].astype(BF16),
                        preferred_element_type=F32)
    o = o * (1.0 / denom)
    o_ref[0] = (o * _silu(g_ref[0])).astype(o_ref.dtype)


def _moba_decode(page_table, q3, kn3, vn3, g3, cache_k, cache_v, layer):
    db, n_pages = page_table.shape
    depth, n_pool, page = cache_k.shape[0], cache_k.shape[1], cache_k.shape[2]
    assert (n_pages * page) % MOBA_BLOCK == 0
    prow = page * MOBA_KV_HEADS
    ck = cache_k.reshape(depth, n_pool, prow, MOBA_HD)
    cv = cache_v.reshape(depth, n_pool, prow, MOBA_HD)

    def page_spec(pg):
        return pl.BlockSpec((1, 1, prow, MOBA_HD), lambda b, pt: (layer, pt[b, pg], 0, 0))

    def row_spec(r):
        return pl.BlockSpec((1, r, MOBA_HD), lambda b, pt: (b, 0, 0))

    grid_spec = pltpu.PrefetchScalarGridSpec(
        num_scalar_prefetch=1,
        grid=(db,),
        in_specs=[row_spec(MOBA_HEADS), row_spec(MOBA_KV_HEADS), row_spec(MOBA_KV_HEADS),
                  row_spec(MOBA_HEADS)]
                 + [page_spec(pg) for pg in range(n_pages)]
                 + [page_spec(pg) for pg in range(n_pages)],
        out_specs=pl.BlockSpec((1, MOBA_HEADS, MOBA_HD), lambda b, pt: (b, 0, 0)),
    )
    return pl.pallas_call(
        functools.partial(_moba_decode_kernel, n_pages=n_pages, page=page),
        grid_spec=grid_spec,
        out_shape=jax.ShapeDtypeStruct((db, MOBA_HEADS, MOBA_HD), BF16),
        compiler_params=_params("arbitrary"),
        name="moba_decode",
    )(page_table, q3, kn3, vn3, g3, *([ck] * n_pages), *([cv] * n_pages))


_GLA_DEC_GROUP = 8


def _gla_decode_kernel(q_ref, k_ref, la_ref, v_ref, gg_ref, gain_ref, s_ref, o_ref, sn_ref):
    grp = _GLA_DEC_GROUP

    def columns(x):
        xp = jnp.concatenate([x, jnp.zeros((LANE - grp, x.shape[1]), F32)], axis=0)
        return xp.T

    a_t = columns(jnp.exp(la_ref[...]))
    k_t = columns(k_ref[...])
    q_t = columns(q_ref[...] * (GLA_DK ** -0.5))
    gain = gain_ref[...]
    for i in range(grp):
        rows = []
        for h in range(GLA_HEADS):
            r = slice(GLA_DK * h, GLA_DK * (h + 1))
            v_h = jnp.broadcast_to(v_ref[i, h:h + 1, :], (GLA_DK, GLA_DV))
            s_new = a_t[r, i:i + 1] * s_ref[i, h] + k_t[r, i:i + 1] * v_h
            sn_ref[i, h] = s_new
            rows.append(jnp.sum(q_t[r, i:i + 1] * s_new, axis=0, keepdims=True))
        o = jnp.concatenate(rows, axis=0)
        ms = jnp.mean(o * o, axis=-1, keepdims=True)
        o = o * lax.rsqrt(ms + NORM_EPS) * gain
        o_ref[i] = (o * _silu(gg_ref[i])).astype(o_ref.dtype)


def _gla_decode(q2, k2, la2, v3, gg3, gain, state):
    db = q2.shape[0]
    grp = _GLA_DEC_GROUP
    assert db % grp == 0
    vec = pl.BlockSpec((grp, GLA_KW), lambda g: (g, 0))
    hd3 = pl.BlockSpec((grp, GLA_HEADS, GLA_DV), lambda g: (g, 0, 0))
    st = pl.BlockSpec((grp, GLA_HEADS, GLA_DK, GLA_DV), lambda g: (g, 0, 0, 0))
    return pl.pallas_call(
        _gla_decode_kernel,
        grid=(db // grp,),
        in_specs=[vec, vec, vec, hd3, hd3, pl.BlockSpec((1, GLA_DV), lambda g: (0, 0)), st],
        out_specs=[hd3, st],
        out_shape=[jax.ShapeDtypeStruct((db, GLA_HEADS, GLA_DV), BF16),
                   jax.ShapeDtypeStruct(state.shape, F32)],
        compiler_params=_params("arbitrary"),
        name="gla_decode",
    )(q2, k2, la2, v3, gg3, gain, state)


def _tile(n, prefs):
    for p in prefs:
        if n % p == 0:
            return p
    return n


def kernel(x_prompt, x_sample, cache_k, cache_v, state_gla, page_table,
           w_in, w_a2, b_a, gla_gain, w_out, ln_g, ln_b):
    bsz, t, d = x_prompt.shape
    db, dt, _ = x_sample.shape
    depth = w_in.shape[0]
    assert bsz == 1 and dt == 1
    alpha = (2.0 * depth) ** 0.25
    y_p = x_prompt.reshape(t, d)
    y_s = x_sample.reshape(db, d)
    kp_l, vp_l, sp_l, ks_l, vs_l, ss_l = [], [], [], [], [], []
    for l in range(depth):
        w_in_t = jnp.swapaxes(w_in[l], 0, 1)
        w_lr_t = jnp.zeros((LANE, d), BF16).at[:GLA_LOWRANK, :].set(w_in_t[OFF_GA:].astype(BF16))
        w_a2p = jnp.zeros((LANE, GLA_KW), F32).at[:GLA_LOWRANK, :].set(w_a2[l])
        w_out_bf = w_out[l].astype(BF16)
        gain = gla_gain[l].reshape(1, GLA_DV)
        lg, lb = ln_g[l].reshape(1, d), ln_b[l].reshape(1, d)
        x_all = jnp.concatenate([y_p, y_s], axis=0).astype(BF16)
        m_all = t + db
        proj = _in_proj(x_all, w_in_t, MAIN_W, _tile(m_all, (640, 512, 256, 128)), 512)
        decay = _log_decay(x_all, w_lr_t, w_a2p, b_a[l].reshape(1, GLA_KW), t,
                           _tile(math.gcd(t, db), (128,)))
        o_m = _moba_prompt(proj, t)
        o_g, s_p = _gla_prompt(proj, decay, gain, t)
        kp_l.append(proj[:t, OFF_MK:OFF_MV].reshape(bsz, t, MOBA_KV_HEADS, MOBA_HD))
        vp_l.append(proj[:t, OFF_MV:OFF_MG].reshape(bsz, t, MOBA_KV_HEADS, MOBA_HD))
        sp_l.append(s_p.reshape(bsz, GLA_HEADS, GLA_DK, GLA_DV))
        y_p_new = _out_proj(o_g, o_m, w_out_bf, y_p, lg, lb, alpha,
                            _tile(t, (512, 256, 128)), _tile(d, (1024, 512, 256, 128)))
        proj_s = proj[t:]
        k_new = proj_s[:, OFF_MK:OFF_MV].reshape(db, MOBA_KV_HEADS, MOBA_HD)
        v_new = proj_s[:, OFF_MV:OFF_MG].reshape(db, MOBA_KV_HEADS, MOBA_HD)
        o_m_s = _moba_decode(page_table,
                             proj_s[:, OFF_MQ:OFF_MK].reshape(db, MOBA_HEADS, MOBA_HD),
                             k_new, v_new,
                             proj_s[:, OFF_MG:OFF_GQ].reshape(db, MOBA_HEADS, MOBA_HD),
                             cache_k, cache_v, l)
        o_g_s, s_s = _gla_decode(proj_s[:, OFF_GQ:OFF_GK], proj_s[:, OFF_GK:OFF_GV], decay[t:],
                                 proj_s[:, OFF_GV:OFF_GG].reshape(db, GLA_HEADS, GLA_DV),
                                 proj_s[:, OFF_GG:OFF_GA].reshape(db, GLA_HEADS, GLA_DV),
                                 gain, state_gla[l])
        ks_l.append(k_new.reshape(db, dt, MOBA_KV_HEADS, MOBA_HD))
        vs_l.append(v_new.reshape(db, dt, MOBA_KV_HEADS, MOBA_HD))
        ss_l.append(s_s)
        y_s = _out_proj(o_g_s.reshape(db, GLA_VW), o_m_s.reshape(db, MOBA_QW), w_out_bf, y_s, lg, lb,
                        alpha, _tile(db, (128,)), _tile(d, (1024, 512, 256, 128)))
        y_p = y_p_new
    return (y_p.reshape(bsz, t, d), y_s.reshape(db, dt, d),
            jnp.stack(kp_l), jnp.stack(vp_l), jnp.stack(sp_l),
            jnp.stack(ks_l), jnp.stack(vs_l), jnp.stack(ss_l))
```

```python
import functools
import math

import jax
import jax.numpy as jnp
from jax import lax
from jax.experimental import pallas as pl
from jax.experimental.pallas import tpu as pltpu

F32 = jnp.float32
BF16 = jnp.bfloat16
NEG_INF = float("-inf")
MASK_BIAS = -1e30
LOG2E = 1.4426950408889634

MOBA_HEADS = 16
MOBA_KV_HEADS = 4
MOBA_GROUP = MOBA_HEADS // MOBA_KV_HEADS
MOBA_HD = 128
MOBA_BLOCK = 256
MOBA_TOPK = 3
GLA_HEADS = 16
GLA_DK = 64
GLA_DV = 128
GLA_LOWRANK = 16
GLA_GATE_TAU = 16.0
GLA_SUB = 16
GLA_PAIRS = GLA_HEADS // 2
NORM_EPS = 1e-5

MOBA_QW = MOBA_HEADS * MOBA_HD
MOBA_KVW = MOBA_KV_HEADS * MOBA_HD
GLA_KW = GLA_HEADS * GLA_DK
GLA_VW = GLA_HEADS * GLA_DV
OFF_MQ = 0
OFF_MK = OFF_MQ + MOBA_QW
OFF_MV = OFF_MK + MOBA_KVW
OFF_MG = OFF_MV + MOBA_KVW
OFF_GQ = OFF_MG + MOBA_QW
OFF_GK = OFF_GQ + GLA_KW
OFF_GV = OFF_GK + GLA_KW
OFF_GG = OFF_GV + GLA_VW
OFF_GA = OFF_GG + GLA_VW
MAIN_W = OFF_GA

LANE = 128
BF16_SUBLANES = 16
VMEM_LIMIT_BYTES = 56 * 1024 * 1024

_NT = (((1,), (1,)), ((), ()))
_TN = (((0,), (0,)), ((), ()))


def _params(*sem):
    return pltpu.CompilerParams(dimension_semantics=sem, vmem_limit_bytes=VMEM_LIMIT_BYTES)


def _silu(x):
    return x * (1.0 / (1.0 + jnp.exp(-x)))


def _split_bf16(x):
    hi = x.astype(BF16)
    return hi, (x - hi.astype(F32)).astype(BF16)


def _stack_cast_kernel(xp_ref, xs_ref, o_ref, *, n_prompt_tiles):
    i = pl.program_id(0)

    @pl.when(i < n_prompt_tiles)
    def _prompt_rows():
        o_ref[...] = xp_ref[...].astype(o_ref.dtype)

    @pl.when(i >= n_prompt_tiles)
    def _decode_rows():
        o_ref[...] = xs_ref[...].astype(o_ref.dtype)


def _stack_cast(xp, xs, tr):
    (t, d), db = xp.shape, xs.shape[0]
    assert t % tr == 0 and db % tr == 0
    npt, nst = t // tr, db // tr
    return pl.pallas_call(
        functools.partial(_stack_cast_kernel, n_prompt_tiles=npt),
        grid=(npt + nst,),
        in_specs=[pl.BlockSpec((tr, d), lambda i: (jnp.minimum(i, npt - 1), 0)),
                  pl.BlockSpec((tr, d), lambda i: (jnp.maximum(i - npt, 0), 0))],
        out_specs=pl.BlockSpec((tr, d), lambda i: (i, 0)),
        out_shape=jax.ShapeDtypeStruct((t + db, d), BF16),
        compiler_params=_params("arbitrary"),
        name="stack_cast",
    )(xp, xs)


def _in_proj_kernel(x_ref, wt_ref, o_ref, wbf_ref):
    @pl.when(pl.program_id(1) == 0)
    def _cast_weight_tile():
        wbf_ref[...] = wt_ref[...].astype(BF16)

    o_ref[...] = lax.dot_general(x_ref[...], wbf_ref[...], _NT, preferred_element_type=F32)


def _in_proj(x, wt, n_out, tm, tn):
    m, k = x.shape
    assert n_out % tn == 0 and m % tm == 0 and wt.shape[0] >= n_out
    return pl.pallas_call(
        _in_proj_kernel,
        grid=(n_out // tn, m // tm),
        in_specs=[pl.BlockSpec((tm, k), lambda j, i: (i, 0)),
                  pl.BlockSpec((tn, k), lambda j, i: (j, 0))],
        out_specs=pl.BlockSpec((tm, tn), lambda j, i: (i, j)),
        out_shape=jax.ShapeDtypeStruct((m, n_out), F32),
        scratch_shapes=[pltpu.VMEM((tn, k), BF16)],
        compiler_params=_params("arbitrary", "arbitrary"),
        name="in_proj",
    )(x, wt)


def _kv_rows_kernel(k_ref, v_ref, ko_ref, vo_ref):
    tok = k_ref.shape[0]
    for h in range(MOBA_KV_HEADS):
        cols = slice(MOBA_HD * h, MOBA_HD * (h + 1))
        ko_ref[pl.ds(h, tok, stride=MOBA_KV_HEADS), :] = k_ref[:, cols]
        vo_ref[pl.ds(h, tok, stride=MOBA_KV_HEADS), :] = v_ref[:, cols]


def _kv_rows(proj, t, tr):
    assert t % tr == 0
    out = jax.ShapeDtypeStruct((t * MOBA_KV_HEADS, MOBA_HD), F32)
    return pl.pallas_call(
        _kv_rows_kernel,
        grid=(t // tr,),
        in_specs=[pl.BlockSpec((tr, MOBA_KVW), lambda i: (i, OFF_MK // MOBA_KVW)),
                  pl.BlockSpec((tr, MOBA_KVW), lambda i: (i, OFF_MV // MOBA_KVW))],
        out_specs=[pl.BlockSpec((tr * MOBA_KV_HEADS, MOBA_HD), lambda i: (i, 0)),
                   pl.BlockSpec((tr * MOBA_KV_HEADS, MOBA_HD), lambda i: (i, 0))],
        out_shape=[out, out],
        compiler_params=_params("arbitrary"),
        name="kv_rows",
    )(proj, proj)


def _log_decay_kernel(x_ref, wlr_ref, wa2_ref, ba_ref, o_ref, *, cumulative):
    ga = lax.dot_general(x_ref[...], wlr_ref[...], _NT, preferred_element_type=F32)
    ga_hi, ga_lo = _split_bf16(ga)
    wa_hi, wa_lo = _split_bf16(wa2_ref[...])
    z = (jnp.dot(ga_hi, wa_hi, preferred_element_type=F32)
         + (jnp.dot(ga_lo, wa_hi, preferred_element_type=F32)
            + jnp.dot(ga_hi, wa_lo, preferred_element_type=F32))) + ba_ref[...]
    log_a = (jnp.minimum(z, 0.0) - jnp.log(1.0 + jnp.exp(-jnp.abs(z)))) * (1.0 / GLA_GATE_TAU)
    if cumulative:
        pos = lax.broadcasted_iota(jnp.int32, log_a.shape, 0) % GLA_SUB
        step = 1
        while step < GLA_SUB:
            log_a = log_a + jnp.where(pos >= step, pltpu.roll(log_a, step, axis=0), 0.0)
            step *= 2
    o_ref[...] = log_a


def _log_decay(x, w_lr, w_a2p, b_a, row0, n_rows, tm, cumulative):
    k = x.shape[1]
    assert row0 % tm == 0 and n_rows % tm == 0 and tm % GLA_SUB == 0
    t0 = row0 // tm
    return pl.pallas_call(
        functools.partial(_log_decay_kernel, cumulative=cumulative),
        grid=(n_rows // tm,),
        in_specs=[pl.BlockSpec((tm, k), lambda i: (t0 + i, 0)),
                  pl.BlockSpec((LANE, k), lambda i: (0, 0)),
                  pl.BlockSpec((LANE, GLA_KW), lambda i: (0, 0)),
                  pl.BlockSpec((1, GLA_KW), lambda i: (0, 0))],
        out_specs=pl.BlockSpec((tm, GLA_KW), lambda i: (i, 0)),
        out_shape=jax.ShapeDtypeStruct((n_rows, GLA_KW), F32),
        compiler_params=_params("arbitrary"),
        name="gla_log_decay",
    )(x, w_lr, w_a2p, b_a)


def _topk_bias(gate, n_valid, n_rows):
    blk = lax.broadcasted_iota(jnp.int32, gate.shape, 0)
    cur = jnp.where(blk < n_valid, gate, NEG_INF)
    sel = jnp.zeros(gate.shape, jnp.bool_)
    for _ in range(MOBA_TOPK):
        mx = jnp.max(cur, axis=0, keepdims=True)
        hit = jnp.logical_and(cur == mx, mx > NEG_INF)
        first = jnp.min(jnp.where(hit, blk, n_rows), axis=0, keepdims=True)
        pick = blk == first
        sel = jnp.logical_or(sel, pick)
        cur = jnp.where(pick, NEG_INF, cur)
    return jnp.where(sel, 0.0, MASK_BIAS)


def _moba_prompt_kernel(q_ref, k_ref, v_ref, g_ref, o_ref,
                        kaug_ref, vt_ref, kmean_ref, qaug_ref, m_ref, acc_ref, sa_ref, sb_ref,
                        *, n_blocks):
    qi = pl.program_id(1)
    blk_sz = MOBA_BLOCK
    hd = MOBA_HD
    rows = MOBA_GROUP * blk_sz

    @pl.when(qi == 0)
    def _prepare_kv_head():
        kmean_ref[...] = jnp.zeros(kmean_ref.shape, F32)
        vt_ref[hd:, :] = jnp.ones((vt_ref.shape[0] - hd, vt_ref.shape[1]), BF16)
        lane_blk = lax.broadcasted_iota(jnp.int32, (blk_sz, LANE), 1)

        def body(j, carry):
            r0 = pl.multiple_of(j * blk_sz, blk_sz)
            kb = k_ref[pl.ds(r0, blk_sz), :]
            kaug_ref[pl.ds(r0, blk_sz), 0:hd] = kb.astype(BF16)
            kaug_ref[pl.ds(r0, blk_sz), hd:hd + LANE] = jnp.where(lane_blk == j, 1.0, 0.0).astype(BF16)
            kmean_ref[pl.ds(j, 1), :] = jnp.sum(kb, axis=0, keepdims=True) * (1.0 / blk_sz)
            vt_ref[0:hd, pl.ds(r0, blk_sz)] = v_ref[pl.ds(r0, blk_sz), :].T.astype(BF16)
            return carry

        lax.fori_loop(0, n_blocks, body, 0)

    q = q_ref[...]
    qs = jnp.concatenate([q[:, hd * g:hd * (g + 1)] for g in range(MOBA_GROUP)], axis=0)
    gate = lax.dot_general(kmean_ref[...], qs, _NT, precision=lax.Precision.HIGHEST,
                           preferred_element_type=F32)
    nb_pad = kmean_ref.shape[0]
    bias = _topk_bias(gate, qi, nb_pad)
    bias = jnp.concatenate([bias, jnp.full((LANE - nb_pad, rows), MASK_BIAS, F32)], axis=0)
    qaug_ref[:, 0:hd] = (qs * (hd ** -0.5 * LOG2E)).astype(BF16)
    qaug_ref[:, hd:hd + LANE] = bias.T.astype(BF16)

    def scores(r0):
        return lax.dot_general(kaug_ref[pl.ds(r0, blk_sz), :], qaug_ref[...], _NT,
                               preferred_element_type=F32)

    def accumulate(s, r0):
        m_prev = m_ref[...]
        m_new = jnp.maximum(m_prev, jnp.max(s, axis=0, keepdims=True))
        p = jnp.exp2(s - m_new).astype(BF16)
        pv = jnp.dot(vt_ref[:, pl.ds(r0, blk_sz)], p, preferred_element_type=F32)
        acc_ref[...] = jnp.exp2(m_prev - m_new) * acc_ref[...] + pv
        m_ref[...] = m_new

    sa_ref[...] = scores(0)

    r_own = pl.multiple_of(qi * blk_sz, blk_sz)
    s = lax.dot_general(kaug_ref[pl.ds(r_own, blk_sz), 0:hd], qaug_ref[:, 0:hd], _NT,
                        preferred_element_type=F32)
    key_t = lax.broadcasted_iota(jnp.int32, s.shape, 0)
    row_t = lax.broadcasted_iota(jnp.int32, s.shape, 1) % blk_sz
    s = jnp.where(key_t <= row_t, s, NEG_INF)
    m0 = jnp.max(s, axis=0, keepdims=True)
    m_ref[...] = m0
    acc_ref[...] = jnp.dot(vt_ref[:, pl.ds(r_own, blk_sz)], jnp.exp2(s - m0).astype(BF16),
                           preferred_element_type=F32)

    n_pairs = (qi + 1) // 2

    def past_pair(jj, carry):
        r0 = pl.multiple_of(jj * (2 * blk_sz), 2 * blk_sz)
        r1 = pl.multiple_of(r0 + blk_sz, blk_sz)
        sb_ref[...] = scores(r1)
        accumulate(sa_ref[...], r0)
        r2 = pl.multiple_of(jnp.minimum(r0 + 2 * blk_sz, (n_blocks - 1) * blk_sz), blk_sz)
        sa_ref[...] = scores(r2)
        accumulate(sb_ref[...], r1)
        return carry

    lax.fori_loop(0, n_pairs, past_pair, 0)

    acc = acc_ref[...]
    o = (acc[0:hd, :] * (1.0 / acc[hd:hd + 1, :])).T
    o = jnp.concatenate([o[blk_sz * g:blk_sz * (g + 1), :] for g in range(MOBA_GROUP)], axis=1)
    o_ref[...] = (o * _silu(g_ref[...])).astype(o_ref.dtype)


def _moba_prompt(proj, t):
    assert t % (2 * MOBA_BLOCK) == 0
    nb = t // MOBA_BLOCK
    nb_pad = -(-nb // 8) * 8
    assert nb_pad <= LANE
    rows = MOBA_GROUP * MOBA_BLOCK
    gw = MOBA_GROUP * MOBA_HD
    return pl.pallas_call(
        functools.partial(_moba_prompt_kernel, n_blocks=nb),
        grid=(MOBA_KV_HEADS, nb),
        in_specs=[pl.BlockSpec((MOBA_BLOCK, gw), lambda h, i: (i, OFF_MQ // gw + h)),
                  pl.BlockSpec((t, MOBA_HD), lambda h, i: (0, OFF_MK // MOBA_HD + h)),
                  pl.BlockSpec((t, MOBA_HD), lambda h, i: (0, OFF_MV // MOBA_HD + h)),
                  pl.BlockSpec((MOBA_BLOCK, gw), lambda h, i: (i, OFF_MG // gw + h))],
        out_specs=pl.BlockSpec((MOBA_BLOCK, gw), lambda h, i: (i, h)),
        out_shape=jax.ShapeDtypeStruct((t, MOBA_QW), BF16),
        scratch_shapes=[pltpu.VMEM((t, MOBA_HD + LANE), BF16),
                        pltpu.VMEM((MOBA_HD + BF16_SUBLANES, t), BF16),
                        pltpu.VMEM((nb_pad, MOBA_HD), F32),
                        pltpu.VMEM((rows, MOBA_HD + LANE), BF16),
                        pltpu.VMEM((1, rows), F32),
                        pltpu.VMEM((MOBA_HD + BF16_SUBLANES, rows), F32),
                        pltpu.VMEM((MOBA_BLOCK, rows), F32),
                        pltpu.VMEM((MOBA_BLOCK, rows), F32)],
        compiler_params=_params("arbitrary", "arbitrary"),
        name="moba_prompt",
    )(proj, proj, proj, proj)


_GLA_GROUP = 4
_GLA_PAIRS_PER_STEP = 4


def _gla_prompt_kernel(q_ref, k_ref, b_ref, v_ref, gg_ref, gain_ref, o_ref, sout_ref,
                       s_ref, oacc_ref, *, n_tblocks, tb):
    ti = pl.program_id(1)
    sub = GLA_SUB
    kw = 2 * GLA_DK
    vw = 2 * GLA_DV
    npp = _GLA_PAIRS_PER_STEP

    @pl.when(ti == 0)
    def _zero_state():
        s_ref[...] = jnp.zeros(s_ref.shape, F32)

    row_h = lax.broadcasted_iota(jnp.int32, (kw, vw), 0) // GLA_DK
    col_h = lax.broadcasted_iota(jnp.int32, (kw, vw), 1) // GLA_DV
    same_head = row_h == col_h
    head_sum = jnp.where(same_head, 1.0, 0.0).astype(BF16)
    j_idx = lax.broadcasted_iota(jnp.int32, (sub, kw), 0)
    grp = _GLA_GROUP
    gt = grp * sub
    rs_i = lax.broadcasted_iota(jnp.int32, (gt, gt * sub), 0)
    rs_r = lax.broadcasted_iota(jnp.int32, (gt, gt * sub), 1) // sub
    row_sum = jnp.where(rs_i == rs_r, 1.0, 0.0).astype(BF16)

    def pair_group(q, k, b, v, state):
        w_rows, v_rows = [], []
        for s in range(grp):
            lo = s * sub
            b_s, k_s, v_s = b[lo:lo + sub, :], k[lo:lo + sub, :], v[lo:lo + sub, :]
            for i in range(sub):
                d = b_s[i:i + 1, :] - b_s
                dec = jnp.exp(jnp.where(j_idx <= i, d, NEG_INF))
                w_rows.append(dec * k_s * q[lo + i:lo + i + 1, :])
            v_rows.extend([v_s] * sub)
        w = jnp.concatenate(w_rows, axis=0).astype(BF16)
        sc = jnp.dot(w, head_sum, preferred_element_type=F32)
        z = (sc * jnp.concatenate(v_rows, axis=0)).astype(BF16)
        o_diag = jnp.dot(row_sum, z, preferred_element_type=F32)
        q_dec = (q * jnp.exp(b)).astype(BF16)
        upds, decays = [], []
        for s in range(grp):
            lo = s * sub
            b_s, k_s, v_s = b[lo:lo + sub, :], k[lo:lo + sub, :], v[lo:lo + sub, :]
            b_last = b_s[sub - 1:sub, :]
            k_dec = (k_s * jnp.exp(b_last - b_s)).astype(BF16)
            upd = lax.dot_general(k_dec, v_s.astype(BF16), _TN, preferred_element_type=F32)
            upds.append(jnp.where(same_head, upd, 0.0))
            e_col = jnp.broadcast_to(jnp.exp(b_last), (kw, kw)).T
            decays.append(jnp.concatenate([e_col, e_col], axis=1))
        o_inter = []
        for s in range(grp):
            lo = s * sub
            o_inter.append(jnp.dot(q_dec[lo:lo + sub, :], state.astype(BF16),
                                   preferred_element_type=F32))
            state = state * decays[s] + upds[s]
        return jnp.concatenate(o_inter, axis=0) + o_diag, state

    def group(g, carry):
        r0 = pl.multiple_of(g * gt, gt)
        for pp in range(npp):
            ks, vs = slice(kw * pp, kw * (pp + 1)), slice(vw * pp, vw * (pp + 1))
            o, state = pair_group(q_ref[pl.ds(r0, gt), ks] * (GLA_DK ** -0.5), k_ref[pl.ds(r0, gt), ks],
                                  b_ref[pl.ds(r0, gt), ks], v_ref[pl.ds(r0, gt), vs], s_ref[pp])
            s_ref[pp] = state
            oacc_ref[pl.ds(r0, gt), vs] = o
        return carry

    lax.fori_loop(0, tb // gt, group, 0)

    o = oacc_ref[...]
    gain = gain_ref[...]
    heads = []
    for h in range(2 * npp):
        oh = o[:, GLA_DV * h:GLA_DV * (h + 1)]
        ms = jnp.mean(oh * oh, axis=-1, keepdims=True)
        heads.append(oh * lax.rsqrt(ms + NORM_EPS) * gain)
    o = jnp.concatenate(heads, axis=1)
    o_ref[...] = (o * _silu(gg_ref[...])).astype(o_ref.dtype)

    @pl.when(ti == n_tblocks - 1)
    def _emit_state():
        for pp in range(npp):
            st = s_ref[pp]
            sout_ref[2 * pp] = st[0:GLA_DK, 0:GLA_DV]
            sout_ref[2 * pp + 1] = st[GLA_DK:2 * GLA_DK, GLA_DV:2 * GLA_DV]


def _gla_prompt(proj, decay, gain, t):
    tb = 256 if t % 256 == 0 else t
    assert t % tb == 0 and tb % (GLA_SUB * _GLA_GROUP) == 0
    nt = t // tb
    npp = _GLA_PAIRS_PER_STEP
    kw, vw = 2 * GLA_DK * npp, 2 * GLA_DV * npp
    return pl.pallas_call(
        functools.partial(_gla_prompt_kernel, n_tblocks=nt, tb=tb),
        grid=(GLA_PAIRS // npp, nt),
        in_specs=[pl.BlockSpec((tb, kw), lambda p, i: (i, OFF_GQ // kw + p)),
                  pl.BlockSpec((tb, kw), lambda p, i: (i, OFF_GK // kw + p)),
                  pl.BlockSpec((tb, kw), lambda p, i: (i, p)),
                  pl.BlockSpec((tb, vw), lambda p, i: (i, OFF_GV // vw + p)),
                  pl.BlockSpec((tb, vw), lambda p, i: (i, OFF_GG // vw + p)),
                  pl.BlockSpec((1, GLA_DV), lambda p, i: (0, 0))],
        out_specs=[pl.BlockSpec((tb, vw), lambda p, i: (i, p)),
                   pl.BlockSpec((2 * npp, GLA_DK, GLA_DV), lambda p, i: (p, 0, 0))],
        out_shape=[jax.ShapeDtypeStruct((t, GLA_VW), BF16),
                   jax.ShapeDtypeStruct((GLA_HEADS, GLA_DK, GLA_DV), F32)],
        scratch_shapes=[pltpu.VMEM((npp, 2 * GLA_DK, 2 * GLA_DV), F32),
                        pltpu.VMEM((tb, vw), F32)],
        compiler_params=_params("arbitrary", "arbitrary"),
        name="gla_prompt",
    )(proj, proj, decay, proj, proj, gain)


def _out_proj_kernel(og_ref, om_ref, wg_ref, wm_ref, x_ref, lg_ref, lb_ref, o_ref,
                     *, n_ctiles, tn, alpha):
    j = pl.program_id(1)
    mixed = jnp.dot(og_ref[...], wg_ref[...], preferred_element_type=F32)
    mixed = mixed + jnp.dot(om_ref[...], wm_ref[...], preferred_element_type=F32)
    c0 = pl.multiple_of(j * tn, tn)
    o_ref[:, pl.ds(c0, tn)] = alpha * x_ref[...] + mixed

    @pl.when(j == n_ctiles - 1)
    def _layer_norm():
        h = o_ref[...]
        mu = jnp.mean(h, axis=-1, keepdims=True)
        hc = h - mu
        var = jnp.mean(hc * hc, axis=-1, keepdims=True)
        o_ref[...] = hc * lax.rsqrt(var + NORM_EPS) * lg_ref[...] + lb_ref[...]


def _out_proj(og, om, w_out_bf, x, ln_g, ln_b, alpha, tm, tn):
    m, d = x.shape
    kg, km = og.shape[1], om.shape[1]
    assert kg == km and w_out_bf.shape[0] == kg + km
    nct = d // tn
    return pl.pallas_call(
        functools.partial(_out_proj_kernel, n_ctiles=nct, tn=tn, alpha=alpha),
        grid=(m // tm, nct),
        in_specs=[pl.BlockSpec((tm, kg), lambda i, j: (i, 0)),
                  pl.BlockSpec((tm, km), lambda i, j: (i, 0)),
                  pl.BlockSpec((kg, tn), lambda i, j: (0, j)),
                  pl.BlockSpec((km, tn), lambda i, j: (1, j)),
                  pl.BlockSpec((tm, tn), lambda i, j: (i, j)),
                  pl.BlockSpec((1, d), lambda i, j: (0, 0)),
                  pl.BlockSpec((1, d), lambda i, j: (0, 0))],
        out_specs=pl.BlockSpec((tm, d), lambda i, j: (i, 0)),
        out_shape=jax.ShapeDtypeStruct((m, d), F32),
        compiler_params=_params("arbitrary", "arbitrary"),
        name="out_proj_ln",
    )(og, om, w_out_bf, w_out_bf, x, ln_g, ln_b)


_MOBA_DEC_SEQS = 2


def _moba_decode_kernel(pt_ref, q_ref, kn_ref, vn_ref, g_ref, *rest, n_pages, page):
    del pt_ref
    n_seq = _MOBA_DEC_SEQS
    o_ref = rest[2 * n_seq * n_pages]
    for sq in range(n_seq):
        k_refs = rest[sq * n_pages:(sq + 1) * n_pages]
        v_refs = rest[(n_seq + sq) * n_pages:(n_seq + sq + 1) * n_pages]
        _moba_decode_one(q_ref.at[sq], kn_ref.at[sq], vn_ref.at[sq], g_ref.at[sq], k_refs, v_refs,
                         o_ref.at[sq], n_pages, page)


def _moba_decode_one(q_ref, kn_ref, vn_ref, g_ref, k_refs, v_refs, o_ref, n_pages, page):
    heads = MOBA_HEADS
    prow = page * MOBA_KV_HEADS
    brow = MOBA_BLOCK * MOBA_KV_HEADS
    n_blk = (n_pages * page) // MOBA_BLOCK

    q = q_ref[...] * (MOBA_HD ** -0.5)
    qb = q.astype(BF16)
    s = jnp.concatenate(
        [lax.dot_general(qb, k_refs[p][0, 0].astype(BF16), _NT, preferred_element_type=F32)
         for p in range(n_pages)], axis=1)
    lane = lax.broadcasted_iota(jnp.int32, s.shape, 1)
    head_kv = lax.broadcasted_iota(jnp.int32, s.shape, 0) // MOBA_GROUP
    own_kv = (lane % MOBA_KV_HEADS) == head_kv
    s_own = jnp.where(own_kv, s, 0.0)

    bm = jnp.concatenate(
        [jnp.broadcast_to(jnp.sum(s_own[:, brow * j:brow * (j + 1)], axis=1, keepdims=True)
                          * (1.0 / MOBA_BLOCK), (heads, brow)) for j in range(n_blk)], axis=1)
    blk = lane // brow
    cur = bm
    sel = jnp.zeros(s.shape, jnp.bool_)
    for _ in range(min(MOBA_TOPK, n_blk)):
        mx = jnp.max(cur, axis=1, keepdims=True)
        first = jnp.min(jnp.where(cur == mx, blk, n_blk), axis=1, keepdims=True)
        pick = blk == first
        sel = jnp.logical_or(sel, pick)
        cur = jnp.where(pick, NEG_INF, cur)

    kn = kn_ref[...]
    vn = vn_ref[...]
    kn16 = jnp.concatenate([jnp.broadcast_to(kn[i:i + 1, :], (MOBA_GROUP, MOBA_HD))
                            for i in range(MOBA_KV_HEADS)], axis=0)
    vn16 = jnp.concatenate([jnp.broadcast_to(vn[i:i + 1, :], (MOBA_GROUP, MOBA_HD))
                            for i in range(MOBA_KV_HEADS)], axis=0)
    s_new = jnp.sum(q * kn16, axis=1, keepdims=True)
    sm = jnp.where(jnp.logical_and(sel, own_kv), s, NEG_INF)
    m = jnp.maximum(jnp.max(sm, axis=1, keepdims=True), s_new)
    p = jnp.exp(sm - m)
    p_new = jnp.exp(s_new - m)
    denom = jnp.sum(p, axis=1, keepdims=True) + p_new
    pb = p.astype(BF16)
    o = p_new * vn16
    for pg in range(n_pages):
        o = o + jnp.dot(pb[:, prow * pg:prow * (pg + 1)], v_refs[pg][0, 0].astype(BF16),
                        preferred_element_type=F32)
    o = o * (1.0 / denom)
    o_ref[...] = (o * _silu(g_ref[...])).astype(o_ref.dtype)


def _moba_decode(page_table, q3, kn3, vn3, g3, cache_k, cache_v, layer):
    db, n_pages = page_table.shape
    depth, n_pool, page = cache_k.shape[0], cache_k.shape[1], cache_k.shape[2]
    assert (n_pages * page) % MOBA_BLOCK == 0
    prow = page * MOBA_KV_HEADS
    ck = cache_k.reshape(depth, n_pool, prow, MOBA_HD)
    cv = cache_v.reshape(depth, n_pool, prow, MOBA_HD)

    n_seq = _MOBA_DEC_SEQS
    assert db % n_seq == 0

    def page_spec(sq, pg):
        return pl.BlockSpec((1, 1, prow, MOBA_HD), lambda b, pt: (layer, pt[b * n_seq + sq, pg], 0, 0))

    def row_spec(r):
        return pl.BlockSpec((n_seq, r, MOBA_HD), lambda b, pt: (b, 0, 0))

    pages = [page_spec(sq, pg) for sq in range(n_seq) for pg in range(n_pages)]
    grid_spec = pltpu.PrefetchScalarGridSpec(
        num_scalar_prefetch=1,
        grid=(db // n_seq,),
        in_specs=[row_spec(MOBA_HEADS), row_spec(MOBA_KV_HEADS), row_spec(MOBA_KV_HEADS),
                  row_spec(MOBA_HEADS)] + pages + pages,
        out_specs=pl.BlockSpec((n_seq, MOBA_HEADS, MOBA_HD), lambda b, pt: (b, 0, 0)),
    )
    n_ops = n_seq * n_pages
    return pl.pallas_call(
        functools.partial(_moba_decode_kernel, n_pages=n_pages, page=page),
        grid_spec=grid_spec,
        out_shape=jax.ShapeDtypeStruct((db, MOBA_HEADS, MOBA_HD), BF16),
        compiler_params=_params("arbitrary"),
        name="moba_decode",
    )(page_table, q3, kn3, vn3, g3, *([ck] * n_ops), *([cv] * n_ops))


_GLA_DEC_GROUP = 8


def _gla_decode_kernel(q_ref, k_ref, la_ref, v_ref, gg_ref, gain_ref, s_ref, o_ref, sn_ref):
    grp = _GLA_DEC_GROUP

    def columns(x):
        xp = jnp.concatenate([x, jnp.zeros((LANE - grp, x.shape[1]), F32)], axis=0)
        return xp.T

    a_t = columns(jnp.exp(la_ref[...]))
    k_t = columns(k_ref[...])
    q_t = columns(q_ref[...] * (GLA_DK ** -0.5))
    gain = gain_ref[...]
    for i in range(grp):
        rows = []
        for h in range(GLA_HEADS):
            r = slice(GLA_DK * h, GLA_DK * (h + 1))
            v_h = jnp.broadcast_to(v_ref[i, h:h + 1, :], (GLA_DK, GLA_DV))
            s_new = a_t[r, i:i + 1] * s_ref[i, h] + k_t[r, i:i + 1] * v_h
            sn_ref[i, h] = s_new
            rows.append(jnp.sum(q_t[r, i:i + 1] * s_new, axis=0, keepdims=True))
        o = jnp.concatenate(rows, axis=0)
        ms = jnp.mean(o * o, axis=-1, keepdims=True)
        o = o * lax.rsqrt(ms + NORM_EPS) * gain
        o_ref[i] = (o * _silu(gg_ref[i])).astype(o_ref.dtype)


def _gla_decode(q2, k2, la2, v3, gg3, gain, state):
    db = q2.shape[0]
    grp = _GLA_DEC_GROUP
    assert db % grp == 0
    vec = pl.BlockSpec((grp, GLA_KW), lambda g: (g, 0))
    hd3 = pl.BlockSpec((grp, GLA_HEADS, GLA_DV), lambda g: (g, 0, 0))
    st = pl.BlockSpec((grp, GLA_HEADS, GLA_DK, GLA_DV), lambda g: (g, 0, 0, 0))
    return pl.pallas_call(
        _gla_decode_kernel,
        grid=(db // grp,),
        in_specs=[vec, vec, vec, hd3, hd3, pl.BlockSpec((1, GLA_DV), lambda g: (0, 0)), st],
        out_specs=[hd3, st],
        out_shape=[jax.ShapeDtypeStruct((db, GLA_HEADS, GLA_DV), BF16),
                   jax.ShapeDtypeStruct(state.shape, F32)],
        compiler_params=_params("arbitrary"),
        name="gla_decode",
    )(q2, k2, la2, v3, gg3, gain, state)


def _tile(n, prefs):
    for p in prefs:
        if n % p == 0:
            return p
    return n


def kernel(x_prompt, x_sample, cache_k, cache_v, state_gla, page_table,
           w_in, w_a2, b_a, gla_gain, w_out, ln_g, ln_b):
    bsz, t, d = x_prompt.shape
    db, dt, _ = x_sample.shape
    depth = w_in.shape[0]
    assert bsz == 1 and dt == 1
    alpha = (2.0 * depth) ** 0.25
    y_p = x_prompt.reshape(t, d)
    y_s = x_sample.reshape(db, d)
    kp_l, vp_l, sp_l, ks_l, vs_l, ss_l = [], [], [], [], [], []
    for l in range(depth):
        w_in_t = jnp.swapaxes(w_in[l], 0, 1)
        w_lr_t = jnp.zeros((LANE, d), BF16).at[:GLA_LOWRANK, :].set(w_in_t[OFF_GA:].astype(BF16))
        w_a2p = jnp.zeros((LANE, GLA_KW), F32).at[:GLA_LOWRANK, :].set(w_a2[l])
        w_out_bf = w_out[l].astype(BF16)
        gain = gla_gain[l].reshape(1, GLA_DV)
        lg, lb = ln_g[l].reshape(1, d), ln_b[l].reshape(1, d)
        x_all = _stack_cast(y_p, y_s, _tile(math.gcd(t, db), (128,)))
        m_all = t + db
        proj = _in_proj(x_all, w_in_t, MAIN_W, _tile(m_all, (1040, 640, 512, 256, 128)), 512)
        ba2 = b_a[l].reshape(1, GLA_KW)
        decay = _log_decay(x_all, w_lr_t, w_a2p, ba2, 0, t, _tile(math.gcd(t, 512), (512,)), True)
        log_a_s = _log_decay(x_all, w_lr_t, w_a2p, ba2, t, db, db, False)
        o_m = _moba_prompt(proj, t)
        o_g, s_p = _gla_prompt(proj, decay, gain, t)
        k_rows, v_rows = _kv_rows(proj, t, _tile(t, (512, 256, 128)))
        kp_l.append(k_rows.reshape(bsz, t, MOBA_KV_HEADS, MOBA_HD))
        vp_l.append(v_rows.reshape(bsz, t, MOBA_KV_HEADS, MOBA_HD))
        sp_l.append(s_p.reshape(bsz, GLA_HEADS, GLA_DK, GLA_DV))
        y_p_new = _out_proj(o_g, o_m, w_out_bf, y_p, lg, lb, alpha,
                            _tile(t, (512, 256, 128)), _tile(d, (1024, 512, 256, 128)))
        proj_s = proj[t:]
        k_new = proj_s[:, OFF_MK:OFF_MV].reshape(db, MOBA_KV_HEADS, MOBA_HD)
        v_new = proj_s[:, OFF_MV:OFF_MG].reshape(db, MOBA_KV_HEADS, MOBA_HD)
        o_m_s = _moba_decode(page_table,
                             proj_s[:, OFF_MQ:OFF_MK].reshape(db, MOBA_HEADS, MOBA_HD),
                             k_new, v_new,
                             proj_s[:, OFF_MG:OFF_GQ].reshape(db, MOBA_HEADS, MOBA_HD),
                             cache_k, cache_v, l)
        o_g_s, s_s = _gla_decode(proj_s[:, OFF_GQ:OFF_GK], proj_s[:, OFF_GK:OFF_GV], log_a_s,
                                 proj_s[:, OFF_GV:OFF_GG].reshape(db, GLA_HEADS, GLA_DV),
                                 proj_s[:, OFF_GG:OFF_GA].reshape(db, GLA_HEADS, GLA_DV),
                                 gain, state_gla[l])
        ks_l.append(k_new.reshape(db, dt, MOBA_KV_HEADS, MOBA_HD))
        vs_l.append(v_new.reshape(db, dt, MOBA_KV_HEADS, MOBA_HD))
        ss_l.append(s_s)
        y_s = _out_proj(o_g_s.reshape(db, GLA_VW), o_m_s.reshape(db, MOBA_QW), w_out_bf, y_s, lg, lb,
                        alpha, _tile(db, (128,)), _tile(d, (1024, 512, 256, 128)))
        y_p = y_p_new
    return (y_p.reshape(bsz, t, d), y_s.reshape(db, dt, d),
            jnp.stack(kp_l), jnp.stack(vp_l), jnp.stack(sp_l),
            jnp.stack(ks_l), jnp.stack(vs_l), jnp.stack(ss_l))
```

```python
import functools
import math

import jax
import jax.numpy as jnp
from jax import lax
from jax.experimental import pallas as pl
from jax.experimental.pallas import tpu as pltpu

F32 = jnp.float32
BF16 = jnp.bfloat16
NEG_INF = float("-inf")
MASK_BIAS = -1e30
LOG2E = 1.4426950408889634

MOBA_HEADS = 16
MOBA_KV_HEADS = 4
MOBA_GROUP = MOBA_HEADS // MOBA_KV_HEADS
MOBA_HD = 128
MOBA_BLOCK = 256
MOBA_TOPK = 3
GLA_HEADS = 16
GLA_DK = 64
GLA_DV = 128
GLA_LOWRANK = 16
GLA_GATE_TAU = 16.0
GLA_SUB = 16
GLA_PAIRS = GLA_HEADS // 2
NORM_EPS = 1e-5

MOBA_QW = MOBA_HEADS * MOBA_HD
MOBA_KVW = MOBA_KV_HEADS * MOBA_HD
GLA_KW = GLA_HEADS * GLA_DK
GLA_VW = GLA_HEADS * GLA_DV
OFF_MQ = 0
OFF_MK = OFF_MQ + MOBA_QW
OFF_MV = OFF_MK + MOBA_KVW
OFF_MG = OFF_MV + MOBA_KVW
OFF_GQ = OFF_MG + MOBA_QW
OFF_GK = OFF_GQ + GLA_KW
OFF_GV = OFF_GK + GLA_KW
OFF_GG = OFF_GV + GLA_VW
OFF_GA = OFF_GG + GLA_VW
MAIN_W = OFF_GA

LANE = 128
BF16_SUBLANES = 16
VMEM_LIMIT_BYTES = 60 * 1024 * 1024

_NT = (((1,), (1,)), ((), ()))
_TN = (((0,), (0,)), ((), ()))


def _params(*sem):
    return pltpu.CompilerParams(dimension_semantics=sem, vmem_limit_bytes=VMEM_LIMIT_BYTES)


def _silu(x):
    return x * (1.0 / (1.0 + jnp.exp(-x)))


def _split_bf16(x):
    hi = x.astype(BF16)
    return hi, (x - hi.astype(F32)).astype(BF16)


def _stack_cast_kernel(xp_ref, xs_ref, o_ref, *, n_prompt_tiles):
    i = pl.program_id(0)

    @pl.when(i < n_prompt_tiles)
    def _prompt_rows():
        o_ref[...] = xp_ref[...].astype(o_ref.dtype)

    @pl.when(i >= n_prompt_tiles)
    def _decode_rows():
        o_ref[...] = xs_ref[...].astype(o_ref.dtype)


def _stack_cast(xp, xs, tr):
    (t, d), db = xp.shape, xs.shape[0]
    assert t % tr == 0 and db % tr == 0
    npt, nst = t // tr, db // tr
    return pl.pallas_call(
        functools.partial(_stack_cast_kernel, n_prompt_tiles=npt),
        grid=(npt + nst,),
        in_specs=[pl.BlockSpec((tr, d), lambda i: (jnp.minimum(i, npt - 1), 0)),
                  pl.BlockSpec((tr, d), lambda i: (jnp.maximum(i - npt, 0), 0))],
        out_specs=pl.BlockSpec((tr, d), lambda i: (i, 0)),
        out_shape=jax.ShapeDtypeStruct((t + db, d), BF16),
        compiler_params=_params("arbitrary"),
        name="stack_cast",
    )(xp, xs)


def _in_proj_kernel(x_ref, wt_ref, o_ref, wbf_ref):
    @pl.when(pl.program_id(1) == 0)
    def _cast_weight_tile():
        wbf_ref[...] = wt_ref[...].astype(BF16)

    o_ref[...] = lax.dot_general(x_ref[...], wbf_ref[...], _NT, preferred_element_type=F32)


def _in_proj(x, wt, n_out, tm, tn):
    m, k = x.shape
    assert n_out % tn == 0 and m % tm == 0 and wt.shape[0] >= n_out
    return pl.pallas_call(
        _in_proj_kernel,
        grid=(n_out // tn, m // tm),
        in_specs=[pl.BlockSpec((tm, k), lambda j, i: (i, 0)),
                  pl.BlockSpec((tn, k), lambda j, i: (j, 0))],
        out_specs=pl.BlockSpec((tm, tn), lambda j, i: (i, j)),
        out_shape=jax.ShapeDtypeStruct((m, n_out), F32),
        scratch_shapes=[pltpu.VMEM((tn, k), BF16)],
        compiler_params=_params("arbitrary", "arbitrary"),
        name="in_proj",
    )(x, wt)


def _kv_rows_kernel(k_ref, v_ref, ko_ref, vo_ref):
    tok = k_ref.shape[0]
    for h in range(MOBA_KV_HEADS):
        cols = slice(MOBA_HD * h, MOBA_HD * (h + 1))
        ko_ref[pl.ds(h, tok, stride=MOBA_KV_HEADS), :] = k_ref[:, cols]
        vo_ref[pl.ds(h, tok, stride=MOBA_KV_HEADS), :] = v_ref[:, cols]


def _kv_rows(proj, t, tr):
    assert t % tr == 0
    out = jax.ShapeDtypeStruct((t * MOBA_KV_HEADS, MOBA_HD), F32)
    return pl.pallas_call(
        _kv_rows_kernel,
        grid=(t // tr,),
        in_specs=[pl.BlockSpec((tr, MOBA_KVW), lambda i: (i, OFF_MK // MOBA_KVW)),
                  pl.BlockSpec((tr, MOBA_KVW), lambda i: (i, OFF_MV // MOBA_KVW))],
        out_specs=[pl.BlockSpec((tr * MOBA_KV_HEADS, MOBA_HD), lambda i: (i, 0)),
                   pl.BlockSpec((tr * MOBA_KV_HEADS, MOBA_HD), lambda i: (i, 0))],
        out_shape=[out, out],
        compiler_params=_params("arbitrary"),
        name="kv_rows",
    )(proj, proj)


def _log_decay_kernel(x_ref, wlr_ref, wa2_ref, ba_ref, o_ref, *, cumulative):
    ga = lax.dot_general(x_ref[...], wlr_ref[...], _NT, preferred_element_type=F32)
    ga_hi, ga_lo = _split_bf16(ga)
    wa_hi, wa_lo = _split_bf16(wa2_ref[...])
    z = (jnp.dot(ga_hi, wa_hi, preferred_element_type=F32)
         + (jnp.dot(ga_lo, wa_hi, preferred_element_type=F32)
            + jnp.dot(ga_hi, wa_lo, preferred_element_type=F32))) + ba_ref[...]
    log_a = (jnp.minimum(z, 0.0) - jnp.log(1.0 + jnp.exp(-jnp.abs(z)))) * (1.0 / GLA_GATE_TAU)
    if cumulative:
        pos = lax.broadcasted_iota(jnp.int32, log_a.shape, 0) % GLA_SUB
        step = 1
        while step < GLA_SUB:
            log_a = log_a + jnp.where(pos >= step, pltpu.roll(log_a, step, axis=0), 0.0)
            step *= 2
    o_ref[...] = log_a


def _log_decay(x, w_lr, w_a2p, b_a, row0, n_rows, tm, cumulative):
    k = x.shape[1]
    assert row0 % tm == 0 and n_rows % tm == 0 and tm % GLA_SUB == 0
    t0 = row0 // tm
    return pl.pallas_call(
        functools.partial(_log_decay_kernel, cumulative=cumulative),
        grid=(n_rows // tm,),
        in_specs=[pl.BlockSpec((tm, k), lambda i: (t0 + i, 0)),
                  pl.BlockSpec((LANE, k), lambda i: (0, 0)),
                  pl.BlockSpec((LANE, GLA_KW), lambda i: (0, 0)),
                  pl.BlockSpec((1, GLA_KW), lambda i: (0, 0))],
        out_specs=pl.BlockSpec((tm, GLA_KW), lambda i: (i, 0)),
        out_shape=jax.ShapeDtypeStruct((n_rows, GLA_KW), F32),
        compiler_params=_params("arbitrary"),
        name="gla_log_decay",
    )(x, w_lr, w_a2p, b_a)


def _topk_bias(gate, n_valid, n_rows):
    blk = lax.broadcasted_iota(jnp.int32, gate.shape, 0)
    cur = jnp.where(blk < n_valid, gate, NEG_INF)
    sel = jnp.zeros(gate.shape, jnp.bool_)
    for _ in range(MOBA_TOPK):
        mx = jnp.max(cur, axis=0, keepdims=True)
        hit = jnp.logical_and(cur == mx, mx > NEG_INF)
        first = jnp.min(jnp.where(hit, blk, n_rows), axis=0, keepdims=True)
        pick = blk == first
        sel = jnp.logical_or(sel, pick)
        cur = jnp.where(pick, NEG_INF, cur)
    return jnp.where(sel, 0.0, MASK_BIAS)


def _moba_prompt_kernel(q_ref, k_ref, v_ref, g_ref, o_ref,
                        kaug_ref, vt_ref, kmean_ref, qaug_ref, m_ref, acc_ref, sa_ref, sb_ref,
                        *, n_blocks):
    qi = pl.program_id(1)
    blk_sz = MOBA_BLOCK
    hd = MOBA_HD
    rows = MOBA_GROUP * blk_sz

    @pl.when(qi == 0)
    def _prepare_kv_head():
        kmean_ref[...] = jnp.zeros(kmean_ref.shape, F32)
        vt_ref[hd:, :] = jnp.ones((vt_ref.shape[0] - hd, vt_ref.shape[1]), BF16)
        lane_blk = lax.broadcasted_iota(jnp.int32, (blk_sz, LANE), 1)

        def body(j, carry):
            r0 = pl.multiple_of(j * blk_sz, blk_sz)
            kb = k_ref[pl.ds(r0, blk_sz), :]
            kaug_ref[pl.ds(r0, blk_sz), 0:hd] = kb.astype(BF16)
            kaug_ref[pl.ds(r0, blk_sz), hd:hd + LANE] = jnp.where(lane_blk == j, 1.0, 0.0).astype(BF16)
            kmean_ref[pl.ds(j, 1), :] = jnp.sum(kb, axis=0, keepdims=True) * (1.0 / blk_sz)
            vt_ref[0:hd, pl.ds(r0, blk_sz)] = v_ref[pl.ds(r0, blk_sz), :].T.astype(BF16)
            return carry

        lax.fori_loop(0, n_blocks, body, 0)

    q = q_ref[...]
    qs = jnp.concatenate([q[:, hd * g:hd * (g + 1)] for g in range(MOBA_GROUP)], axis=0)
    gate = lax.dot_general(kmean_ref[...], qs, _NT, precision=lax.Precision.HIGHEST,
                           preferred_element_type=F32)
    nb_pad = kmean_ref.shape[0]
    bias = _topk_bias(gate, qi, nb_pad)
    bias = jnp.concatenate([bias, jnp.full((LANE - nb_pad, rows), MASK_BIAS, F32)], axis=0)
    qaug_ref[:, 0:hd] = (qs * (hd ** -0.5 * LOG2E)).astype(BF16)
    qaug_ref[:, hd:hd + LANE] = bias.T.astype(BF16)

    def scores(r0):
        return lax.dot_general(kaug_ref[pl.ds(r0, blk_sz), :], qaug_ref[...], _NT,
                               preferred_element_type=F32)

    def accumulate(s, r0):
        m_prev = m_ref[...]
        m_new = jnp.maximum(m_prev, jnp.max(s, axis=0, keepdims=True))
        p = jnp.exp2(s - m_new).astype(BF16)
        pv = jnp.dot(vt_ref[:, pl.ds(r0, blk_sz)], p, preferred_element_type=F32)
        acc_ref[...] = jnp.exp2(m_prev - m_new) * acc_ref[...] + pv
        m_ref[...] = m_new

    sa_ref[...] = scores(0)

    r_own = pl.multiple_of(qi * blk_sz, blk_sz)
    s = lax.dot_general(kaug_ref[pl.ds(r_own, blk_sz), 0:hd], qaug_ref[:, 0:hd], _NT,
                        preferred_element_type=F32)
    key_t = lax.broadcasted_iota(jnp.int32, s.shape, 0)
    row_t = lax.broadcasted_iota(jnp.int32, s.shape, 1) % blk_sz
    s = jnp.where(key_t <= row_t, s, NEG_INF)
    m0 = jnp.max(s, axis=0, keepdims=True)
    m_ref[...] = m0
    acc_ref[...] = jnp.dot(vt_ref[:, pl.ds(r_own, blk_sz)], jnp.exp2(s - m0).astype(BF16),
                           preferred_element_type=F32)

    n_pairs = (qi + 1) // 2

    def past_pair(jj, carry):
        r0 = pl.multiple_of(jj * (2 * blk_sz), 2 * blk_sz)
        r1 = pl.multiple_of(r0 + blk_sz, blk_sz)
        sb_ref[...] = scores(r1)
        accumulate(sa_ref[...], r0)
        r2 = pl.multiple_of(jnp.minimum(r0 + 2 * blk_sz, (n_blocks - 1) * blk_sz), blk_sz)
        sa_ref[...] = scores(r2)
        accumulate(sb_ref[...], r1)
        return carry

    lax.fori_loop(0, n_pairs, past_pair, 0)

    acc = acc_ref[...]
    o = (acc[0:hd, :] * (1.0 / acc[hd:hd + 1, :])).T
    o = jnp.concatenate([o[blk_sz * g:blk_sz * (g + 1), :] for g in range(MOBA_GROUP)], axis=1)
    o_ref[...] = (o * _silu(g_ref[...])).astype(o_ref.dtype)


def _moba_prompt(proj, t):
    assert t % (2 * MOBA_BLOCK) == 0
    nb = t // MOBA_BLOCK
    nb_pad = -(-nb // 8) * 8
    assert nb_pad <= LANE
    rows = MOBA_GROUP * MOBA_BLOCK
    gw = MOBA_GROUP * MOBA_HD
    return pl.pallas_call(
        functools.partial(_moba_prompt_kernel, n_blocks=nb),
        grid=(MOBA_KV_HEADS, nb),
        in_specs=[pl.BlockSpec((MOBA_BLOCK, gw), lambda h, i: (i, OFF_MQ // gw + h)),
                  pl.BlockSpec((t, MOBA_HD), lambda h, i: (0, OFF_MK // MOBA_HD + h)),
                  pl.BlockSpec((t, MOBA_HD), lambda h, i: (0, OFF_MV // MOBA_HD + h)),
                  pl.BlockSpec((MOBA_BLOCK, gw), lambda h, i: (i, OFF_MG // gw + h))],
        out_specs=pl.BlockSpec((MOBA_BLOCK, gw), lambda h, i: (i, h)),
        out_shape=jax.ShapeDtypeStruct((t, MOBA_QW), BF16),
        scratch_shapes=[pltpu.VMEM((t, MOBA_HD + LANE), BF16),
                        pltpu.VMEM((MOBA_HD + BF16_SUBLANES, t), BF16),
                        pltpu.VMEM((nb_pad, MOBA_HD), F32),
                        pltpu.VMEM((rows, MOBA_HD + LANE), BF16),
                        pltpu.VMEM((1, rows), F32),
                        pltpu.VMEM((MOBA_HD + BF16_SUBLANES, rows), F32),
                        pltpu.VMEM((MOBA_BLOCK, rows), F32),
                        pltpu.VMEM((MOBA_BLOCK, rows), F32)],
        compiler_params=_params("arbitrary", "arbitrary"),
        name="moba_prompt",
    )(proj, proj, proj, proj)


_GLA_GROUP = 4
_GLA_PAIRS_PER_STEP = 4


def _gla_prompt_kernel(q_ref, k_ref, b_ref, v_ref, gg_ref, gain_ref, o_ref, sout_ref,
                       s_ref, oacc_ref, *, n_tblocks, tb):
    ti = pl.program_id(1)
    sub = GLA_SUB
    kw = 2 * GLA_DK
    vw = 2 * GLA_DV
    npp = _GLA_PAIRS_PER_STEP

    @pl.when(ti == 0)
    def _zero_state():
        s_ref[...] = jnp.zeros(s_ref.shape, F32)

    row_h = lax.broadcasted_iota(jnp.int32, (kw, vw), 0) // GLA_DK
    col_h = lax.broadcasted_iota(jnp.int32, (kw, vw), 1) // GLA_DV
    same_head = row_h == col_h
    head_sum = jnp.where(same_head, 1.0, 0.0).astype(BF16)
    j_idx = lax.broadcasted_iota(jnp.int32, (sub, kw), 0)
    grp = _GLA_GROUP
    gt = grp * sub
    half = sub // 2
    sub_rows = half * half + half * sub
    rs_c = lax.broadcasted_iota(jnp.int32, (gt, grp * sub_rows), 1)
    rs_in = rs_c % sub_rows
    rs_tok = (rs_c // sub_rows) * sub + jnp.where(rs_in < half * half, rs_in // half,
                                                  half + (rs_in - half * half) // sub)
    row_sum = jnp.where(lax.broadcasted_iota(jnp.int32, (gt, grp * sub_rows), 0) == rs_tok,
                        1.0, 0.0).astype(BF16)

    def pair_group(q, k, b, v, state):
        w_rows, v_rows = [], []
        for s in range(grp):
            lo = s * sub
            b_s, k_s, v_s = b[lo:lo + sub, :], k[lo:lo + sub, :], v[lo:lo + sub, :]
            for i in range(sub):
                nj = half if i < half else sub
                d = b_s[i:i + 1, :] - b_s[0:nj, :]
                dec = jnp.exp(jnp.where(j_idx[0:nj, :] <= i, d, NEG_INF))
                w_rows.append(dec * k_s[0:nj, :] * q[lo + i:lo + i + 1, :])
                v_rows.append(v_s[0:nj, :])
        w = jnp.concatenate(w_rows, axis=0).astype(BF16)
        sc = jnp.dot(w, head_sum, preferred_element_type=F32)
        z = (sc * jnp.concatenate(v_rows, axis=0)).astype(BF16)
        o_diag = jnp.dot(row_sum, z, preferred_element_type=F32)
        q_dec = (q * jnp.exp(b)).astype(BF16)
        upds, decays = [], []
        for s in range(grp):
            lo = s * sub
            b_s, k_s, v_s = b[lo:lo + sub, :], k[lo:lo + sub, :], v[lo:lo + sub, :]
            b_last = b_s[sub - 1:sub, :]
            k_dec = (k_s * jnp.exp(b_last - b_s)).astype(BF16)
            upd = lax.dot_general(k_dec, v_s.astype(BF16), _TN, preferred_element_type=F32)
            upds.append(jnp.where(same_head, upd, 0.0))
            e_col = jnp.broadcast_to(jnp.exp(b_last), (kw, kw)).T
            decays.append(jnp.concatenate([e_col, e_col], axis=1))
        o_inter = []
        for s in range(grp):
            lo = s * sub
            o_inter.append(jnp.dot(q_dec[lo:lo + sub, :], state.astype(BF16),
                                   preferred_element_type=F32))
            state = state * decays[s] + upds[s]
        return jnp.concatenate(o_inter, axis=0) + o_diag, state

    def group(g, carry):
        r0 = pl.multiple_of(g * gt, gt)
        for pp in range(npp):
            ks, vs = slice(kw * pp, kw * (pp + 1)), slice(vw * pp, vw * (pp + 1))
            o, state = pair_group(q_ref[pl.ds(r0, gt), ks] * (GLA_DK ** -0.5), k_ref[pl.ds(r0, gt), ks],
                                  b_ref[pl.ds(r0, gt), ks], v_ref[pl.ds(r0, gt), vs], s_ref[pp])
            s_ref[pp] = state
            oacc_ref[pl.ds(r0, gt), vs] = o
        return carry

    lax.fori_loop(0, tb // gt, group, 0)

    o = oacc_ref[...]
    gain = gain_ref[...]
    heads = []
    for h in range(2 * npp):
        oh = o[:, GLA_DV * h:GLA_DV * (h + 1)]
        ms = jnp.mean(oh * oh, axis=-1, keepdims=True)
        heads.append(oh * lax.rsqrt(ms + NORM_EPS) * gain)
    o = jnp.concatenate(heads, axis=1)
    o_ref[...] = (o * _silu(gg_ref[...])).astype(o_ref.dtype)

    @pl.when(ti == n_tblocks - 1)
    def _emit_state():
        for pp in range(npp):
            st = s_ref[pp]
            sout_ref[2 * pp] = st[0:GLA_DK, 0:GLA_DV]
            sout_ref[2 * pp + 1] = st[GLA_DK:2 * GLA_DK, GLA_DV:2 * GLA_DV]


def _gla_prompt(proj, decay, gain, t):
    tb = 256 if t % 256 == 0 else t
    assert t % tb == 0 and tb % (GLA_SUB * _GLA_GROUP) == 0
    nt = t // tb
    npp = _GLA_PAIRS_PER_STEP
    kw, vw = 2 * GLA_DK * npp, 2 * GLA_DV * npp
    return pl.pallas_call(
        functools.partial(_gla_prompt_kernel, n_tblocks=nt, tb=tb),
        grid=(GLA_PAIRS // npp, nt),
        in_specs=[pl.BlockSpec((tb, kw), lambda p, i: (i, OFF_GQ // kw + p)),
                  pl.BlockSpec((tb, kw), lambda p, i: (i, OFF_GK // kw + p)),
                  pl.BlockSpec((tb, kw), lambda p, i: (i, p)),
                  pl.BlockSpec((tb, vw), lambda p, i: (i, OFF_GV // vw + p)),
                  pl.BlockSpec((tb, vw), lambda p, i: (i, OFF_GG // vw + p)),
                  pl.BlockSpec((1, GLA_DV), lambda p, i: (0, 0))],
        out_specs=[pl.BlockSpec((tb, vw), lambda p, i: (i, p)),
                   pl.BlockSpec((2 * npp, GLA_DK, GLA_DV), lambda p, i: (p, 0, 0))],
        out_shape=[jax.ShapeDtypeStruct((t, GLA_VW), BF16),
                   jax.ShapeDtypeStruct((GLA_HEADS, GLA_DK, GLA_DV), F32)],
        scratch_shapes=[pltpu.VMEM((npp, 2 * GLA_DK, 2 * GLA_DV), F32),
                        pltpu.VMEM((tb, vw), F32)],
        compiler_params=_params("arbitrary", "arbitrary"),
        name="gla_prompt",
    )(proj, proj, decay, proj, proj, gain)


def _out_proj_kernel(og_ref, om_ref, wg_ref, wm_ref, x_ref, lg_ref, lb_ref, o_ref,
                     *, n_ctiles, tn, alpha):
    j = pl.program_id(1)
    mixed = jnp.dot(og_ref[...], wg_ref[...], preferred_element_type=F32)
    mixed = mixed + jnp.dot(om_ref[...], wm_ref[...], preferred_element_type=F32)
    c0 = pl.multiple_of(j * tn, tn)
    o_ref[:, pl.ds(c0, tn)] = alpha * x_ref[...] + mixed

    @pl.when(j == n_ctiles - 1)
    def _layer_norm():
        h = o_ref[...]
        mu = jnp.mean(h, axis=-1, keepdims=True)
        hc = h - mu
        var = jnp.mean(hc * hc, axis=-1, keepdims=True)
        o_ref[...] = hc * lax.rsqrt(var + NORM_EPS) * lg_ref[...] + lb_ref[...]


def _out_proj(og, om, w_out_bf, x, ln_g, ln_b, alpha, tm, tn):
    m, d = x.shape
    kg, km = og.shape[1], om.shape[1]
    assert kg == km and w_out_bf.shape[0] == kg + km
    nct = d // tn
    return pl.pallas_call(
        functools.partial(_out_proj_kernel, n_ctiles=nct, tn=tn, alpha=alpha),
        grid=(m // tm, nct),
        in_specs=[pl.BlockSpec((tm, kg), lambda i, j: (i, 0)),
                  pl.BlockSpec((tm, km), lambda i, j: (i, 0)),
                  pl.BlockSpec((kg, tn), lambda i, j: (0, j)),
                  pl.BlockSpec((km, tn), lambda i, j: (1, j)),
                  pl.BlockSpec((tm, tn), lambda i, j: (i, j)),
                  pl.BlockSpec((1, d), lambda i, j: (0, 0)),
                  pl.BlockSpec((1, d), lambda i, j: (0, 0))],
        out_specs=pl.BlockSpec((tm, d), lambda i, j: (i, 0)),
        out_shape=jax.ShapeDtypeStruct((m, d), F32),
        compiler_params=_params("arbitrary", "arbitrary"),
        name="out_proj_ln",
    )(og, om, w_out_bf, w_out_bf, x, ln_g, ln_b)


_MOBA_DEC_SEQS = 2


def _moba_decode_kernel(pt_ref, q_ref, kn_ref, vn_ref, g_ref, *rest, n_pages, page):
    del pt_ref
    n_seq = _MOBA_DEC_SEQS
    o_ref = rest[2 * n_seq * n_pages]
    k_refs = [rest[sq * n_pages:(sq + 1) * n_pages] for sq in range(n_seq)]
    v_refs = [rest[(n_seq + sq) * n_pages:(n_seq + sq + 1) * n_pages] for sq in range(n_seq)]
    qs = [q_ref[sq] * (MOBA_HD ** -0.5) for sq in range(n_seq)]
    ss = [_moba_decode_scores(qs[sq], k_refs[sq], n_pages) for sq in range(n_seq)]
    ps = [_moba_decode_select(ss[sq], qs[sq], kn_ref[sq], n_pages, page) for sq in range(n_seq)]
    for sq in range(n_seq):
        pb, p_new, denom = ps[sq]
        o = _moba_decode_values(pb, p_new, denom, vn_ref[sq], v_refs[sq], n_pages, page)
        o_ref[sq] = (o * _silu(g_ref[sq])).astype(o_ref.dtype)


def _moba_decode_scores(q, k_refs, n_pages):
    qb = q.astype(BF16)
    return jnp.concatenate(
        [lax.dot_general(qb, k_refs[p][0, 0].astype(BF16), _NT, preferred_element_type=F32)
         for p in range(n_pages)], axis=1)


def _expand_kv_rows(x):
    return jnp.concatenate([jnp.broadcast_to(x[i:i + 1, :], (MOBA_GROUP, MOBA_HD))
                            for i in range(MOBA_KV_HEADS)], axis=0)


def _moba_decode_select(s, q, kn, n_pages, page):
    heads = MOBA_HEADS
    brow = MOBA_BLOCK * MOBA_KV_HEADS
    n_blk = (n_pages * page) // MOBA_BLOCK
    lane = lax.broadcasted_iota(jnp.int32, s.shape, 1)
    head_kv = lax.broadcasted_iota(jnp.int32, s.shape, 0) // MOBA_GROUP
    own_kv = (lane % MOBA_KV_HEADS) == head_kv
    s_own = jnp.where(own_kv, s, 0.0)

    bm = jnp.concatenate(
        [jnp.broadcast_to(jnp.sum(s_own[:, brow * j:brow * (j + 1)], axis=1, keepdims=True)
                          * (1.0 / MOBA_BLOCK), (heads, brow)) for j in range(n_blk)], axis=1)
    blk = lane // brow
    cur = bm
    sel = jnp.zeros(s.shape, jnp.bool_)
    for _ in range(min(MOBA_TOPK, n_blk)):
        mx = jnp.max(cur, axis=1, keepdims=True)
        first = jnp.min(jnp.where(cur == mx, blk, n_blk), axis=1, keepdims=True)
        pick = blk == first
        sel = jnp.logical_or(sel, pick)
        cur = jnp.where(pick, NEG_INF, cur)

    s_new = jnp.sum(q * _expand_kv_rows(kn), axis=1, keepdims=True)
    sm = jnp.where(jnp.logical_and(sel, own_kv), s, NEG_INF)
    m = jnp.maximum(jnp.max(sm, axis=1, keepdims=True), s_new)
    p = jnp.exp(sm - m)
    p_new = jnp.exp(s_new - m)
    denom = jnp.sum(p, axis=1, keepdims=True) + p_new
    return p.astype(BF16), p_new, denom


def _moba_decode_values(pb, p_new, denom, vn, v_refs, n_pages, page):
    prow = page * MOBA_KV_HEADS
    o = p_new * _expand_kv_rows(vn)
    for pg in range(n_pages):
        o = o + jnp.dot(pb[:, prow * pg:prow * (pg + 1)], v_refs[pg][0, 0].astype(BF16),
                        preferred_element_type=F32)
    return o * (1.0 / denom)


def _moba_decode(page_table, q3, kn3, vn3, g3, cache_k, cache_v, layer):
    db, n_pages = page_table.shape
    depth, n_pool, page = cache_k.shape[0], cache_k.shape[1], cache_k.shape[2]
    assert (n_pages * page) % MOBA_BLOCK == 0
    prow = page * MOBA_KV_HEADS
    ck = cache_k.reshape(depth, n_pool, prow, MOBA_HD)
    cv = cache_v.reshape(depth, n_pool, prow, MOBA_HD)

    n_seq = _MOBA_DEC_SEQS
    assert db % n_seq == 0

    def page_spec(sq, pg):
        return pl.BlockSpec((1, 1, prow, MOBA_HD), lambda b, pt: (layer, pt[b * n_seq + sq, pg], 0, 0))

    def row_spec(r):
        return pl.BlockSpec((n_seq, r, MOBA_HD), lambda b, pt: (b, 0, 0))

    pages = [page_spec(sq, pg) for sq in range(n_seq) for pg in range(n_pages)]
    grid_spec = pltpu.PrefetchScalarGridSpec(
        num_scalar_prefetch=1,
        grid=(db // n_seq,),
        in_specs=[row_spec(MOBA_HEADS), row_spec(MOBA_KV_HEADS), row_spec(MOBA_KV_HEADS),
                  row_spec(MOBA_HEADS)] + pages + pages,
        out_specs=pl.BlockSpec((n_seq, MOBA_HEADS, MOBA_HD), lambda b, pt: (b, 0, 0)),
    )
    n_ops = n_seq * n_pages
    return pl.pallas_call(
        functools.partial(_moba_decode_kernel, n_pages=n_pages, page=page),
        grid_spec=grid_spec,
        out_shape=jax.ShapeDtypeStruct((db, MOBA_HEADS, MOBA_HD), BF16),
        compiler_params=_params("arbitrary"),
        name="moba_decode",
    )(page_table, q3, kn3, vn3, g3, *([ck] * n_ops), *([cv] * n_ops))


_GLA_DEC_GROUP = 8


def _gla_decode_kernel(q_ref, k_ref, la_ref, v_ref, gg_ref, gain_ref, s_ref, o_ref, sn_ref):
    grp = _GLA_DEC_GROUP

    def columns(x):
        xp = jnp.concatenate([x, jnp.zeros((LANE - grp, x.shape[1]), F32)], axis=0)
        return xp.T

    a_t = columns(jnp.exp(la_ref[...]))
    k_t = columns(k_ref[...])
    q_t = columns(q_ref[...] * (GLA_DK ** -0.5))
    gain = gain_ref[...]
    for i in range(grp):
        rows = []
        for h in range(GLA_HEADS):
            r = slice(GLA_DK * h, GLA_DK * (h + 1))
            v_h = jnp.broadcast_to(v_ref[i, h:h + 1, :], (GLA_DK, GLA_DV))
            s_new = a_t[r, i:i + 1] * s_ref[i, h] + k_t[r, i:i + 1] * v_h
            sn_ref[i, h] = s_new
            rows.append(jnp.sum(q_t[r, i:i + 1] * s_new, axis=0, keepdims=True))
        o = jnp.concatenate(rows, axis=0)
        ms = jnp.mean(o * o, axis=-1, keepdims=True)
        o = o * lax.rsqrt(ms + NORM_EPS) * gain
        o_ref[i] = (o * _silu(gg_ref[i])).astype(o_ref.dtype)


def _gla_decode(q2, k2, la2, v3, gg3, gain, state):
    db = q2.shape[0]
    grp = _GLA_DEC_GROUP
    assert db % grp == 0
    vec = pl.BlockSpec((grp, GLA_KW), lambda g: (g, 0))
    hd3 = pl.BlockSpec((grp, GLA_HEADS, GLA_DV), lambda g: (g, 0, 0))
    st = pl.BlockSpec((grp, GLA_HEADS, GLA_DK, GLA_DV), lambda g: (g, 0, 0, 0))
    return pl.pallas_call(
        _gla_decode_kernel,
        grid=(db // grp,),
        in_specs=[vec, vec, vec, hd3, hd3, pl.BlockSpec((1, GLA_DV), lambda g: (0, 0)), st],
        out_specs=[hd3, st],
        out_shape=[jax.ShapeDtypeStruct((db, GLA_HEADS, GLA_DV), BF16),
                   jax.ShapeDtypeStruct(state.shape, F32)],
        compiler_params=_params("arbitrary"),
        name="gla_decode",
    )(q2, k2, la2, v3, gg3, gain, state)


def _tile(n, prefs):
    for p in prefs:
        if n % p == 0:
            return p
    return n


def kernel(x_prompt, x_sample, cache_k, cache_v, state_gla, page_table,
           w_in, w_a2, b_a, gla_gain, w_out, ln_g, ln_b):
    bsz, t, d = x_prompt.shape
    db, dt, _ = x_sample.shape
    depth = w_in.shape[0]
    assert bsz == 1 and dt == 1
    alpha = (2.0 * depth) ** 0.25
    y_p = x_prompt.reshape(t, d)
    y_s = x_sample.reshape(db, d)
    kp_l, vp_l, sp_l, ks_l, vs_l, ss_l = [], [], [], [], [], []
    for l in range(depth):
        w_in_t = jnp.swapaxes(w_in[l], 0, 1)
        w_lr_t = jnp.zeros((LANE, d), BF16).at[:GLA_LOWRANK, :].set(w_in_t[OFF_GA:].astype(BF16))
        w_a2p = jnp.zeros((LANE, GLA_KW), F32).at[:GLA_LOWRANK, :].set(w_a2[l])
        w_out_bf = w_out[l].astype(BF16)
        gain = gla_gain[l].reshape(1, GLA_DV)
        lg, lb = ln_g[l].reshape(1, d), ln_b[l].reshape(1, d)
        x_all = _stack_cast(y_p, y_s, _tile(math.gcd(t, db), (128,)))
        m_all = t + db
        proj = _in_proj(x_all, w_in_t, MAIN_W, _tile(m_all, (520, 512, 256, 128)), 1024)
        ba2 = b_a[l].reshape(1, GLA_KW)
        decay = _log_decay(x_all, w_lr_t, w_a2p, ba2, 0, t, _tile(math.gcd(t, 512), (512,)), True)
        log_a_s = _log_decay(x_all, w_lr_t, w_a2p, ba2, t, db, db, False)
        o_m = _moba_prompt(proj, t)
        o_g, s_p = _gla_prompt(proj, decay, gain, t)
        k_rows, v_rows = _kv_rows(proj, t, _tile(t, (512, 256, 128)))
        kp_l.append(k_rows.reshape(bsz, t, MOBA_KV_HEADS, MOBA_HD))
        vp_l.append(v_rows.reshape(bsz, t, MOBA_KV_HEADS, MOBA_HD))
        sp_l.append(s_p.reshape(bsz, GLA_HEADS, GLA_DK, GLA_DV))
        y_p_new = _out_proj(o_g, o_m, w_out_bf, y_p, lg, lb, alpha,
                            _tile(t, (512, 256, 128)), _tile(d, (1024, 512, 256, 128)))
        proj_s = proj[t:]
        k_new = proj_s[:, OFF_MK:OFF_MV].reshape(db, MOBA_KV_HEADS, MOBA_HD)
        v_new = proj_s[:, OFF_MV:OFF_MG].reshape(db, MOBA_KV_HEADS, MOBA_HD)
        o_m_s = _moba_decode(page_table,
                             proj_s[:, OFF_MQ:OFF_MK].reshape(db, MOBA_HEADS, MOBA_HD),
                             k_new, v_new,
                             proj_s[:, OFF_MG:OFF_GQ].reshape(db, MOBA_HEADS, MOBA_HD),
                             cache_k, cache_v, l)
        o_g_s, s_s = _gla_decode(proj_s[:, OFF_GQ:OFF_GK], proj_s[:, OFF_GK:OFF_GV], log_a_s,
                                 proj_s[:, OFF_GV:OFF_GG].reshape(db, GLA_HEADS, GLA_DV),
                                 proj_s[:, OFF_GG:OFF_GA].reshape(db, GLA_HEADS, GLA_DV),
                                 gain, state_gla[l])
        ks_l.append(k_new.reshape(db, dt, MOBA_KV_HEADS, MOBA_HD))
        vs_l.append(v_new.reshape(db, dt, MOBA_KV_HEADS, MOBA_HD))
        ss_l.append(s_s)
        y_s = _out_proj(o_g_s.reshape(db, GLA_VW), o_m_s.reshape(db, MOBA_QW), w_out_bf, y_s, lg, lb,
                        alpha, _tile(db, (128,)), _tile(d, (1024, 512, 256, 128)))
        y_p = y_p_new
    return (y_p.reshape(bsz, t, d), y_s.reshape(db, dt, d),
            jnp.stack(kp_l), jnp.stack(vp_l), jnp.stack(sp_l),
            jnp.stack(ks_l), jnp.stack(vs_l), jnp.stack(ss_l))
```

```python
import functools
import math

import jax
import jax.numpy as jnp
from jax import lax
from jax.experimental import pallas as pl
from jax.experimental.pallas import tpu as pltpu

F32 = jnp.float32
BF16 = jnp.bfloat16
NEG_INF = float("-inf")
MASK_BIAS = -1e30
LOG2E = 1.4426950408889634

MOBA_HEADS = 16
MOBA_KV_HEADS = 4
MOBA_GROUP = MOBA_HEADS // MOBA_KV_HEADS
MOBA_HD = 128
MOBA_BLOCK = 256
MOBA_TOPK = 3
GLA_HEADS = 16
GLA_DK = 64
GLA_DV = 128
GLA_LOWRANK = 16
GLA_GATE_TAU = 16.0
GLA_SUB = 16
GLA_PAIRS = GLA_HEADS // 2
NORM_EPS = 1e-5

MOBA_QW = MOBA_HEADS * MOBA_HD
MOBA_KVW = MOBA_KV_HEADS * MOBA_HD
GLA_KW = GLA_HEADS * GLA_DK
GLA_VW = GLA_HEADS * GLA_DV
OFF_MQ = 0
OFF_MK = OFF_MQ + MOBA_QW
OFF_MV = OFF_MK + MOBA_KVW
OFF_MG = OFF_MV + MOBA_KVW
OFF_GQ = OFF_MG + MOBA_QW
OFF_GK = OFF_GQ + GLA_KW
OFF_GV = OFF_GK + GLA_KW
OFF_GG = OFF_GV + GLA_VW
OFF_GA = OFF_GG + GLA_VW
MAIN_W = OFF_GA

LANE = 128
BF16_SUBLANES = 16
VMEM_LIMIT_BYTES = 60 * 1024 * 1024

_NT = (((1,), (1,)), ((), ()))
_TN = (((0,), (0,)), ((), ()))


def _params(*sem):
    return pltpu.CompilerParams(dimension_semantics=sem, vmem_limit_bytes=VMEM_LIMIT_BYTES)


def _silu(x):
    return x * (1.0 / (1.0 + jnp.exp(-x)))


def _split_bf16(x):
    hi = x.astype(BF16)
    return hi, (x - hi.astype(F32)).astype(BF16)


def _stack_cast_kernel(xp_ref, xs_ref, o_ref, *, n_prompt_tiles):
    i = pl.program_id(0)

    @pl.when(i < n_prompt_tiles)
    def _prompt_rows():
        o_ref[...] = xp_ref[...].astype(o_ref.dtype)

    @pl.when(i >= n_prompt_tiles)
    def _decode_rows():
        o_ref[...] = xs_ref[...].astype(o_ref.dtype)


def _stack_cast(xp, xs, tr):
    (t, d), db = xp.shape, xs.shape[0]
    assert t % tr == 0 and db % tr == 0
    npt, nst = t // tr, db // tr
    return pl.pallas_call(
        functools.partial(_stack_cast_kernel, n_prompt_tiles=npt),
        grid=(npt + nst,),
        in_specs=[pl.BlockSpec((tr, d), lambda i: (jnp.minimum(i, npt - 1), 0)),
                  pl.BlockSpec((tr, d), lambda i: (jnp.maximum(i - npt, 0), 0))],
        out_specs=pl.BlockSpec((tr, d), lambda i: (i, 0)),
        out_shape=jax.ShapeDtypeStruct((t + db, d), BF16),
        compiler_params=_params("arbitrary"),
        name="stack_cast",
    )(xp, xs)


def _in_proj_kernel(x_ref, wt_ref, o_ref, wbf_ref):
    @pl.when(pl.program_id(1) == 0)
    def _cast_weight_tile():
        wbf_ref[...] = wt_ref[...].astype(BF16)

    o_ref[...] = lax.dot_general(x_ref[...], wbf_ref[...], _NT, preferred_element_type=F32)


def _in_proj(x, wt, n_out, tm, tn):
    m, k = x.shape
    assert n_out % tn == 0 and m % tm == 0 and wt.shape[0] >= n_out
    return pl.pallas_call(
        _in_proj_kernel,
        grid=(n_out // tn, m // tm),
        in_specs=[pl.BlockSpec((tm, k), lambda j, i: (i, 0)),
                  pl.BlockSpec((tn, k), lambda j, i: (j, 0))],
        out_specs=pl.BlockSpec((tm, tn), lambda j, i: (i, j)),
        out_shape=jax.ShapeDtypeStruct((m, n_out), F32),
        scratch_shapes=[pltpu.VMEM((tn, k), BF16)],
        compiler_params=_params("arbitrary", "arbitrary"),
        name="in_proj",
    )(x, wt)


def _kv_rows_kernel(k_ref, v_ref, ko_ref, vo_ref):
    tok = k_ref.shape[0]
    for h in range(MOBA_KV_HEADS):
        cols = slice(MOBA_HD * h, MOBA_HD * (h + 1))
        ko_ref[pl.ds(h, tok, stride=MOBA_KV_HEADS), :] = k_ref[:, cols]
        vo_ref[pl.ds(h, tok, stride=MOBA_KV_HEADS), :] = v_ref[:, cols]


def _kv_rows(proj, t, tr):
    assert t % tr == 0
    out = jax.ShapeDtypeStruct((t * MOBA_KV_HEADS, MOBA_HD), F32)
    return pl.pallas_call(
        _kv_rows_kernel,
        grid=(t // tr,),
        in_specs=[pl.BlockSpec((tr, MOBA_KVW), lambda i: (i, OFF_MK // MOBA_KVW)),
                  pl.BlockSpec((tr, MOBA_KVW), lambda i: (i, OFF_MV // MOBA_KVW))],
        out_specs=[pl.BlockSpec((tr * MOBA_KV_HEADS, MOBA_HD), lambda i: (i, 0)),
                   pl.BlockSpec((tr * MOBA_KV_HEADS, MOBA_HD), lambda i: (i, 0))],
        out_shape=[out, out],
        compiler_params=_params("arbitrary"),
        name="kv_rows",
    )(proj, proj)


def _log_decay_kernel(x_ref, wlr_ref, wa2_ref, ba_ref, o_ref, *, cumulative):
    ga = lax.dot_general(x_ref[...], wlr_ref[...], _NT, preferred_element_type=F32)
    ga_hi, ga_lo = _split_bf16(ga)
    wa_hi, wa_lo = _split_bf16(wa2_ref[...])
    z = (jnp.dot(ga_hi, wa_hi, preferred_element_type=F32)
         + (jnp.dot(ga_lo, wa_hi, preferred_element_type=F32)
            + jnp.dot(ga_hi, wa_lo, preferred_element_type=F32))) + ba_ref[...]
    log_a = (jnp.minimum(z, 0.0) - jnp.log(1.0 + jnp.exp(-jnp.abs(z)))) * (1.0 / GLA_GATE_TAU)
    if cumulative:
        pos = lax.broadcasted_iota(jnp.int32, log_a.shape, 0) % GLA_SUB
        step = 1
        while step < GLA_SUB:
            log_a = log_a + jnp.where(pos >= step, pltpu.roll(log_a, step, axis=0), 0.0)
            step *= 2
    o_ref[...] = log_a


def _log_decay(x, w_lr, w_a2p, b_a, row0, n_rows, tm, cumulative):
    k = x.shape[1]
    assert row0 % tm == 0 and n_rows % tm == 0 and tm % GLA_SUB == 0
    t0 = row0 // tm
    return pl.pallas_call(
        functools.partial(_log_decay_kernel, cumulative=cumulative),
        grid=(n_rows // tm,),
        in_specs=[pl.BlockSpec((tm, k), lambda i: (t0 + i, 0)),
                  pl.BlockSpec((LANE, k), lambda i: (0, 0)),
                  pl.BlockSpec((LANE, GLA_KW), lambda i: (0, 0)),
                  pl.BlockSpec((1, GLA_KW), lambda i: (0, 0))],
        out_specs=pl.BlockSpec((tm, GLA_KW), lambda i: (i, 0)),
        out_shape=jax.ShapeDtypeStruct((n_rows, GLA_KW), F32),
        compiler_params=_params("arbitrary"),
        name="gla_log_decay",
    )(x, w_lr, w_a2p, b_a)


def _topk_bias(gate, n_valid, n_rows):
    blk = lax.broadcasted_iota(jnp.int32, gate.shape, 0)
    cur = jnp.where(blk < n_valid, gate, NEG_INF)
    sel = jnp.zeros(gate.shape, jnp.bool_)
    for _ in range(MOBA_TOPK):
        mx = jnp.max(cur, axis=0, keepdims=True)
        hit = jnp.logical_and(cur == mx, mx > NEG_INF)
        first = jnp.min(jnp.where(hit, blk, n_rows), axis=0, keepdims=True)
        pick = blk == first
        sel = jnp.logical_or(sel, pick)
        cur = jnp.where(pick, NEG_INF, cur)
    return jnp.where(sel, 0.0, MASK_BIAS)


def _moba_prompt_kernel(q_ref, k_ref, v_ref, g_ref, o_ref,
                        kaug_ref, vt_ref, kmean_ref, qaug_ref, m_ref, acc_ref, sa_ref, sb_ref,
                        *, n_blocks):
    qi = pl.program_id(1)
    blk_sz = MOBA_BLOCK
    hd = MOBA_HD
    rows = MOBA_GROUP * blk_sz

    @pl.when(qi == 0)
    def _prepare_kv_head():
        kmean_ref[...] = jnp.zeros(kmean_ref.shape, F32)
        vt_ref[hd:, :] = jnp.ones((vt_ref.shape[0] - hd, vt_ref.shape[1]), BF16)
        lane_blk = lax.broadcasted_iota(jnp.int32, (blk_sz, LANE), 1)

        def body(j, carry):
            r0 = pl.multiple_of(j * blk_sz, blk_sz)
            kb = k_ref[pl.ds(r0, blk_sz), :]
            kaug_ref[pl.ds(r0, blk_sz), 0:hd] = kb.astype(BF16)
            kaug_ref[pl.ds(r0, blk_sz), hd:hd + LANE] = jnp.where(lane_blk == j, 1.0, 0.0).astype(BF16)
            kmean_ref[pl.ds(j, 1), :] = jnp.sum(kb, axis=0, keepdims=True) * (1.0 / blk_sz)
            vt_ref[0:hd, pl.ds(r0, blk_sz)] = v_ref[pl.ds(r0, blk_sz), :].T.astype(BF16)
            return carry

        lax.fori_loop(0, n_blocks, body, 0)

    q = q_ref[...]
    qs = jnp.concatenate([q[:, hd * g:hd * (g + 1)] for g in range(MOBA_GROUP)], axis=0)
    gate = lax.dot_general(kmean_ref[...], qs, _NT, precision=lax.Precision.HIGHEST,
                           preferred_element_type=F32)
    nb_pad = kmean_ref.shape[0]
    bias = _topk_bias(gate, qi, nb_pad)
    bias = jnp.concatenate([bias, jnp.full((LANE - nb_pad, rows), MASK_BIAS, F32)], axis=0)
    qaug_ref[:, 0:hd] = (qs * (hd ** -0.5 * LOG2E)).astype(BF16)
    qaug_ref[:, hd:hd + LANE] = bias.T.astype(BF16)

    def scores(r0):
        return lax.dot_general(kaug_ref[pl.ds(r0, blk_sz), :], qaug_ref[...], _NT,
                               preferred_element_type=F32)

    def accumulate(s, r0):
        m_prev = m_ref[...]
        m_new = jnp.maximum(m_prev, jnp.max(s, axis=0, keepdims=True))
        p = jnp.exp2(s - m_new).astype(BF16)
        pv = jnp.dot(vt_ref[:, pl.ds(r0, blk_sz)], p, preferred_element_type=F32)
        acc_ref[...] = jnp.exp2(m_prev - m_new) * acc_ref[...] + pv
        m_ref[...] = m_new

    sa_ref[...] = scores(0)

    r_own = pl.multiple_of(qi * blk_sz, blk_sz)
    s = lax.dot_general(kaug_ref[pl.ds(r_own, blk_sz), 0:hd], qaug_ref[:, 0:hd], _NT,
                        preferred_element_type=F32)
    key_t = lax.broadcasted_iota(jnp.int32, s.shape, 0)
    row_t = lax.broadcasted_iota(jnp.int32, s.shape, 1) % blk_sz
    s = jnp.where(key_t <= row_t, s, NEG_INF)
    m0 = jnp.max(s, axis=0, keepdims=True)
    m_ref[...] = m0
    acc_ref[...] = jnp.dot(vt_ref[:, pl.ds(r_own, blk_sz)], jnp.exp2(s - m0).astype(BF16),
                           preferred_element_type=F32)

    n_pairs = (qi + 1) // 2

    def past_pair(jj, carry):
        r0 = pl.multiple_of(jj * (2 * blk_sz), 2 * blk_sz)
        r1 = pl.multiple_of(r0 + blk_sz, blk_sz)
        sb_ref[...] = scores(r1)
        accumulate(sa_ref[...], r0)
        r2 = pl.multiple_of(jnp.minimum(r0 + 2 * blk_sz, (n_blocks - 1) * blk_sz), blk_sz)
        sa_ref[...] = scores(r2)
        accumulate(sb_ref[...], r1)
        return carry

    lax.fori_loop(0, n_pairs, past_pair, 0)

    acc = acc_ref[...]
    o = (acc[0:hd, :] * (1.0 / acc[hd:hd + 1, :])).T
    o = jnp.concatenate([o[blk_sz * g:blk_sz * (g + 1), :] for g in range(MOBA_GROUP)], axis=1)
    o_ref[...] = (o * _silu(g_ref[...])).astype(o_ref.dtype)


def _moba_prompt(proj, t):
    assert t % (2 * MOBA_BLOCK) == 0
    nb = t // MOBA_BLOCK
    nb_pad = -(-nb // 8) * 8
    assert nb_pad <= LANE
    rows = MOBA_GROUP * MOBA_BLOCK
    gw = MOBA_GROUP * MOBA_HD
    return pl.pallas_call(
        functools.partial(_moba_prompt_kernel, n_blocks=nb),
        grid=(MOBA_KV_HEADS, nb),
        in_specs=[pl.BlockSpec((MOBA_BLOCK, gw), lambda h, i: (i, OFF_MQ // gw + h)),
                  pl.BlockSpec((t, MOBA_HD), lambda h, i: (0, OFF_MK // MOBA_HD + h)),
                  pl.BlockSpec((t, MOBA_HD), lambda h, i: (0, OFF_MV // MOBA_HD + h)),
                  pl.BlockSpec((MOBA_BLOCK, gw), lambda h, i: (i, OFF_MG // gw + h))],
        out_specs=pl.BlockSpec((MOBA_BLOCK, gw), lambda h, i: (i, h)),
        out_shape=jax.ShapeDtypeStruct((t, MOBA_QW), BF16),
        scratch_shapes=[pltpu.VMEM((t, MOBA_HD + LANE), BF16),
                        pltpu.VMEM((MOBA_HD + BF16_SUBLANES, t), BF16),
                        pltpu.VMEM((nb_pad, MOBA_HD), F32),
                        pltpu.VMEM((rows, MOBA_HD + LANE), BF16),
                        pltpu.VMEM((1, rows), F32),
                        pltpu.VMEM((MOBA_HD + BF16_SUBLANES, rows), F32),
                        pltpu.VMEM((MOBA_BLOCK, rows), F32),
                        pltpu.VMEM((MOBA_BLOCK, rows), F32)],
        compiler_params=_params("arbitrary", "arbitrary"),
        name="moba_prompt",
    )(proj, proj, proj, proj)


_GLA_GROUP = 4
_GLA_PAIRS_PER_STEP = 8
_GLA_PAIRS_PER_VCHUNK = 4


def _gla_prompt_kernel(q_ref, k_ref, b_ref, *rest, n_tblocks, tb):
    ti = pl.program_id(1)
    sub = GLA_SUB
    kw = 2 * GLA_DK
    vw = 2 * GLA_DV
    npp = _GLA_PAIRS_PER_STEP
    nvc = npp // _GLA_PAIRS_PER_VCHUNK
    v_refs, gg_refs = rest[:nvc], rest[nvc:2 * nvc]
    gain_ref, o_ref, sout_ref, s_ref, oacc_ref = rest[2 * nvc:]

    @pl.when(ti == 0)
    def _zero_state():
        s_ref[...] = jnp.zeros(s_ref.shape, F32)

    row_h = lax.broadcasted_iota(jnp.int32, (kw, vw), 0) // GLA_DK
    col_h = lax.broadcasted_iota(jnp.int32, (kw, vw), 1) // GLA_DV
    same_head = row_h == col_h
    head_sum = jnp.where(same_head, 1.0, 0.0).astype(BF16)
    j_idx = lax.broadcasted_iota(jnp.int32, (sub, kw), 0)
    grp = _GLA_GROUP
    lane_head = lax.broadcasted_iota(jnp.int32, (grp * sub, kw), 1) // GLA_DK
    gt = grp * sub
    half = sub // 2
    sub_rows = half * half + half * sub
    rs_c = lax.broadcasted_iota(jnp.int32, (gt, grp * sub_rows), 1)
    rs_in = rs_c % sub_rows
    rs_tok = (rs_c // sub_rows) * sub + jnp.where(rs_in < half * half, rs_in // half,
                                                  half + (rs_in - half * half) // sub)
    row_sum = jnp.where(lax.broadcasted_iota(jnp.int32, (gt, grp * sub_rows), 0) == rs_tok,
                        1.0, 0.0).astype(BF16)

    def pair_group(q, k, b, v, state):
        w_rows, v_rows = [], []
        for s in range(grp):
            lo = s * sub
            b_s, k_s, v_s = b[lo:lo + sub, :], k[lo:lo + sub, :], v[lo:lo + sub, :]
            for i in range(sub):
                nj = half if i < half else sub
                d = b_s[i:i + 1, :] - b_s[0:nj, :]
                dec = jnp.exp(jnp.where(j_idx[0:nj, :] < i, d, NEG_INF))
                w_rows.append(dec * k_s[0:nj, :] * q[lo + i:lo + i + 1, :])
                v_rows.append(v_s[0:nj, :])
        w = jnp.concatenate(w_rows, axis=0).astype(BF16)
        sc = jnp.dot(w, head_sum, preferred_element_type=F32)
        z = (sc * jnp.concatenate(v_rows, axis=0)).astype(BF16)
        o_diag = jnp.dot(row_sum, z, preferred_element_type=F32)
        qk = q * k
        o_self = jnp.concatenate(
            [jnp.sum(jnp.where(lane_head == h, qk, 0.0), axis=1, keepdims=True)
             * v[:, GLA_DV * h:GLA_DV * (h + 1)] for h in range(2)], axis=1)
        o_diag = o_diag + o_self
        q_dec = (q * jnp.exp(b)).astype(BF16)
        upds, decays = [], []
        for s in range(grp):
            lo = s * sub
            b_s, k_s, v_s = b[lo:lo + sub, :], k[lo:lo + sub, :], v[lo:lo + sub, :]
            b_last = b_s[sub - 1:sub, :]
            k_dec = (k_s * jnp.exp(b_last - b_s)).astype(BF16)
            upd = lax.dot_general(k_dec, v_s.astype(BF16), _TN, preferred_element_type=F32)
            upds.append(jnp.where(same_head, upd, 0.0))
            e_col = jnp.broadcast_to(jnp.exp(b_last), (kw, kw)).T
            decays.append(jnp.concatenate([e_col, e_col], axis=1))
        o_inter = []
        for s in range(grp):
            lo = s * sub
            o_inter.append(jnp.dot(q_dec[lo:lo + sub, :], state.astype(BF16),
                                   preferred_element_type=F32))
            state = state * decays[s] + upds[s]
        return jnp.concatenate(o_inter, axis=0) + o_diag, state

    def group(g, carry):
        r0 = pl.multiple_of(g * gt, gt)
        for pp in range(npp):
            ks, vs = slice(kw * pp, kw * (pp + 1)), slice(vw * pp, vw * (pp + 1))
            pc = pp % _GLA_PAIRS_PER_VCHUNK
            v_pair = v_refs[pp // _GLA_PAIRS_PER_VCHUNK][pl.ds(r0, gt), vw * pc:vw * (pc + 1)]
            o, state = pair_group(q_ref[pl.ds(r0, gt), ks] * (GLA_DK ** -0.5), k_ref[pl.ds(r0, gt), ks],
                                  b_ref[pl.ds(r0, gt), ks], v_pair, s_ref[pp])
            s_ref[pp] = state
            oacc_ref[pl.ds(r0, gt), vs] = o
        return carry

    lax.fori_loop(0, tb // gt, group, 0)

    o = oacc_ref[...]
    gain = gain_ref[...]
    heads = []
    for h in range(2 * npp):
        oh = o[:, GLA_DV * h:GLA_DV * (h + 1)]
        ms = jnp.mean(oh * oh, axis=-1, keepdims=True)
        heads.append(oh * lax.rsqrt(ms + NORM_EPS) * gain)
    o = jnp.concatenate(heads, axis=1)
    gate = jnp.concatenate([r[...] for r in gg_refs], axis=1)
    o_ref[...] = (o * _silu(gate)).astype(o_ref.dtype)

    @pl.when(ti == n_tblocks - 1)
    def _emit_state():
        for pp in range(npp):
            st = s_ref[pp]
            sout_ref[2 * pp] = st[0:GLA_DK, 0:GLA_DV]
            sout_ref[2 * pp + 1] = st[GLA_DK:2 * GLA_DK, GLA_DV:2 * GLA_DV]


def _gla_prompt(proj, decay, gain, t):
    tb = 256 if t % 256 == 0 else t
    assert t % tb == 0 and tb % (GLA_SUB * _GLA_GROUP) == 0
    nt = t // tb
    npp = _GLA_PAIRS_PER_STEP
    kw, vw = 2 * GLA_DK * npp, 2 * GLA_DV * npp
    vcw = 2 * GLA_DV * _GLA_PAIRS_PER_VCHUNK
    nvc = npp // _GLA_PAIRS_PER_VCHUNK
    assert GLA_PAIRS % npp == 0 and npp % _GLA_PAIRS_PER_VCHUNK == 0
    assert OFF_GQ % kw == 0 and OFF_GK % kw == 0 and OFF_GV % vcw == 0 and OFF_GG % vcw == 0

    def vchunk(off, c):
        return pl.BlockSpec((tb, vcw), lambda p, i: (i, off // vcw + p * nvc + c))

    return pl.pallas_call(
        functools.partial(_gla_prompt_kernel, n_tblocks=nt, tb=tb),
        grid=(GLA_PAIRS // npp, nt),
        in_specs=[pl.BlockSpec((tb, kw), lambda p, i: (i, OFF_GQ // kw + p)),
                  pl.BlockSpec((tb, kw), lambda p, i: (i, OFF_GK // kw + p)),
                  pl.BlockSpec((tb, kw), lambda p, i: (i, p))]
                 + [vchunk(OFF_GV, c) for c in range(nvc)]
                 + [vchunk(OFF_GG, c) for c in range(nvc)]
                 + [pl.BlockSpec((1, GLA_DV), lambda p, i: (0, 0))],
        out_specs=[pl.BlockSpec((tb, vw), lambda p, i: (i, p)),
                   pl.BlockSpec((2 * npp, GLA_DK, GLA_DV), lambda p, i: (p, 0, 0))],
        out_shape=[jax.ShapeDtypeStruct((t, GLA_VW), BF16),
                   jax.ShapeDtypeStruct((GLA_HEADS, GLA_DK, GLA_DV), F32)],
        scratch_shapes=[pltpu.VMEM((npp, 2 * GLA_DK, 2 * GLA_DV), F32),
                        pltpu.VMEM((tb, vw), F32)],
        compiler_params=_params("arbitrary", "arbitrary"),
        name="gla_prompt",
    )(proj, proj, decay, *([proj] * (2 * nvc)), gain)


def _out_proj_kernel(og_ref, om_ref, wg_ref, wm_ref, x_ref, lg_ref, lb_ref, o_ref,
                     *, n_ctiles, tn, alpha):
    j = pl.program_id(1)
    mixed = jnp.dot(og_ref[...], wg_ref[...], preferred_element_type=F32)
    mixed = mixed + jnp.dot(om_ref[...], wm_ref[...], preferred_element_type=F32)
    c0 = pl.multiple_of(j * tn, tn)
    o_ref[:, pl.ds(c0, tn)] = alpha * x_ref[...] + mixed

    @pl.when(j == n_ctiles - 1)
    def _layer_norm():
        h = o_ref[...]
        mu = jnp.mean(h, axis=-1, keepdims=True)
        hc = h - mu
        var = jnp.mean(hc * hc, axis=-1, keepdims=True)
        o_ref[...] = hc * lax.rsqrt(var + NORM_EPS) * lg_ref[...] + lb_ref[...]


def _out_proj(og, om, w_out_bf, x, ln_g, ln_b, alpha, tm, tn):
    m, d = x.shape
    kg, km = og.shape[1], om.shape[1]
    assert kg == km and w_out_bf.shape[0] == kg + km
    nct = d // tn
    return pl.pallas_call(
        functools.partial(_out_proj_kernel, n_ctiles=nct, tn=tn, alpha=alpha),
        grid=(m // tm, nct),
        in_specs=[pl.BlockSpec((tm, kg), lambda i, j: (i, 0)),
                  pl.BlockSpec((tm, km), lambda i, j: (i, 0)),
                  pl.BlockSpec((kg, tn), lambda i, j: (0, j)),
                  pl.BlockSpec((km, tn), lambda i, j: (1, j)),
                  pl.BlockSpec((tm, tn), lambda i, j: (i, j)),
                  pl.BlockSpec((1, d), lambda i, j: (0, 0)),
                  pl.BlockSpec((1, d), lambda i, j: (0, 0))],
        out_specs=pl.BlockSpec((tm, d), lambda i, j: (i, 0)),
        out_shape=jax.ShapeDtypeStruct((m, d), F32),
        compiler_params=_params("arbitrary", "arbitrary"),
        name="out_proj_ln",
    )(og, om, w_out_bf, w_out_bf, x, ln_g, ln_b)


_MOBA_DEC_SEQS = 2


def _moba_decode_kernel(pt_ref, q_ref, kn_ref, vn_ref, g_ref, *rest, n_pages, page):
    del pt_ref
    n_seq = _MOBA_DEC_SEQS
    o_ref = rest[2 * n_seq * n_pages]
    k_refs = [rest[sq * n_pages:(sq + 1) * n_pages] for sq in range(n_seq)]
    v_refs = [rest[(n_seq + sq) * n_pages:(n_seq + sq + 1) * n_pages] for sq in range(n_seq)]
    qs = [q_ref[sq] * (MOBA_HD ** -0.5) for sq in range(n_seq)]
    ss = [_moba_decode_scores(qs[sq], k_refs[sq], n_pages) for sq in range(n_seq)]
    ps = [_moba_decode_select(ss[sq], qs[sq], kn_ref[sq], n_pages, page) for sq in range(n_seq)]
    for sq in range(n_seq):
        pb, p_new, denom = ps[sq]
        o = _moba_decode_values(pb, p_new, denom, vn_ref[sq], v_refs[sq], n_pages, page)
        o_ref[sq] = (o * _silu(g_ref[sq])).astype(o_ref.dtype)


def _moba_decode_scores(q, k_refs, n_pages):
    qb = q.astype(BF16)
    return jnp.concatenate(
        [lax.dot_general(qb, k_refs[p][0, 0].astype(BF16), _NT, preferred_element_type=F32)
         for p in range(n_pages)], axis=1)


def _expand_kv_rows(x):
    return jnp.concatenate([jnp.broadcast_to(x[i:i + 1, :], (MOBA_GROUP, MOBA_HD))
                            for i in range(MOBA_KV_HEADS)], axis=0)


def _moba_decode_select(s, q, kn, n_pages, page):
    heads = MOBA_HEADS
    brow = MOBA_BLOCK * MOBA_KV_HEADS
    n_blk = (n_pages * page) // MOBA_BLOCK
    lane = lax.broadcasted_iota(jnp.int32, s.shape, 1)
    head_kv = lax.broadcasted_iota(jnp.int32, s.shape, 0) // MOBA_GROUP
    own_kv = (lane % MOBA_KV_HEADS) == head_kv
    s_own = jnp.where(own_kv, s, 0.0)

    assert n_blk <= LANE
    blk_lane = lax.broadcasted_iota(jnp.int32, (heads, LANE), 1)
    cur = jnp.full((heads, LANE), NEG_INF, F32)
    for j in range(n_blk):
        mean_j = jnp.sum(s_own[:, brow * j:brow * (j + 1)], axis=1, keepdims=True) * (1.0 / MOBA_BLOCK)
        cur = jnp.where(blk_lane == j, mean_j, cur)
    sel = jnp.zeros((heads, LANE), F32)
    for _ in range(min(MOBA_TOPK, n_blk)):
        mx = jnp.max(cur, axis=1, keepdims=True)
        first = jnp.min(jnp.where(cur == mx, blk_lane, n_blk), axis=1, keepdims=True)
        pick = blk_lane == first
        sel = jnp.where(pick, 1.0, sel)
        cur = jnp.where(pick, NEG_INF, cur)
    bias = jnp.concatenate(
        [jnp.broadcast_to(jnp.where(jnp.sum(jnp.where(blk_lane == j, sel, 0.0), axis=1, keepdims=True) > 0.5,
                                    0.0, NEG_INF), (heads, brow)) for j in range(n_blk)], axis=1)

    s_new = jnp.sum(q * _expand_kv_rows(kn), axis=1, keepdims=True)
    sm = jnp.where(own_kv, s + bias, NEG_INF)
    m = jnp.maximum(jnp.max(sm, axis=1, keepdims=True), s_new)
    p = jnp.exp(sm - m)
    p_new = jnp.exp(s_new - m)
    denom = jnp.sum(p, axis=1, keepdims=True) + p_new
    return p.astype(BF16), p_new, denom


def _moba_decode_values(pb, p_new, denom, vn, v_refs, n_pages, page):
    prow = page * MOBA_KV_HEADS
    o = p_new * _expand_kv_rows(vn)
    for pg in range(n_pages):
        o = o + jnp.dot(pb[:, prow * pg:prow * (pg + 1)], v_refs[pg][0, 0].astype(BF16),
                        preferred_element_type=F32)
    return o * (1.0 / denom)


def _moba_decode(page_table, q3, kn3, vn3, g3, cache_k, cache_v, layer):
    db, n_pages = page_table.shape
    depth, n_pool, page = cache_k.shape[0], cache_k.shape[1], cache_k.shape[2]
    assert (n_pages * page) % MOBA_BLOCK == 0
    prow = page * MOBA_KV_HEADS
    ck = cache_k.reshape(depth, n_pool, prow, MOBA_HD)
    cv = cache_v.reshape(depth, n_pool, prow, MOBA_HD)

    n_seq = _MOBA_DEC_SEQS
    assert db % n_seq == 0

    def page_spec(sq, pg):
        return pl.BlockSpec((1, 1, prow, MOBA_HD), lambda b, pt: (layer, pt[b * n_seq + sq, pg], 0, 0))

    def row_spec(r):
        return pl.BlockSpec((n_seq, r, MOBA_HD), lambda b, pt: (b, 0, 0))

    pages = [page_spec(sq, pg) for sq in range(n_seq) for pg in range(n_pages)]
    grid_spec = pltpu.PrefetchScalarGridSpec(
        num_scalar_prefetch=1,
        grid=(db // n_seq,),
        in_specs=[row_spec(MOBA_HEADS), row_spec(MOBA_KV_HEADS), row_spec(MOBA_KV_HEADS),
                  row_spec(MOBA_HEADS)] + pages + pages,
        out_specs=pl.BlockSpec((n_seq, MOBA_HEADS, MOBA_HD), lambda b, pt: (b, 0, 0)),
    )
    n_ops = n_seq * n_pages
    return pl.pallas_call(
        functools.partial(_moba_decode_kernel, n_pages=n_pages, page=page),
        grid_spec=grid_spec,
        out_shape=jax.ShapeDtypeStruct((db, MOBA_HEADS, MOBA_HD), BF16),
        compiler_params=_params("arbitrary"),
        name="moba_decode",
    )(page_table, q3, kn3, vn3, g3, *([ck] * n_ops), *([cv] * n_ops))


_GLA_DEC_GROUP = 8


def _gla_decode_kernel(q_ref, k_ref, la_ref, v_ref, gg_ref, gain_ref, s_ref, o_ref, sn_ref):
    grp = _GLA_DEC_GROUP

    def columns(x):
        xp = jnp.concatenate([x, jnp.zeros((LANE - grp, x.shape[1]), F32)], axis=0)
        return xp.T

    a_t = columns(jnp.exp(la_ref[...]))
    k_t = columns(k_ref[...])
    q_t = columns(q_ref[...] * (GLA_DK ** -0.5))
    gain = gain_ref[...]
    for i in range(grp):
        rows = []
        for h in range(GLA_HEADS):
            r = slice(GLA_DK * h, GLA_DK * (h + 1))
            v_h = jnp.broadcast_to(v_ref[i, h:h + 1, :], (GLA_DK, GLA_DV))
            s_new = a_t[r, i:i + 1] * s_ref[i, h] + k_t[r, i:i + 1] * v_h
            sn_ref[i, h] = s_new
            rows.append(jnp.sum(q_t[r, i:i + 1] * s_new, axis=0, keepdims=True))
        o = jnp.concatenate(rows, axis=0)
        ms = jnp.mean(o * o, axis=-1, keepdims=True)
        o = o * lax.rsqrt(ms + NORM_EPS) * gain
        o_ref[i] = (o * _silu(gg_ref[i])).astype(o_ref.dtype)


def _gla_decode(q2, k2, la2, v3, gg3, gain, state):
    db = q2.shape[0]
    grp = _GLA_DEC_GROUP
    assert db % grp == 0
    vec = pl.BlockSpec((grp, GLA_KW), lambda g: (g, 0))
    hd3 = pl.BlockSpec((grp, GLA_HEADS, GLA_DV), lambda g: (g, 0, 0))
    st = pl.BlockSpec((grp, GLA_HEADS, GLA_DK, GLA_DV), lambda g: (g, 0, 0, 0))
    return pl.pallas_call(
        _gla_decode_kernel,
        grid=(db // grp,),
        in_specs=[vec, vec, vec, hd3, hd3, pl.BlockSpec((1, GLA_DV), lambda g: (0, 0)), st],
        out_specs=[hd3, st],
        out_shape=[jax.ShapeDtypeStruct((db, GLA_HEADS, GLA_DV), BF16),
                   jax.ShapeDtypeStruct(state.shape, F32)],
        compiler_params=_params("arbitrary"),
        name="gla_decode",
    )(q2, k2, la2, v3, gg3, gain, state)


def _tile(n, prefs):
    for p in prefs:
        if n % p == 0:
            return p
    return n


def kernel(x_prompt, x_sample, cache_k, cache_v, state_gla, page_table,
           w_in, w_a2, b_a, gla_gain, w_out, ln_g, ln_b):
    bsz, t, d = x_prompt.shape
    db, dt, _ = x_sample.shape
    depth = w_in.shape[0]
    assert bsz == 1 and dt == 1
    alpha = (2.0 * depth) ** 0.25
    y_p = x_prompt.reshape(t, d)
    y_s = x_sample.reshape(db, d)
    kp_l, vp_l, sp_l, ks_l, vs_l, ss_l = [], [], [], [], [], []
    for l in range(depth):
        w_in_t = jnp.swapaxes(w_in[l], 0, 1)
        w_lr_t = jnp.zeros((LANE, d), BF16).at[:GLA_LOWRANK, :].set(w_in_t[OFF_GA:].astype(BF16))
        w_a2p = jnp.zeros((LANE, GLA_KW), F32).at[:GLA_LOWRANK, :].set(w_a2[l])
        w_out_bf = w_out[l].astype(BF16)
        gain = gla_gain[l].reshape(1, GLA_DV)
        lg, lb = ln_g[l].reshape(1, d), ln_b[l].reshape(1, d)
        x_all = _stack_cast(y_p, y_s, _tile(math.gcd(t, db), (128,)))
        m_all = t + db
        proj = _in_proj(x_all, w_in_t, MAIN_W, _tile(m_all, (520, 512, 256, 128)), 1024)
        ba2 = b_a[l].reshape(1, GLA_KW)
        decay = _log_decay(x_all, w_lr_t, w_a2p, ba2, 0, t, _tile(math.gcd(t, 512), (512,)), True)
        log_a_s = _log_decay(x_all, w_lr_t, w_a2p, ba2, t, db, db, False)
        o_m = _moba_prompt(proj, t)
        o_g, s_p = _gla_prompt(proj, decay, gain, t)
        k_rows, v_rows = _kv_rows(proj, t, _tile(t, (512, 256, 128)))
        kp_l.append(k_rows.reshape(bsz, t, MOBA_KV_HEADS, MOBA_HD))
        vp_l.append(v_rows.reshape(bsz, t, MOBA_KV_HEADS, MOBA_HD))
        sp_l.append(s_p.reshape(bsz, GLA_HEADS, GLA_DK, GLA_DV))
        y_p_new = _out_proj(o_g, o_m, w_out_bf, y_p, lg, lb, alpha,
                            _tile(t, (512, 256, 128)), _tile(d, (1024, 512, 256, 128)))
        proj_s = proj[t:]
        k_new = proj_s[:, OFF_MK:OFF_MV].reshape(db, MOBA_KV_HEADS, MOBA_HD)
        v_new = proj_s[:, OFF_MV:OFF_MG].reshape(db, MOBA_KV_HEADS, MOBA_HD)
        o_m_s = _moba_decode(page_table,
                             proj_s[:, OFF_MQ:OFF_MK].reshape(db, MOBA_HEADS, MOBA_HD),
                             k_new, v_new,
                             proj_s[:, OFF_MG:OFF_GQ].reshape(db, MOBA_HEADS, MOBA_HD),
                             cache_k, cache_v, l)
        o_g_s, s_s = _gla_decode(proj_s[:, OFF_GQ:OFF_GK], proj_s[:, OFF_GK:OFF_GV], log_a_s,
                                 proj_s[:, OFF_GV:OFF_GG].reshape(db, GLA_HEADS, GLA_DV),
                                 proj_s[:, OFF_GG:OFF_GA].reshape(db, GLA_HEADS, GLA_DV),
                                 gain, state_gla[l])
        ks_l.append(k_new.reshape(db, dt, MOBA_KV_HEADS, MOBA_HD))
        vs_l.append(v_new.reshape(db, dt, MOBA_KV_HEADS, MOBA_HD))
        ss_l.append(s_s)
        y_s = _out_proj(o_g_s.reshape(db, GLA_VW), o_m_s.reshape(db, MOBA_QW), w_out_bf, y_s, lg, lb,
                        alpha, _tile(db, (128,)), _tile(d, (1024, 512, 256, 128)))
        y_p = y_p_new
    return (y_p.reshape(bsz, t, d), y_s.reshape(db, dt, d),
            jnp.stack(kp_l), jnp.stack(vp_l), jnp.stack(sp_l),
            jnp.stack(ks_l), jnp.stack(vs_l), jnp.stack(ss_l))
```

```python
import functools
import math

import jax
import jax.numpy as jnp
from jax import lax
from jax.experimental import pallas as pl
from jax.experimental.pallas import tpu as pltpu

F32 = jnp.float32
BF16 = jnp.bfloat16
NEG_INF = float("-inf")
MASK_BIAS = -1e30
LOG2E = 1.4426950408889634

MOBA_HEADS = 16
MOBA_KV_HEADS = 4
MOBA_GROUP = MOBA_HEADS // MOBA_KV_HEADS
MOBA_HD = 128
MOBA_BLOCK = 256
MOBA_TOPK = 3
GLA_HEADS = 16
GLA_DK = 64
GLA_DV = 128
GLA_LOWRANK = 16
GLA_GATE_TAU = 16.0
GLA_SUB = 16
GLA_PAIRS = GLA_HEADS // 2
NORM_EPS = 1e-5

MOBA_QW = MOBA_HEADS * MOBA_HD
MOBA_KVW = MOBA_KV_HEADS * MOBA_HD
GLA_KW = GLA_HEADS * GLA_DK
GLA_VW = GLA_HEADS * GLA_DV
OFF_MQ = 0
OFF_MK = OFF_MQ + MOBA_QW
OFF_MV = OFF_MK + MOBA_KVW
OFF_MG = OFF_MV + MOBA_KVW
OFF_GQ = OFF_MG + MOBA_QW
OFF_GK = OFF_GQ + GLA_KW
OFF_GV = OFF_GK + GLA_KW
OFF_GG = OFF_GV + GLA_VW
OFF_GA = OFF_GG + GLA_VW
MAIN_W = OFF_GA

LANE = 128
BF16_SUBLANES = 16
VMEM_LIMIT_BYTES = 60 * 1024 * 1024

_NT = (((1,), (1,)), ((), ()))
_TN = (((0,), (0,)), ((), ()))


def _params(*sem):
    return pltpu.CompilerParams(dimension_semantics=sem, vmem_limit_bytes=VMEM_LIMIT_BYTES)


def _silu(x):
    return x * (1.0 / (1.0 + jnp.exp(-x)))


def _split_bf16(x):
    hi = x.astype(BF16)
    return hi, (x - hi.astype(F32)).astype(BF16)


def _stack_cast_kernel(xp_ref, xs_ref, o_ref, *, n_prompt_tiles):
    i = pl.program_id(0)

    @pl.when(i < n_prompt_tiles)
    def _prompt_rows():
        o_ref[...] = xp_ref[...].astype(o_ref.dtype)

    @pl.when(i >= n_prompt_tiles)
    def _decode_rows():
        o_ref[...] = xs_ref[...].astype(o_ref.dtype)


def _stack_cast(xp, xs, tr):
    (t, d), db = xp.shape, xs.shape[0]
    assert t % tr == 0 and db % tr == 0
    npt, nst = t // tr, db // tr
    return pl.pallas_call(
        functools.partial(_stack_cast_kernel, n_prompt_tiles=npt),
        grid=(npt + nst,),
        in_specs=[pl.BlockSpec((tr, d), lambda i: (jnp.minimum(i, npt - 1), 0)),
                  pl.BlockSpec((tr, d), lambda i: (jnp.maximum(i - npt, 0), 0))],
        out_specs=pl.BlockSpec((tr, d), lambda i: (i, 0)),
        out_shape=jax.ShapeDtypeStruct((t + db, d), BF16),
        compiler_params=_params("arbitrary"),
        name="stack_cast",
    )(xp, xs)


def _in_proj_kernel(x_ref, wt_ref, o_ref, wbf_ref):
    @pl.when(pl.program_id(1) == 0)
    def _cast_weight_tile():
        wbf_ref[...] = wt_ref[...].astype(BF16)

    o_ref[...] = lax.dot_general(x_ref[...], wbf_ref[...], _NT, preferred_element_type=F32)


def _in_proj(x, wt, n_out, tm, tn):
    m, k = x.shape
    assert n_out % tn == 0 and m % tm == 0 and wt.shape[0] >= n_out
    return pl.pallas_call(
        _in_proj_kernel,
        grid=(n_out // tn, m // tm),
        in_specs=[pl.BlockSpec((tm, k), lambda j, i: (i, 0)),
                  pl.BlockSpec((tn, k), lambda j, i: (j, 0))],
        out_specs=pl.BlockSpec((tm, tn), lambda j, i: (i, j)),
        out_shape=jax.ShapeDtypeStruct((m, n_out), F32),
        scratch_shapes=[pltpu.VMEM((tn, k), BF16)],
        compiler_params=_params("arbitrary", "arbitrary"),
        name="in_proj",
    )(x, wt)


def _kv_rows_kernel(k_ref, v_ref, ko_ref, vo_ref):
    tok = k_ref.shape[0]
    for h in range(MOBA_KV_HEADS):
        cols = slice(MOBA_HD * h, MOBA_HD * (h + 1))
        ko_ref[pl.ds(h, tok, stride=MOBA_KV_HEADS), :] = k_ref[:, cols]
        vo_ref[pl.ds(h, tok, stride=MOBA_KV_HEADS), :] = v_ref[:, cols]


def _kv_rows(proj, t, tr):
    assert t % tr == 0
    out = jax.ShapeDtypeStruct((t * MOBA_KV_HEADS, MOBA_HD), F32)
    return pl.pallas_call(
        _kv_rows_kernel,
        grid=(t // tr,),
        in_specs=[pl.BlockSpec((tr, MOBA_KVW), lambda i: (i, OFF_MK // MOBA_KVW)),
                  pl.BlockSpec((tr, MOBA_KVW), lambda i: (i, OFF_MV // MOBA_KVW))],
        out_specs=[pl.BlockSpec((tr * MOBA_KV_HEADS, MOBA_HD), lambda i: (i, 0)),
                   pl.BlockSpec((tr * MOBA_KV_HEADS, MOBA_HD), lambda i: (i, 0))],
        out_shape=[out, out],
        compiler_params=_params("arbitrary"),
        name="kv_rows",
    )(proj, proj)


def _log_decay_kernel(x_ref, wlr_ref, wa2_ref, ba_ref, o_ref, *, cumulative):
    ga = lax.dot_general(x_ref[...], wlr_ref[...], _NT, preferred_element_type=F32)
    ga_hi, ga_lo = _split_bf16(ga)
    wa_hi, wa_lo = _split_bf16(wa2_ref[...])
    z = (jnp.dot(ga_hi, wa_hi, preferred_element_type=F32)
         + (jnp.dot(ga_lo, wa_hi, preferred_element_type=F32)
            + jnp.dot(ga_hi, wa_lo, preferred_element_type=F32))) + ba_ref[...]
    log_a = (jnp.minimum(z, 0.0) - jnp.log(1.0 + jnp.exp(-jnp.abs(z)))) * (1.0 / GLA_GATE_TAU)
    if cumulative:
        pos = lax.broadcasted_iota(jnp.int32, log_a.shape, 0) % GLA_SUB
        step = 1
        while step < GLA_SUB:
            log_a = log_a + jnp.where(pos >= step, pltpu.roll(log_a, step, axis=0), 0.0)
            step *= 2
    o_ref[...] = log_a


def _log_decay(x, w_lr, w_a2p, b_a, row0, n_rows, tm, cumulative):
    k = x.shape[1]
    assert row0 % tm == 0 and n_rows % tm == 0 and tm % GLA_SUB == 0
    t0 = row0 // tm
    return pl.pallas_call(
        functools.partial(_log_decay_kernel, cumulative=cumulative),
        grid=(n_rows // tm,),
        in_specs=[pl.BlockSpec((tm, k), lambda i: (t0 + i, 0)),
                  pl.BlockSpec((LANE, k), lambda i: (0, 0)),
                  pl.BlockSpec((LANE, GLA_KW), lambda i: (0, 0)),
                  pl.BlockSpec((1, GLA_KW), lambda i: (0, 0))],
        out_specs=pl.BlockSpec((tm, GLA_KW), lambda i: (i, 0)),
        out_shape=jax.ShapeDtypeStruct((n_rows, GLA_KW), F32),
        compiler_params=_params("arbitrary"),
        name="gla_log_decay",
    )(x, w_lr, w_a2p, b_a)


def _topk_bias(gate, n_valid, n_rows):
    blk = lax.broadcasted_iota(jnp.int32, gate.shape, 0)
    cur = jnp.where(blk < n_valid, gate, NEG_INF)
    sel = jnp.zeros(gate.shape, jnp.bool_)
    for _ in range(MOBA_TOPK):
        mx = jnp.max(cur, axis=0, keepdims=True)
        hit = jnp.logical_and(cur == mx, mx > NEG_INF)
        first = jnp.min(jnp.where(hit, blk, n_rows), axis=0, keepdims=True)
        pick = blk == first
        sel = jnp.logical_or(sel, pick)
        cur = jnp.where(pick, NEG_INF, cur)
    return jnp.where(sel, 0.0, MASK_BIAS)


_MOBA_KV_PER_STEP = 2


def _moba_prompt_kernel(q_ref, k_ref, v_ref, g_ref, o_ref,
                        kaug_ref, vt_ref, kmean_ref, qaug_ref, m_ref, acc_ref, sa_ref, sb_ref,
                        *, n_blocks):
    qi = pl.program_id(1)
    blk_sz = MOBA_BLOCK
    hd = MOBA_HD
    rows = MOBA_GROUP * blk_sz
    gw = MOBA_GROUP * hd
    chains = range(_MOBA_KV_PER_STEP)

    @pl.when(qi == 0)
    def _prepare_kv_heads():
        kmean_ref[...] = jnp.zeros(kmean_ref.shape, F32)
        vt_ref[:, hd:, :] = jnp.ones((vt_ref.shape[0], vt_ref.shape[1] - hd, vt_ref.shape[2]), BF16)
        lane_blk = lax.broadcasted_iota(jnp.int32, (blk_sz, LANE), 1)

        def body(j, carry):
            r0 = pl.multiple_of(j * blk_sz, blk_sz)
            for c in chains:
                kb = k_ref[pl.ds(r0, blk_sz), hd * c:hd * (c + 1)]
                kaug_ref[c, pl.ds(r0, blk_sz), 0:hd] = kb.astype(BF16)
                kaug_ref[c, pl.ds(r0, blk_sz), hd:hd + LANE] = jnp.where(lane_blk == j, 1.0, 0.0).astype(BF16)
                kmean_ref[c, pl.ds(j, 1), :] = jnp.sum(kb, axis=0, keepdims=True) * (1.0 / blk_sz)
                vt_ref[c, 0:hd, pl.ds(r0, blk_sz)] = v_ref[pl.ds(r0, blk_sz), hd * c:hd * (c + 1)].T.astype(BF16)
            return carry

        lax.fori_loop(0, n_blocks, body, 0)

    nb_pad = kmean_ref.shape[1]
    for c in chains:
        q = q_ref[:, gw * c:gw * (c + 1)]
        qs = jnp.concatenate([q[:, hd * g:hd * (g + 1)] for g in range(MOBA_GROUP)], axis=0)
        gate = lax.dot_general(kmean_ref[c], qs, _NT, precision=lax.Precision.HIGHEST,
                               preferred_element_type=F32)
        bias = _topk_bias(gate, qi, nb_pad)
        qaug_ref[c, 0:hd, :] = (qs * (hd ** -0.5 * LOG2E)).T.astype(BF16)
        qaug_ref[c, hd:hd + nb_pad, :] = bias.astype(BF16)
        qaug_ref[c, hd + nb_pad:, :] = jnp.full((LANE - nb_pad, rows), MASK_BIAS, BF16)

    def scores(c, r0):
        return jnp.dot(kaug_ref[c, pl.ds(r0, blk_sz), :], qaug_ref[c],
                       preferred_element_type=F32)

    def accumulate(c, s, r0):
        m_prev = m_ref[c]
        m_new = jnp.maximum(m_prev, jnp.max(s, axis=0, keepdims=True))
        p = jnp.exp2(s - m_new).astype(BF16)
        pv = jnp.dot(vt_ref[c, :, pl.ds(r0, blk_sz)], p, preferred_element_type=F32)
        acc_ref[c] = jnp.exp2(m_prev - m_new) * acc_ref[c] + pv
        m_ref[c] = m_new

    for c in chains:
        sa_ref[c] = scores(c, 0)

    r_own = pl.multiple_of(qi * blk_sz, blk_sz)
    key_t = lax.broadcasted_iota(jnp.int32, (blk_sz, rows), 0)
    row_t = lax.broadcasted_iota(jnp.int32, (blk_sz, rows), 1) % blk_sz
    for c in chains:
        s = jnp.dot(kaug_ref[c, pl.ds(r_own, blk_sz), 0:hd], qaug_ref[c, 0:hd, :],
                    preferred_element_type=F32)
        s = jnp.where(key_t <= row_t, s, NEG_INF)
        m0 = jnp.max(s, axis=0, keepdims=True)
        m_ref[c] = m0
        acc_ref[c] = jnp.dot(vt_ref[c, :, pl.ds(r_own, blk_sz)], jnp.exp2(s - m0).astype(BF16),
                             preferred_element_type=F32)

    n_pairs = (qi + 1) // 2

    def past_pair(jj, carry):
        r0 = pl.multiple_of(jj * (2 * blk_sz), 2 * blk_sz)
        r1 = pl.multiple_of(r0 + blk_sz, blk_sz)
        r2 = pl.multiple_of(jnp.minimum(r0 + 2 * blk_sz, (n_blocks - 1) * blk_sz), blk_sz)
        for c in chains:
            sb_ref[c] = scores(c, r1)
        for c in chains:
            accumulate(c, sa_ref[c], r0)
        for c in chains:
            sa_ref[c] = scores(c, r2)
        for c in chains:
            accumulate(c, sb_ref[c], r1)
        return carry

    lax.fori_loop(0, n_pairs, past_pair, 0)

    for c in chains:
        acc = acc_ref[c]
        o = (acc[0:hd, :] * (1.0 / acc[hd:hd + 1, :])).T
        o = jnp.concatenate([o[blk_sz * g:blk_sz * (g + 1), :] for g in range(MOBA_GROUP)], axis=1)
        o_ref[:, gw * c:gw * (c + 1)] = (o * _silu(g_ref[:, gw * c:gw * (c + 1)])).astype(o_ref.dtype)


def _moba_prompt(proj, t):
    assert t % (2 * MOBA_BLOCK) == 0
    nb = t // MOBA_BLOCK
    nb_pad = -(-nb // BF16_SUBLANES) * BF16_SUBLANES
    assert nb_pad <= LANE
    rows = MOBA_GROUP * MOBA_BLOCK
    nc = _MOBA_KV_PER_STEP
    gw = MOBA_GROUP * MOBA_HD * nc
    kvw = MOBA_HD * nc
    assert MOBA_KV_HEADS % nc == 0
    assert OFF_MQ % gw == 0 and OFF_MG % gw == 0 and OFF_MK % kvw == 0 and OFF_MV % kvw == 0
    kv_spec = functools.partial(pl.BlockSpec, (t, kvw), pipeline_mode=pl.Buffered(1))
    return pl.pallas_call(
        functools.partial(_moba_prompt_kernel, n_blocks=nb),
        grid=(MOBA_KV_HEADS // nc, nb),
        in_specs=[pl.BlockSpec((MOBA_BLOCK, gw), lambda h, i: (i, OFF_MQ // gw + h)),
                  kv_spec(lambda h, i: (0, OFF_MK // kvw + h)),
                  kv_spec(lambda h, i: (0, OFF_MV // kvw + h)),
                  pl.BlockSpec((MOBA_BLOCK, gw), lambda h, i: (i, OFF_MG // gw + h))],
        out_specs=pl.BlockSpec((MOBA_BLOCK, gw), lambda h, i: (i, h)),
        out_shape=jax.ShapeDtypeStruct((t, MOBA_QW), BF16),
        scratch_shapes=[pltpu.VMEM((nc, t, MOBA_HD + LANE), BF16),
                        pltpu.VMEM((nc, MOBA_HD + BF16_SUBLANES, t), BF16),
                        pltpu.VMEM((nc, nb_pad, MOBA_HD), F32),
                        pltpu.VMEM((nc, MOBA_HD + LANE, rows), BF16),
                        pltpu.VMEM((nc, 1, rows), F32),
                        pltpu.VMEM((nc, MOBA_HD + BF16_SUBLANES, rows), F32),
                        pltpu.VMEM((nc, MOBA_BLOCK, rows), F32),
                        pltpu.VMEM((nc, MOBA_BLOCK, rows), F32)],
        compiler_params=_params("arbitrary", "arbitrary"),
        name="moba_prompt",
    )(proj, proj, proj, proj)


_GLA_GROUP = 4
_GLA_PAIRS_PER_STEP = 8
_GLA_PAIRS_PER_VCHUNK = 4


def _gla_prompt_kernel(q_ref, k_ref, b_ref, *rest, n_tblocks, tb):
    ti = pl.program_id(1)
    sub = GLA_SUB
    kw = 2 * GLA_DK
    vw = 2 * GLA_DV
    npp = _GLA_PAIRS_PER_STEP
    nvc = npp // _GLA_PAIRS_PER_VCHUNK
    v_refs, gg_refs = rest[:nvc], rest[nvc:2 * nvc]
    gain_ref, o_ref, sout_ref, s_ref, oacc_ref = rest[2 * nvc:]

    @pl.when(ti == 0)
    def _zero_state():
        s_ref[...] = jnp.zeros(s_ref.shape, F32)

    row_h = lax.broadcasted_iota(jnp.int32, (kw, vw), 0) // GLA_DK
    col_h = lax.broadcasted_iota(jnp.int32, (kw, vw), 1) // GLA_DV
    same_head = row_h == col_h
    head_sum = jnp.where(same_head, 1.0, 0.0).astype(BF16)
    j_idx = lax.broadcasted_iota(jnp.int32, (sub, kw), 0)
    grp = _GLA_GROUP
    lane_head = lax.broadcasted_iota(jnp.int32, (grp * sub, kw), 1) // GLA_DK
    gt = grp * sub
    half = sub // 2
    sub_rows = half * half + half * sub
    rs_c = lax.broadcasted_iota(jnp.int32, (gt, grp * sub_rows), 1)
    rs_in = rs_c % sub_rows
    rs_tok = (rs_c // sub_rows) * sub + jnp.where(rs_in < half * half, rs_in // half,
                                                  half + (rs_in - half * half) // sub)
    row_sum = jnp.where(lax.broadcasted_iota(jnp.int32, (gt, grp * sub_rows), 0) == rs_tok,
                        1.0, 0.0).astype(BF16)

    def pair_group(q, k, b, v, state):
        w_rows, v_rows = [], []
        for s in range(grp):
            lo = s * sub
            b_s, k_s, v_s = b[lo:lo + sub, :], k[lo:lo + sub, :], v[lo:lo + sub, :]
            for i in range(sub):
                nj = half if i < half else sub
                d = b_s[i:i + 1, :] - b_s[0:nj, :]
                dec = jnp.exp(jnp.where(j_idx[0:nj, :] < i, d, NEG_INF))
                w_rows.append(dec * k_s[0:nj, :] * q[lo + i:lo + i + 1, :])
                v_rows.append(v_s[0:nj, :])
        w = jnp.concatenate(w_rows, axis=0).astype(BF16)
        sc = jnp.dot(w, head_sum, preferred_element_type=F32)
        z = (sc * jnp.concatenate(v_rows, axis=0)).astype(BF16)
        o_diag = jnp.dot(row_sum, z, preferred_element_type=F32)
        qk = q * k
        o_self = jnp.concatenate(
            [jnp.sum(jnp.where(lane_head == h, qk, 0.0), axis=1, keepdims=True)
             * v[:, GLA_DV * h:GLA_DV * (h + 1)] for h in range(2)], axis=1)
        o_diag = o_diag + o_self
        q_dec = (q * jnp.exp(b)).astype(BF16)
        upds, decays = [], []
        for s in range(grp):
            lo = s * sub
            b_s, k_s, v_s = b[lo:lo + sub, :], k[lo:lo + sub, :], v[lo:lo + sub, :]
            b_last = b_s[sub - 1:sub, :]
            k_dec = (k_s * jnp.exp(b_last - b_s)).astype(BF16)
            upd = lax.dot_general(k_dec, v_s.astype(BF16), _TN, preferred_element_type=F32)
            upds.append(jnp.where(same_head, upd, 0.0))
            e_col = jnp.broadcast_to(jnp.exp(b_last), (kw, kw)).T
            decays.append(jnp.concatenate([e_col, e_col], axis=1))
        o_inter = []
        for s in range(grp):
            lo = s * sub
            o_inter.append(jnp.dot(q_dec[lo:lo + sub, :], state.astype(BF16),
                                   preferred_element_type=F32))
            state = state * decays[s] + upds[s]
        return jnp.concatenate(o_inter, axis=0) + o_diag, state

    def group(g, carry):
        r0 = pl.multiple_of(g * gt, gt)
        for pp in range(npp):
            ks, vs = slice(kw * pp, kw * (pp + 1)), slice(vw * pp, vw * (pp + 1))
            pc = pp % _GLA_PAIRS_PER_VCHUNK
            v_pair = v_refs[pp // _GLA_PAIRS_PER_VCHUNK][pl.ds(r0, gt), vw * pc:vw * (pc + 1)]
            o, state = pair_group(q_ref[pl.ds(r0, gt), ks] * (GLA_DK ** -0.5), k_ref[pl.ds(r0, gt), ks],
                                  b_ref[pl.ds(r0, gt), ks], v_pair, s_ref[pp])
            s_ref[pp] = state
            oacc_ref[pl.ds(r0, gt), vs] = o
        return carry

    lax.fori_loop(0, tb // gt, group, 0)

    o = oacc_ref[...]
    gain = gain_ref[...]
    heads = []
    for h in range(2 * npp):
        oh = o[:, GLA_DV * h:GLA_DV * (h + 1)]
        ms = jnp.mean(oh * oh, axis=-1, keepdims=True)
        heads.append(oh * lax.rsqrt(ms + NORM_EPS) * gain)
    o = jnp.concatenate(heads, axis=1)
    gate = jnp.concatenate([r[...] for r in gg_refs], axis=1)
    o_ref[...] = (o * _silu(gate)).astype(o_ref.dtype)

    @pl.when(ti == n_tblocks - 1)
    def _emit_state():
        for pp in range(npp):
            st = s_ref[pp]
            sout_ref[2 * pp] = st[0:GLA_DK, 0:GLA_DV]
            sout_ref[2 * pp + 1] = st[GLA_DK:2 * GLA_DK, GLA_DV:2 * GLA_DV]


def _gla_prompt(proj, decay, gain, t):
    tb = 256 if t % 256 == 0 else t
    assert t % tb == 0 and tb % (GLA_SUB * _GLA_GROUP) == 0
    nt = t // tb
    npp = _GLA_PAIRS_PER_STEP
    kw, vw = 2 * GLA_DK * npp, 2 * GLA_DV * npp
    vcw = 2 * GLA_DV * _GLA_PAIRS_PER_VCHUNK
    nvc = npp // _GLA_PAIRS_PER_VCHUNK
    assert GLA_PAIRS % npp == 0 and npp % _GLA_PAIRS_PER_VCHUNK == 0
    assert OFF_GQ % kw == 0 and OFF_GK % kw == 0 and OFF_GV % vcw == 0 and OFF_GG % vcw == 0

    def vchunk(off, c):
        return pl.BlockSpec((tb, vcw), lambda p, i: (i, off // vcw + p * nvc + c))

    return pl.pallas_call(
        functools.partial(_gla_prompt_kernel, n_tblocks=nt, tb=tb),
        grid=(GLA_PAIRS // npp, nt),
        in_specs=[pl.BlockSpec((tb, kw), lambda p, i: (i, OFF_GQ // kw + p)),
                  pl.BlockSpec((tb, kw), lambda p, i: (i, OFF_GK // kw + p)),
                  pl.BlockSpec((tb, kw), lambda p, i: (i, p))]
                 + [vchunk(OFF_GV, c) for c in range(nvc)]
                 + [vchunk(OFF_GG, c) for c in range(nvc)]
                 + [pl.BlockSpec((1, GLA_DV), lambda p, i: (0, 0))],
        out_specs=[pl.BlockSpec((tb, vw), lambda p, i: (i, p)),
                   pl.BlockSpec((2 * npp, GLA_DK, GLA_DV), lambda p, i: (p, 0, 0))],
        out_shape=[jax.ShapeDtypeStruct((t, GLA_VW), BF16),
                   jax.ShapeDtypeStruct((GLA_HEADS, GLA_DK, GLA_DV), F32)],
        scratch_shapes=[pltpu.VMEM((npp, 2 * GLA_DK, 2 * GLA_DV), F32),
                        pltpu.VMEM((tb, vw), F32)],
        compiler_params=_params("arbitrary", "arbitrary"),
        name="gla_prompt",
    )(proj, proj, decay, *([proj] * (2 * nvc)), gain)


def _out_proj_kernel(og_ref, om_ref, wg_ref, wm_ref, x_ref, lg_ref, lb_ref, o_ref,
                     *, n_ctiles, tn, alpha):
    j = pl.program_id(1)
    mixed = jnp.dot(og_ref[...], wg_ref[...], preferred_element_type=F32)
    mixed = mixed + jnp.dot(om_ref[...], wm_ref[...], preferred_element_type=F32)
    c0 = pl.multiple_of(j * tn, tn)
    o_ref[:, pl.ds(c0, tn)] = alpha * x_ref[...] + mixed

    @pl.when(j == n_ctiles - 1)
    def _layer_norm():
        h = o_ref[...]
        mu = jnp.mean(h, axis=-1, keepdims=True)
        hc = h - mu
        var = jnp.mean(hc * hc, axis=-1, keepdims=True)
        o_ref[...] = hc * lax.rsqrt(var + NORM_EPS) * lg_ref[...] + lb_ref[...]


def _out_proj(og, om, w_out_bf, x, ln_g, ln_b, alpha, tm, tn):
    m, d = x.shape
    kg, km = og.shape[1], om.shape[1]
    assert kg == km and w_out_bf.shape[0] == kg + km
    nct = d // tn
    return pl.pallas_call(
        functools.partial(_out_proj_kernel, n_ctiles=nct, tn=tn, alpha=alpha),
        grid=(m // tm, nct),
        in_specs=[pl.BlockSpec((tm, kg), lambda i, j: (i, 0)),
                  pl.BlockSpec((tm, km), lambda i, j: (i, 0)),
                  pl.BlockSpec((kg, tn), lambda i, j: (0, j)),
                  pl.BlockSpec((km, tn), lambda i, j: (1, j)),
                  pl.BlockSpec((tm, tn), lambda i, j: (i, j)),
                  pl.BlockSpec((1, d), lambda i, j: (0, 0)),
                  pl.BlockSpec((1, d), lambda i, j: (0, 0))],
        out_specs=pl.BlockSpec((tm, d), lambda i, j: (i, 0)),
        out_shape=jax.ShapeDtypeStruct((m, d), F32),
        compiler_params=_params("arbitrary", "arbitrary"),
        name="out_proj_ln",
    )(og, om, w_out_bf, w_out_bf, x, ln_g, ln_b)


_MOBA_DEC_SEQS = 2


def _moba_decode_kernel(pt_ref, q_ref, kn_ref, vn_ref, g_ref, *rest, n_pages, page):
    del pt_ref
    n_seq = _MOBA_DEC_SEQS
    o_ref = rest[2 * n_seq * n_pages]
    k_refs = [rest[sq * n_pages:(sq + 1) * n_pages] for sq in range(n_seq)]
    v_refs = [rest[(n_seq + sq) * n_pages:(n_seq + sq + 1) * n_pages] for sq in range(n_seq)]
    qs = [q_ref[sq] * (MOBA_HD ** -0.5) for sq in range(n_seq)]
    ss = [_moba_decode_scores(qs[sq], k_refs[sq], n_pages) for sq in range(n_seq)]
    ps = [_moba_decode_select(ss[sq], qs[sq], kn_ref[sq], n_pages, page) for sq in range(n_seq)]
    for sq in range(n_seq):
        pb, p_new, denom = ps[sq]
        o = _moba_decode_values(pb, p_new, denom, vn_ref[sq], v_refs[sq], n_pages, page)
        o_ref[sq] = (o * _silu(g_ref[sq])).astype(o_ref.dtype)


def _moba_decode_scores(q, k_refs, n_pages):
    qb = q.astype(BF16)
    return jnp.concatenate(
        [lax.dot_general(qb, k_refs[p][0, 0].astype(BF16), _NT, preferred_element_type=F32)
         for p in range(n_pages)], axis=1)


def _expand_kv_rows(x):
    return jnp.concatenate([jnp.broadcast_to(x[i:i + 1, :], (MOBA_GROUP, MOBA_HD))
                            for i in range(MOBA_KV_HEADS)], axis=0)


def _moba_decode_select(s, q, kn, n_pages, page):
    heads = MOBA_HEADS
    brow = MOBA_BLOCK * MOBA_KV_HEADS
    n_blk = (n_pages * page) // MOBA_BLOCK
    lane = lax.broadcasted_iota(jnp.int32, s.shape, 1)
    head_kv = lax.broadcasted_iota(jnp.int32, s.shape, 0) // MOBA_GROUP
    own_kv = (lane % MOBA_KV_HEADS) == head_kv
    s_own = jnp.where(own_kv, s, 0.0)

    assert n_blk <= LANE
    blk_lane = lax.broadcasted_iota(jnp.int32, (heads, LANE), 1)
    cur = jnp.full((heads, LANE), NEG_INF, F32)
    for j in range(n_blk):
        mean_j = jnp.sum(s_own[:, brow * j:brow * (j + 1)], axis=1, keepdims=True) * (1.0 / MOBA_BLOCK)
        cur = jnp.where(blk_lane == j, mean_j, cur)
    sel = jnp.zeros((heads, LANE), F32)
    for _ in range(min(MOBA_TOPK, n_blk)):
        mx = jnp.max(cur, axis=1, keepdims=True)
        first = jnp.min(jnp.where(cur == mx, blk_lane, n_blk), axis=1, keepdims=True)
        pick = blk_lane == first
        sel = jnp.where(pick, 1.0, sel)
        cur = jnp.where(pick, NEG_INF, cur)
    bias = jnp.concatenate(
        [jnp.broadcast_to(jnp.where(jnp.sum(jnp.where(blk_lane == j, sel, 0.0), axis=1, keepdims=True) > 0.5,
                                    0.0, NEG_INF), (heads, brow)) for j in range(n_blk)], axis=1)

    s_new = jnp.sum(q * _expand_kv_rows(kn), axis=1, keepdims=True)
    sm = jnp.where(own_kv, s + bias, NEG_INF)
    m = jnp.maximum(jnp.max(sm, axis=1, keepdims=True), s_new)
    p = jnp.exp(sm - m)
    p_new = jnp.exp(s_new - m)
    denom = jnp.sum(p, axis=1, keepdims=True) + p_new
    return p.astype(BF16), p_new, denom


def _moba_decode_values(pb, p_new, denom, vn, v_refs, n_pages, page):
    prow = page * MOBA_KV_HEADS
    o = p_new * _expand_kv_rows(vn)
    for pg in range(n_pages):
        o = o + jnp.dot(pb[:, prow * pg:prow * (pg + 1)], v_refs[pg][0, 0].astype(BF16),
                        preferred_element_type=F32)
    return o * (1.0 / denom)


def _moba_decode(page_table, q3, kn3, vn3, g3, cache_k, cache_v, layer):
    db, n_pages = page_table.shape
    depth, n_pool, page = cache_k.shape[0], cache_k.shape[1], cache_k.shape[2]
    assert (n_pages * page) % MOBA_BLOCK == 0
    prow = page * MOBA_KV_HEADS
    ck = cache_k.reshape(depth, n_pool, prow, MOBA_HD)
    cv = cache_v.reshape(depth, n_pool, prow, MOBA_HD)

    n_seq = _MOBA_DEC_SEQS
    assert db % n_seq == 0

    def page_spec(sq, pg):
        return pl.BlockSpec((1, 1, prow, MOBA_HD), lambda b, pt: (layer, pt[b * n_seq + sq, pg], 0, 0))

    def row_spec(r):
        return pl.BlockSpec((n_seq, r, MOBA_HD), lambda b, pt: (b, 0, 0))

    pages = [page_spec(sq, pg) for sq in range(n_seq) for pg in range(n_pages)]
    grid_spec = pltpu.PrefetchScalarGridSpec(
        num_scalar_prefetch=1,
        grid=(db // n_seq,),
        in_specs=[row_spec(MOBA_HEADS), row_spec(MOBA_KV_HEADS), row_spec(MOBA_KV_HEADS),
                  row_spec(MOBA_HEADS)] + pages + pages,
        out_specs=pl.BlockSpec((n_seq, MOBA_HEADS, MOBA_HD), lambda b, pt: (b, 0, 0)),
    )
    n_ops = n_seq * n_pages
    return pl.pallas_call(
        functools.partial(_moba_decode_kernel, n_pages=n_pages, page=page),
        grid_spec=grid_spec,
        out_shape=jax.ShapeDtypeStruct((db, MOBA_HEADS, MOBA_HD), BF16),
        compiler_params=_params("arbitrary"),
        name="moba_decode",
    )(page_table, q3, kn3, vn3, g3, *([ck] * n_ops), *([cv] * n_ops))


_GLA_DEC_GROUP = 8


def _gla_decode_kernel(q_ref, k_ref, la_ref, v_ref, gg_ref, gain_ref, s_ref, o_ref, sn_ref):
    grp = _GLA_DEC_GROUP

    def columns(x):
        xp = jnp.concatenate([x, jnp.zeros((LANE - grp, x.shape[1]), F32)], axis=0)
        return xp.T

    a_t = columns(jnp.exp(la_ref[...]))
    k_t = columns(k_ref[...])
    q_t = columns(q_ref[...] * (GLA_DK ** -0.5))
    gain = gain_ref[...]
    for i in range(grp):
        rows = []
        for h in range(GLA_HEADS):
            r = slice(GLA_DK * h, GLA_DK * (h + 1))
            v_h = jnp.broadcast_to(v_ref[i, h:h + 1, :], (GLA_DK, GLA_DV))
            s_new = a_t[r, i:i + 1] * s_ref[i, h] + k_t[r, i:i + 1] * v_h
            sn_ref[i, h] = s_new
            rows.append(jnp.sum(q_t[r, i:i + 1] * s_new, axis=0, keepdims=True))
        o = jnp.concatenate(rows, axis=0)
        ms = jnp.mean(o * o, axis=-1, keepdims=True)
        o = o * lax.rsqrt(ms + NORM_EPS) * gain
        o_ref[i] = (o * _silu(gg_ref[i])).astype(o_ref.dtype)


def _gla_decode(q2, k2, la2, v3, gg3, gain, state):
    db = q2.shape[0]
    grp = _GLA_DEC_GROUP
    assert db % grp == 0
    vec = pl.BlockSpec((grp, GLA_KW), lambda g: (g, 0))
    hd3 = pl.BlockSpec((grp, GLA_HEADS, GLA_DV), lambda g: (g, 0, 0))
    st = pl.BlockSpec((grp, GLA_HEADS, GLA_DK, GLA_DV), lambda g: (g, 0, 0, 0))
    return pl.pallas_call(
        _gla_decode_kernel,
        grid=(db // grp,),
        in_specs=[vec, vec, vec, hd3, hd3, pl.BlockSpec((1, GLA_DV), lambda g: (0, 0)), st],
        out_specs=[hd3, st],
        out_shape=[jax.ShapeDtypeStruct((db, GLA_HEADS, GLA_DV), BF16),
                   jax.ShapeDtypeStruct(state.shape, F32)],
        compiler_params=_params("arbitrary"),
        name="gla_decode",
    )(q2, k2, la2, v3, gg3, gain, state)


def _tile(n, prefs):
    for p in prefs:
        if n % p == 0:
            return p
    return n


def kernel(x_prompt, x_sample, cache_k, cache_v, state_gla, page_table,
           w_in, w_a2, b_a, gla_gain, w_out, ln_g, ln_b):
    bsz, t, d = x_prompt.shape
    db, dt, _ = x_sample.shape
    depth = w_in.shape[0]
    assert bsz == 1 and dt == 1
    alpha = (2.0 * depth) ** 0.25
    y_p = x_prompt.reshape(t, d)
    y_s = x_sample.reshape(db, d)
    kp_l, vp_l, sp_l, ks_l, vs_l, ss_l = [], [], [], [], [], []
    for l in range(depth):
        w_in_t = jnp.swapaxes(w_in[l], 0, 1)
        w_lr_t = jnp.zeros((LANE, d), BF16).at[:GLA_LOWRANK, :].set(w_in_t[OFF_GA:].astype(BF16))
        w_a2p = jnp.zeros((LANE, GLA_KW), F32).at[:GLA_LOWRANK, :].set(w_a2[l])
        w_out_bf = w_out[l].astype(BF16)
        gain = gla_gain[l].reshape(1, GLA_DV)
        lg, lb = ln_g[l].reshape(1, d), ln_b[l].reshape(1, d)
        x_all = _stack_cast(y_p, y_s, _tile(math.gcd(t, db), (128,)))
        m_all = t + db
        proj = _in_proj(x_all, w_in_t, MAIN_W, _tile(m_all, (520, 512, 256, 128)), 1024)
        ba2 = b_a[l].reshape(1, GLA_KW)
        decay = _log_decay(x_all, w_lr_t, w_a2p, ba2, 0, t, _tile(math.gcd(t, 512), (512,)), True)
        log_a_s = _log_decay(x_all, w_lr_t, w_a2p, ba2, t, db, db, False)
        o_m = _moba_prompt(proj, t)
        o_g, s_p = _gla_prompt(proj, decay, gain, t)
        k_rows, v_rows = _kv_rows(proj, t, _tile(t, (512, 256, 128)))
        kp_l.append(k_rows.reshape(bsz, t, MOBA_KV_HEADS, MOBA_HD))
        vp_l.append(v_rows.reshape(bsz, t, MOBA_KV_HEADS, MOBA_HD))
        sp_l.append(s_p.reshape(bsz, GLA_HEADS, GLA_DK, GLA_DV))
        y_p_new = _out_proj(o_g, o_m, w_out_bf, y_p, lg, lb, alpha,
                            _tile(t, (512, 256, 128)), _tile(d, (1024, 512, 256, 128)))
        proj_s = proj[t:]
        k_new = proj_s[:, OFF_MK:OFF_MV].reshape(db, MOBA_KV_HEADS, MOBA_HD)
        v_new = proj_s[:, OFF_MV:OFF_MG].reshape(db, MOBA_KV_HEADS, MOBA_HD)
        o_m_s = _moba_decode(page_table,
                             proj_s[:, OFF_MQ:OFF_MK].reshape(db, MOBA_HEADS, MOBA_HD),
                             k_new, v_new,
                             proj_s[:, OFF_MG:OFF_GQ].reshape(db, MOBA_HEADS, MOBA_HD),
                             cache_k, cache_v, l)
        o_g_s, s_s = _gla_decode(proj_s[:, OFF_GQ:OFF_GK], proj_s[:, OFF_GK:OFF_GV], log_a_s,
                                 proj_s[:, OFF_GV:OFF_GG].reshape(db, GLA_HEADS, GLA_DV),
                                 proj_s[:, OFF_GG:OFF_GA].reshape(db, GLA_HEADS, GLA_DV),
                                 gain, state_gla[l])
        ks_l.append(k_new.reshape(db, dt, MOBA_KV_HEADS, MOBA_HD))
        vs_l.append(v_new.reshape(db, dt, MOBA_KV_HEADS, MOBA_HD))
        ss_l.append(s_s)
        y_s = _out_proj(o_g_s.reshape(db, GLA_VW), o_m_s.reshape(db, MOBA_QW), w_out_bf, y_s, lg, lb,
                        alpha, _tile(db, (128,)), _tile(d, (1024, 512, 256, 128)))
        y_p = y_p_new
    return (y_p.reshape(bsz, t, d), y_s.reshape(db, dt, d),
            jnp.stack(kp_l), jnp.stack(vp_l), jnp.stack(sp_l),
            jnp.stack(ks_l), jnp.stack(vs_l), jnp.stack(ss_l))
```

```python
import functools
import math

import jax
import jax.numpy as jnp
from jax import lax
from jax.experimental import pallas as pl
from jax.experimental.pallas import tpu as pltpu

F32 = jnp.float32
BF16 = jnp.bfloat16
NEG_INF = float("-inf")
MASK_BIAS = -1e30
LOG2E = 1.4426950408889634

MOBA_HEADS = 16
MOBA_KV_HEADS = 4
MOBA_GROUP = MOBA_HEADS // MOBA_KV_HEADS
MOBA_HD = 128
MOBA_BLOCK = 256
MOBA_TOPK = 3
GLA_HEADS = 16
GLA_DK = 64
GLA_DV = 128
GLA_LOWRANK = 16
GLA_GATE_TAU = 16.0
GLA_SUB = 16
GLA_PAIRS = GLA_HEADS // 2
NORM_EPS = 1e-5

MOBA_QW = MOBA_HEADS * MOBA_HD
MOBA_KVW = MOBA_KV_HEADS * MOBA_HD
GLA_KW = GLA_HEADS * GLA_DK
GLA_VW = GLA_HEADS * GLA_DV
OFF_MQ = 0
OFF_MK = OFF_MQ + MOBA_QW
OFF_MV = OFF_MK + MOBA_KVW
OFF_MG = OFF_MV + MOBA_KVW
OFF_GQ = OFF_MG + MOBA_QW
OFF_GK = OFF_GQ + GLA_KW
OFF_GV = OFF_GK + GLA_KW
OFF_GG = OFF_GV + GLA_VW
OFF_GA = OFF_GG + GLA_VW
MAIN_W = OFF_GA

LANE = 128
BF16_SUBLANES = 16
VMEM_LIMIT_BYTES = 60 * 1024 * 1024

_NT = (((1,), (1,)), ((), ()))
_TN = (((0,), (0,)), ((), ()))


def _params(*sem):
    return pltpu.CompilerParams(dimension_semantics=sem, vmem_limit_bytes=VMEM_LIMIT_BYTES)


def _silu(x):
    return x * (1.0 / (1.0 + jnp.exp(-x)))


def _split_bf16(x):
    hi = x.astype(BF16)
    return hi, (x - hi.astype(F32)).astype(BF16)


def _stack_cast_kernel(xp_ref, xs_ref, wlr_ref, o_ref, ga_ref, *, n_prompt_tiles):
    i = pl.program_id(0)

    def emit(x):
        xb = x.astype(o_ref.dtype)
        o_ref[...] = xb
        ga_ref[...] = lax.dot_general(xb, wlr_ref[...], _NT, preferred_element_type=F32)

    @pl.when(i < n_prompt_tiles)
    def _prompt_rows():
        emit(xp_ref[...])

    @pl.when(i >= n_prompt_tiles)
    def _decode_rows():
        emit(xs_ref[...])


def _stack_cast(xp, xs, w_lr_t, tr):
    (t, d), db = xp.shape, xs.shape[0]
    assert t % tr == 0 and db % tr == 0
    npt, nst = t // tr, db // tr
    return pl.pallas_call(
        functools.partial(_stack_cast_kernel, n_prompt_tiles=npt),
        grid=(npt + nst,),
        in_specs=[pl.BlockSpec((tr, d), lambda i: (jnp.minimum(i, npt - 1), 0)),
                  pl.BlockSpec((tr, d), lambda i: (jnp.maximum(i - npt, 0), 0)),
                  pl.BlockSpec((LANE, d), lambda i: (0, 0))],
        out_specs=[pl.BlockSpec((tr, d), lambda i: (i, 0)),
                   pl.BlockSpec((tr, LANE), lambda i: (i, 0))],
        out_shape=[jax.ShapeDtypeStruct((t + db, d), BF16),
                   jax.ShapeDtypeStruct((t + db, LANE), F32)],
        compiler_params=_params("arbitrary"),
        name="stack_cast",
    )(xp, xs, w_lr_t)


def _in_proj_kernel(x_ref, wt_ref, o_ref, wbf_ref):
    @pl.when(pl.program_id(1) == 0)
    def _cast_weight_tile():
        wbf_ref[...] = wt_ref[...].T.astype(BF16)

    o_ref[...] = jnp.dot(x_ref[...], wbf_ref[...], preferred_element_type=F32)


def _in_proj(x, wt, n_out, tm, tn):
    m, k = x.shape
    assert n_out % tn == 0 and m % tm == 0 and wt.shape[0] >= n_out
    return pl.pallas_call(
        _in_proj_kernel,
        grid=(n_out // tn, m // tm),
        in_specs=[pl.BlockSpec((tm, k), lambda j, i: (i, 0)),
                  pl.BlockSpec((tn, k), lambda j, i: (j, 0))],
        out_specs=pl.BlockSpec((tm, tn), lambda j, i: (i, j)),
        out_shape=jax.ShapeDtypeStruct((m, n_out), F32),
        scratch_shapes=[pltpu.VMEM((k, tn), BF16)],
        compiler_params=_params("arbitrary", "arbitrary"),
        name="in_proj",
    )(x, wt)


def _kv_rows_kernel(k_ref, v_ref, ko_ref, vo_ref):
    tok = k_ref.shape[0]
    for h in range(MOBA_KV_HEADS):
        cols = slice(MOBA_HD * h, MOBA_HD * (h + 1))
        ko_ref[pl.ds(h, tok, stride=MOBA_KV_HEADS), :] = k_ref[:, cols]
        vo_ref[pl.ds(h, tok, stride=MOBA_KV_HEADS), :] = v_ref[:, cols]


def _kv_rows(proj, t, tr):
    assert t % tr == 0
    out = jax.ShapeDtypeStruct((t * MOBA_KV_HEADS, MOBA_HD), F32)
    return pl.pallas_call(
        _kv_rows_kernel,
        grid=(t // tr,),
        in_specs=[pl.BlockSpec((tr, MOBA_KVW), lambda i: (i, OFF_MK // MOBA_KVW)),
                  pl.BlockSpec((tr, MOBA_KVW), lambda i: (i, OFF_MV // MOBA_KVW))],
        out_specs=[pl.BlockSpec((tr * MOBA_KV_HEADS, MOBA_HD), lambda i: (i, 0)),
                   pl.BlockSpec((tr * MOBA_KV_HEADS, MOBA_HD), lambda i: (i, 0))],
        out_shape=[out, out],
        compiler_params=_params("arbitrary"),
        name="kv_rows",
    )(proj, proj)


def _log_decay_kernel(ga_ref, wa2_ref, ba_ref, o_ref, *, cumulative):
    ga_hi, ga_lo = _split_bf16(ga_ref[...])
    wa_hi, wa_lo = _split_bf16(wa2_ref[...])
    z = (jnp.dot(ga_hi, wa_hi, preferred_element_type=F32)
         + (jnp.dot(ga_lo, wa_hi, preferred_element_type=F32)
            + jnp.dot(ga_hi, wa_lo, preferred_element_type=F32))) + ba_ref[...]
    log_a = (jnp.minimum(z, 0.0) - jnp.log(1.0 + jnp.exp(-jnp.abs(z)))) * (1.0 / GLA_GATE_TAU)
    if cumulative:
        pos = lax.broadcasted_iota(jnp.int32, log_a.shape, 0) % GLA_SUB
        step = 1
        while step < GLA_SUB:
            log_a = log_a + jnp.where(pos >= step, pltpu.roll(log_a, step, axis=0), 0.0)
            step *= 2
    o_ref[...] = log_a


def _log_decay(ga, w_a2p, b_a, row0, n_rows, tm, cumulative):
    assert row0 % tm == 0 and n_rows % tm == 0 and tm % GLA_SUB == 0
    t0 = row0 // tm
    return pl.pallas_call(
        functools.partial(_log_decay_kernel, cumulative=cumulative),
        grid=(n_rows // tm,),
        in_specs=[pl.BlockSpec((tm, LANE), lambda i: (t0 + i, 0)),
                  pl.BlockSpec((LANE, GLA_KW), lambda i: (0, 0)),
                  pl.BlockSpec((1, GLA_KW), lambda i: (0, 0))],
        out_specs=pl.BlockSpec((tm, GLA_KW), lambda i: (i, 0)),
        out_shape=jax.ShapeDtypeStruct((n_rows, GLA_KW), F32),
        compiler_params=_params("arbitrary"),
        name="gla_log_decay",
    )(ga, w_a2p, b_a)


def _topk_bias(gate, n_valid, n_rows):
    blk = lax.broadcasted_iota(jnp.int32, gate.shape, 0)
    cur = jnp.where(blk < n_valid, gate, NEG_INF)
    sel = jnp.zeros(gate.shape, jnp.bool_)
    for _ in range(MOBA_TOPK):
        mx = jnp.max(cur, axis=0, keepdims=True)
        hit = jnp.logical_and(cur == mx, mx > NEG_INF)
        first = jnp.min(jnp.where(hit, blk, n_rows), axis=0, keepdims=True)
        pick = blk == first
        sel = jnp.logical_or(sel, pick)
        cur = jnp.where(pick, NEG_INF, cur)
    return jnp.where(sel, 0.0, MASK_BIAS)


_MOBA_KV_PER_STEP = 2


def _moba_prompt_kernel(q_ref, k_ref, v_ref, g_ref, o_ref,
                        kaug_ref, vt_ref, kmean_ref, qaug_ref, m_ref, acc_ref, sa_ref, sb_ref,
                        *, n_blocks):
    qi = pl.program_id(1)
    blk_sz = MOBA_BLOCK
    hd = MOBA_HD
    rows = MOBA_GROUP * blk_sz
    gw = MOBA_GROUP * hd
    chains = range(_MOBA_KV_PER_STEP)

    @pl.when(qi == 0)
    def _prepare_kv_heads():
        kmean_ref[...] = jnp.zeros(kmean_ref.shape, F32)
        vt_ref[:, hd:, :] = jnp.ones((vt_ref.shape[0], vt_ref.shape[1] - hd, vt_ref.shape[2]), BF16)
        lane_blk = lax.broadcasted_iota(jnp.int32, (blk_sz, LANE), 1)

        def body(j, carry):
            r0 = pl.multiple_of(j * blk_sz, blk_sz)
            for c in chains:
                kb = k_ref[pl.ds(r0, blk_sz), hd * c:hd * (c + 1)]
                kaug_ref[c, pl.ds(r0, blk_sz), 0:hd] = kb.astype(BF16)
                kaug_ref[c, pl.ds(r0, blk_sz), hd:hd + LANE] = jnp.where(lane_blk == j, 1.0, 0.0).astype(BF16)
                kmean_ref[c, pl.ds(j, 1), :] = jnp.sum(kb, axis=0, keepdims=True) * (1.0 / blk_sz)
                vt_ref[c, 0:hd, pl.ds(r0, blk_sz)] = v_ref[pl.ds(r0, blk_sz), hd * c:hd * (c + 1)].T.astype(BF16)
            return carry

        lax.fori_loop(0, n_blocks, body, 0)

    nb_pad = kmean_ref.shape[1]
    for c in chains:
        q = q_ref[:, gw * c:gw * (c + 1)]
        qs = jnp.concatenate([q[:, hd * g:hd * (g + 1)] for g in range(MOBA_GROUP)], axis=0)
        gate = lax.dot_general(kmean_ref[c], qs, _NT, precision=lax.Precision.HIGHEST,
                               preferred_element_type=F32)
        bias = _topk_bias(gate, qi, nb_pad)
        qaug_ref[c, 0:hd, :] = (qs * (hd ** -0.5 * LOG2E)).T.astype(BF16)
        qaug_ref[c, hd:hd + nb_pad, :] = bias.astype(BF16)
        qaug_ref[c, hd + nb_pad:, :] = jnp.full((LANE - nb_pad, rows), MASK_BIAS, BF16)

    def scores(c, r0):
        return jnp.dot(kaug_ref[c, pl.ds(r0, blk_sz), :], qaug_ref[c],
                       preferred_element_type=F32)

    def accumulate(c, s, r0):
        m_prev = m_ref[c]
        m_new = jnp.maximum(m_prev, jnp.max(s, axis=0, keepdims=True))
        p = jnp.exp2(s - m_new).astype(BF16)
        pv = jnp.dot(vt_ref[c, :, pl.ds(r0, blk_sz)], p, preferred_element_type=F32)
        acc_ref[c] = jnp.exp2(m_prev - m_new) * acc_ref[c] + pv
        m_ref[c] = m_new

    for c in chains:
        sa_ref[c] = scores(c, 0)

    r_own = pl.multiple_of(qi * blk_sz, blk_sz)
    key_t = lax.broadcasted_iota(jnp.int32, (blk_sz, rows), 0)
    row_t = lax.broadcasted_iota(jnp.int32, (blk_sz, rows), 1) % blk_sz
    for c in chains:
        s = jnp.dot(kaug_ref[c, pl.ds(r_own, blk_sz), 0:hd], qaug_ref[c, 0:hd, :],
                    preferred_element_type=F32)
        s = jnp.where(key_t <= row_t, s, NEG_INF)
        m0 = jnp.max(s, axis=0, keepdims=True)
        m_ref[c] = m0
        acc_ref[c] = jnp.dot(vt_ref[c, :, pl.ds(r_own, blk_sz)], jnp.exp2(s - m0).astype(BF16),
                             preferred_element_type=F32)

    n_pairs = (qi + 1) // 2

    def past_pair(jj, carry):
        r0 = pl.multiple_of(jj * (2 * blk_sz), 2 * blk_sz)
        r1 = pl.multiple_of(r0 + blk_sz, blk_sz)
        r2 = pl.multiple_of(jnp.minimum(r0 + 2 * blk_sz, (n_blocks - 1) * blk_sz), blk_sz)
        for c in chains:
            sb_ref[c] = scores(c, r1)
        for c in chains:
            accumulate(c, sa_ref[c], r0)
        for c in chains:
            sa_ref[c] = scores(c, r2)
        for c in chains:
            accumulate(c, sb_ref[c], r1)
        return carry

    lax.fori_loop(0, n_pairs, past_pair, 0)

    for c in chains:
        acc = acc_ref[c]
        o = (acc[0:hd, :] * (1.0 / acc[hd:hd + 1, :])).T
        o = jnp.concatenate([o[blk_sz * g:blk_sz * (g + 1), :] for g in range(MOBA_GROUP)], axis=1)
        o_ref[:, gw * c:gw * (c + 1)] = (o * _silu(g_ref[:, gw * c:gw * (c + 1)])).astype(o_ref.dtype)


def _moba_prompt(proj, t):
    assert t % (2 * MOBA_BLOCK) == 0
    nb = t // MOBA_BLOCK
    nb_pad = -(-nb // BF16_SUBLANES) * BF16_SUBLANES
    assert nb_pad <= LANE
    rows = MOBA_GROUP * MOBA_BLOCK
    nc = _MOBA_KV_PER_STEP
    gw = MOBA_GROUP * MOBA_HD * nc
    kvw = MOBA_HD * nc
    assert MOBA_KV_HEADS % nc == 0
    assert OFF_MQ % gw == 0 and OFF_MG % gw == 0 and OFF_MK % kvw == 0 and OFF_MV % kvw == 0
    kv_spec = functools.partial(pl.BlockSpec, (t, kvw), pipeline_mode=pl.Buffered(1))
    return pl.pallas_call(
        functools.partial(_moba_prompt_kernel, n_blocks=nb),
        grid=(MOBA_KV_HEADS // nc, nb),
        in_specs=[pl.BlockSpec((MOBA_BLOCK, gw), lambda h, i: (i, OFF_MQ // gw + h)),
                  kv_spec(lambda h, i: (0, OFF_MK // kvw + h)),
                  kv_spec(lambda h, i: (0, OFF_MV // kvw + h)),
                  pl.BlockSpec((MOBA_BLOCK, gw), lambda h, i: (i, OFF_MG // gw + h))],
        out_specs=pl.BlockSpec((MOBA_BLOCK, gw), lambda h, i: (i, h)),
        out_shape=jax.ShapeDtypeStruct((t, MOBA_QW), BF16),
        scratch_shapes=[pltpu.VMEM((nc, t, MOBA_HD + LANE), BF16),
                        pltpu.VMEM((nc, MOBA_HD + BF16_SUBLANES, t), BF16),
                        pltpu.VMEM((nc, nb_pad, MOBA_HD), F32),
                        pltpu.VMEM((nc, MOBA_HD + LANE, rows), BF16),
                        pltpu.VMEM((nc, 1, rows), F32),
                        pltpu.VMEM((nc, MOBA_HD + BF16_SUBLANES, rows), F32),
                        pltpu.VMEM((nc, MOBA_BLOCK, rows), F32),
                        pltpu.VMEM((nc, MOBA_BLOCK, rows), F32)],
        compiler_params=_params("arbitrary", "arbitrary"),
        name="moba_prompt",
    )(proj, proj, proj, proj)


_GLA_GROUP = 4
_GLA_PAIRS_PER_STEP = 8
_GLA_PAIRS_PER_VCHUNK = 4


def _gla_prompt_kernel(q_ref, k_ref, b_ref, *rest, n_tblocks, tb):
    ti = pl.program_id(1)
    sub = GLA_SUB
    kw = 2 * GLA_DK
    vw = 2 * GLA_DV
    npp = _GLA_PAIRS_PER_STEP
    nvc = npp // _GLA_PAIRS_PER_VCHUNK
    v_refs, gg_refs = rest[:nvc], rest[nvc:2 * nvc]
    gain_ref, o_ref, sout_ref, s_ref, oacc_ref = rest[2 * nvc:]

    @pl.when(ti == 0)
    def _zero_state():
        s_ref[...] = jnp.zeros(s_ref.shape, F32)

    row_h = lax.broadcasted_iota(jnp.int32, (kw, vw), 0) // GLA_DK
    col_h = lax.broadcasted_iota(jnp.int32, (kw, vw), 1) // GLA_DV
    same_head = row_h == col_h
    head_sum = jnp.where(same_head, 1.0, 0.0).astype(BF16)
    j_idx = lax.broadcasted_iota(jnp.int32, (sub, kw), 0)
    grp = _GLA_GROUP
    lane_head = lax.broadcasted_iota(jnp.int32, (grp * sub, kw), 1) // GLA_DK
    gt = grp * sub
    half = sub // 2
    sub_rows = half * half + half * sub
    rs_c = lax.broadcasted_iota(jnp.int32, (gt, grp * sub_rows), 1)
    rs_in = rs_c % sub_rows
    rs_tok = (rs_c // sub_rows) * sub + jnp.where(rs_in < half * half, rs_in // half,
                                                  half + (rs_in - half * half) // sub)
    row_sum = jnp.where(lax.broadcasted_iota(jnp.int32, (gt, grp * sub_rows), 0) == rs_tok,
                        1.0, 0.0).astype(BF16)

    def pair_group(q, k, b, v, state):
        w_rows, v_rows = [], []
        for s in range(grp):
            lo = s * sub
            b_s, k_s, v_s = b[lo:lo + sub, :], k[lo:lo + sub, :], v[lo:lo + sub, :]
            for i in range(sub):
                nj = half if i < half else sub
                d = b_s[i:i + 1, :] - b_s[0:nj, :]
                dec = jnp.exp(jnp.where(j_idx[0:nj, :] < i, d, NEG_INF))
                w_rows.append(dec * k_s[0:nj, :] * q[lo + i:lo + i + 1, :])
                v_rows.append(v_s[0:nj, :])
        w = jnp.concatenate(w_rows, axis=0).astype(BF16)
        sc = jnp.dot(w, head_sum, preferred_element_type=F32)
        z = (sc * jnp.concatenate(v_rows, axis=0)).astype(BF16)
        o_diag = jnp.dot(row_sum, z, preferred_element_type=F32)
        qk = q * k
        o_self = jnp.concatenate(
            [jnp.sum(jnp.where(lane_head == h, qk, 0.0), axis=1, keepdims=True)
             * v[:, GLA_DV * h:GLA_DV * (h + 1)] for h in range(2)], axis=1)
        o_diag = o_diag + o_self
        q_dec = (q * jnp.exp(b)).astype(BF16)
        upds, decays = [], []
        for s in range(grp):
            lo = s * sub
            b_s, k_s, v_s = b[lo:lo + sub, :], k[lo:lo + sub, :], v[lo:lo + sub, :]
            b_last = b_s[sub - 1:sub, :]
            k_dec = (k_s * jnp.exp(b_last - b_s)).astype(BF16)
            upd = lax.dot_general(k_dec, v_s.astype(BF16), _TN, preferred_element_type=F32)
            upds.append(jnp.where(same_head, upd, 0.0))
            e_col = jnp.broadcast_to(jnp.exp(b_last), (kw, kw)).T
            decays.append(jnp.concatenate([e_col, e_col], axis=1))
        o_inter = []
        for s in range(grp):
            lo = s * sub
            o_inter.append(jnp.dot(q_dec[lo:lo + sub, :], state.astype(BF16),
                                   preferred_element_type=F32))
            state = state * decays[s] + upds[s]
        return jnp.concatenate(o_inter, axis=0) + o_diag, state

    def group(g, carry):
        r0 = pl.multiple_of(g * gt, gt)
        for pp in range(npp):
            ks, vs = slice(kw * pp, kw * (pp + 1)), slice(vw * pp, vw * (pp + 1))
            pc = pp % _GLA_PAIRS_PER_VCHUNK
            v_pair = v_refs[pp // _GLA_PAIRS_PER_VCHUNK][pl.ds(r0, gt), vw * pc:vw * (pc + 1)]
            o, state = pair_group(q_ref[pl.ds(r0, gt), ks] * (GLA_DK ** -0.5), k_ref[pl.ds(r0, gt), ks],
                                  b_ref[pl.ds(r0, gt), ks], v_pair, s_ref[pp])
            s_ref[pp] = state
            oacc_ref[pl.ds(r0, gt), vs] = o
        return carry

    lax.fori_loop(0, tb // gt, group, 0)

    o = oacc_ref[...]
    gain = gain_ref[...]
    heads = []
    for h in range(2 * npp):
        oh = o[:, GLA_DV * h:GLA_DV * (h + 1)]
        ms = jnp.mean(oh * oh, axis=-1, keepdims=True)
        heads.append(oh * lax.rsqrt(ms + NORM_EPS) * gain)
    o = jnp.concatenate(heads, axis=1)
    gate = jnp.concatenate([r[...] for r in gg_refs], axis=1)
    o_ref[...] = (o * _silu(gate)).astype(o_ref.dtype)

    @pl.when(ti == n_tblocks - 1)
    def _emit_state():
        for pp in range(npp):
            st = s_ref[pp]
            sout_ref[2 * pp] = st[0:GLA_DK, 0:GLA_DV]
            sout_ref[2 * pp + 1] = st[GLA_DK:2 * GLA_DK, GLA_DV:2 * GLA_DV]


def _gla_prompt(proj, decay, gain, t):
    tb = 256 if t % 256 == 0 else t
    assert t % tb == 0 and tb % (GLA_SUB * _GLA_GROUP) == 0
    nt = t // tb
    npp = _GLA_PAIRS_PER_STEP
    kw, vw = 2 * GLA_DK * npp, 2 * GLA_DV * npp
    vcw = 2 * GLA_DV * _GLA_PAIRS_PER_VCHUNK
    nvc = npp // _GLA_PAIRS_PER_VCHUNK
    assert GLA_PAIRS % npp == 0 and npp % _GLA_PAIRS_PER_VCHUNK == 0
    assert OFF_GQ % kw == 0 and OFF_GK % kw == 0 and OFF_GV % vcw == 0 and OFF_GG % vcw == 0

    def vchunk(off, c):
        return pl.BlockSpec((tb, vcw), lambda p, i: (i, off // vcw + p * nvc + c))

    return pl.pallas_call(
        functools.partial(_gla_prompt_kernel, n_tblocks=nt, tb=tb),
        grid=(GLA_PAIRS // npp, nt),
        in_specs=[pl.BlockSpec((tb, kw), lambda p, i: (i, OFF_GQ // kw + p)),
                  pl.BlockSpec((tb, kw), lambda p, i: (i, OFF_GK // kw + p)),
                  pl.BlockSpec((tb, kw), lambda p, i: (i, p))]
                 + [vchunk(OFF_GV, c) for c in range(nvc)]
                 + [vchunk(OFF_GG, c) for c in range(nvc)]
                 + [pl.BlockSpec((1, GLA_DV), lambda p, i: (0, 0))],
        out_specs=[pl.BlockSpec((tb, vw), lambda p, i: (i, p)),
                   pl.BlockSpec((2 * npp, GLA_DK, GLA_DV), lambda p, i: (p, 0, 0))],
        out_shape=[jax.ShapeDtypeStruct((t, GLA_VW), BF16),
                   jax.ShapeDtypeStruct((GLA_HEADS, GLA_DK, GLA_DV), F32)],
        scratch_shapes=[pltpu.VMEM((npp, 2 * GLA_DK, 2 * GLA_DV), F32),
                        pltpu.VMEM((tb, vw), F32)],
        compiler_params=_params("arbitrary", "arbitrary"),
        name="gla_prompt",
    )(proj, proj, decay, *([proj] * (2 * nvc)), gain)


def _out_proj_kernel(og_ref, om_ref, wg_ref, wm_ref, x_ref, lg_ref, lb_ref, o_ref,
                     *, n_ctiles, tn, alpha):
    j = pl.program_id(1)
    mixed = jnp.dot(og_ref[...], wg_ref[...], preferred_element_type=F32)
    mixed = mixed + jnp.dot(om_ref[...], wm_ref[...], preferred_element_type=F32)
    c0 = pl.multiple_of(j * tn, tn)
    o_ref[:, pl.ds(c0, tn)] = alpha * x_ref[...] + mixed

    @pl.when(j == n_ctiles - 1)
    def _layer_norm():
        h = o_ref[...]
        mu = jnp.mean(h, axis=-1, keepdims=True)
        hc = h - mu
        var = jnp.mean(hc * hc, axis=-1, keepdims=True)
        o_ref[...] = hc * lax.rsqrt(var + NORM_EPS) * lg_ref[...] + lb_ref[...]


def _out_proj(og, om, w_out_bf, x, ln_g, ln_b, alpha, tm, tn):
    m, d = x.shape
    kg, km = og.shape[1], om.shape[1]
    assert kg == km and w_out_bf.shape[0] == kg + km
    nct = d // tn
    return pl.pallas_call(
        functools.partial(_out_proj_kernel, n_ctiles=nct, tn=tn, alpha=alpha),
        grid=(m // tm, nct),
        in_specs=[pl.BlockSpec((tm, kg), lambda i, j: (i, 0)),
                  pl.BlockSpec((tm, km), lambda i, j: (i, 0)),
                  pl.BlockSpec((kg, tn), lambda i, j: (0, j)),
                  pl.BlockSpec((km, tn), lambda i, j: (1, j)),
                  pl.BlockSpec((tm, tn), lambda i, j: (i, j)),
                  pl.BlockSpec((1, d), lambda i, j: (0, 0)),
                  pl.BlockSpec((1, d), lambda i, j: (0, 0))],
        out_specs=pl.BlockSpec((tm, d), lambda i, j: (i, 0)),
        out_shape=jax.ShapeDtypeStruct((m, d), F32),
        compiler_params=_params("arbitrary", "arbitrary"),
        name="out_proj_ln",
    )(og, om, w_out_bf, w_out_bf, x, ln_g, ln_b)


_MOBA_DEC_SEQS = 2


def _moba_decode_kernel(pt_ref, q_ref, kn_ref, vn_ref, g_ref, *rest, n_pages, page):
    del pt_ref
    n_seq = _MOBA_DEC_SEQS
    o_ref = rest[2 * n_seq * n_pages]
    k_refs = [rest[sq * n_pages:(sq + 1) * n_pages] for sq in range(n_seq)]
    v_refs = [rest[(n_seq + sq) * n_pages:(n_seq + sq + 1) * n_pages] for sq in range(n_seq)]
    qs = [q_ref[sq] * (MOBA_HD ** -0.5) for sq in range(n_seq)]
    ss = [_moba_decode_scores(qs[sq], k_refs[sq], n_pages) for sq in range(n_seq)]
    ps = [_moba_decode_select(ss[sq], qs[sq], kn_ref[sq], n_pages, page) for sq in range(n_seq)]
    for sq in range(n_seq):
        pb, p_new, denom = ps[sq]
        o = _moba_decode_values(pb, p_new, denom, vn_ref[sq], v_refs[sq], n_pages, page)
        o_ref[sq] = (o * _silu(g_ref[sq])).astype(o_ref.dtype)


def _moba_decode_scores(q, k_refs, n_pages):
    qb = q.astype(BF16)
    return jnp.concatenate(
        [lax.dot_general(qb, k_refs[p][0, 0].astype(BF16), _NT, preferred_element_type=F32)
         for p in range(n_pages)], axis=1)


def _expand_kv_rows(x):
    return jnp.concatenate([jnp.broadcast_to(x[i:i + 1, :], (MOBA_GROUP, MOBA_HD))
                            for i in range(MOBA_KV_HEADS)], axis=0)


def _moba_decode_select(s, q, kn, n_pages, page):
    heads = MOBA_HEADS
    brow = MOBA_BLOCK * MOBA_KV_HEADS
    n_blk = (n_pages * page) // MOBA_BLOCK
    lane = lax.broadcasted_iota(jnp.int32, s.shape, 1)
    head_kv = lax.broadcasted_iota(jnp.int32, s.shape, 0) // MOBA_GROUP
    own_kv = (lane % MOBA_KV_HEADS) == head_kv
    s_own = jnp.where(own_kv, s, 0.0)

    assert n_blk <= LANE
    blk_lane = lax.broadcasted_iota(jnp.int32, (heads, LANE), 1)
    cur = jnp.full((heads, LANE), NEG_INF, F32)
    for j in range(n_blk):
        mean_j = jnp.sum(s_own[:, brow * j:brow * (j + 1)], axis=1, keepdims=True) * (1.0 / MOBA_BLOCK)
        cur = jnp.where(blk_lane == j, mean_j, cur)
    sel = jnp.zeros((heads, LANE), F32)
    for _ in range(min(MOBA_TOPK, n_blk)):
        mx = jnp.max(cur, axis=1, keepdims=True)
        first = jnp.min(jnp.where(cur == mx, blk_lane, n_blk), axis=1, keepdims=True)
        pick = blk_lane == first
        sel = jnp.where(pick, 1.0, sel)
        cur = jnp.where(pick, NEG_INF, cur)
    bias = jnp.concatenate(
        [jnp.broadcast_to(jnp.where(jnp.sum(jnp.where(blk_lane == j, sel, 0.0), axis=1, keepdims=True) > 0.5,
                                    0.0, NEG_INF), (heads, brow)) for j in range(n_blk)], axis=1)

    s_new = jnp.sum(q * _expand_kv_rows(kn), axis=1, keepdims=True)
    sm = jnp.where(own_kv, s + bias, NEG_INF)
    m = jnp.maximum(jnp.max(sm, axis=1, keepdims=True), s_new)
    p = jnp.exp(sm - m)
    p_new = jnp.exp(s_new - m)
    denom = jnp.sum(p, axis=1, keepdims=True) + p_new
    return p.astype(BF16), p_new, denom


def _moba_decode_values(pb, p_new, denom, vn, v_refs, n_pages, page):
    prow = page * MOBA_KV_HEADS
    o = p_new * _expand_kv_rows(vn)
    for pg in range(n_pages):
        o = o + jnp.dot(pb[:, prow * pg:prow * (pg + 1)], v_refs[pg][0, 0].astype(BF16),
                        preferred_element_type=F32)
    return o * (1.0 / denom)


def _moba_decode(page_table, q3, kn3, vn3, g3, cache_k, cache_v, layer):
    db, n_pages = page_table.shape
    depth, n_pool, page = cache_k.shape[0], cache_k.shape[1], cache_k.shape[2]
    assert (n_pages * page) % MOBA_BLOCK == 0
    prow = page * MOBA_KV_HEADS
    ck = cache_k.reshape(depth, n_pool, prow, MOBA_HD)
    cv = cache_v.reshape(depth, n_pool, prow, MOBA_HD)

    n_seq = _MOBA_DEC_SEQS
    assert db % n_seq == 0

    def page_spec(sq, pg):
        return pl.BlockSpec((1, 1, prow, MOBA_HD), lambda b, pt: (layer, pt[b * n_seq + sq, pg], 0, 0))

    def row_spec(r):
        return pl.BlockSpec((n_seq, r, MOBA_HD), lambda b, pt: (b, 0, 0))

    pages = [page_spec(sq, pg) for sq in range(n_seq) for pg in range(n_pages)]
    grid_spec = pltpu.PrefetchScalarGridSpec(
        num_scalar_prefetch=1,
        grid=(db // n_seq,),
        in_specs=[row_spec(MOBA_HEADS), row_spec(MOBA_KV_HEADS), row_spec(MOBA_KV_HEADS),
                  row_spec(MOBA_HEADS)] + pages + pages,
        out_specs=pl.BlockSpec((n_seq, MOBA_HEADS, MOBA_HD), lambda b, pt: (b, 0, 0)),
    )
    n_ops = n_seq * n_pages
    return pl.pallas_call(
        functools.partial(_moba_decode_kernel, n_pages=n_pages, page=page),
        grid_spec=grid_spec,
        out_shape=jax.ShapeDtypeStruct((db, MOBA_HEADS, MOBA_HD), BF16),
        compiler_params=_params("arbitrary"),
        name="moba_decode",
    )(page_table, q3, kn3, vn3, g3, *([ck] * n_ops), *([cv] * n_ops))


_GLA_DEC_GROUP = 8


def _gla_decode_kernel(q_ref, k_ref, la_ref, v_ref, gg_ref, gain_ref, s_ref, o_ref, sn_ref):
    grp = _GLA_DEC_GROUP

    def columns(x):
        xp = jnp.concatenate([x, jnp.zeros((LANE - grp, x.shape[1]), F32)], axis=0)
        return xp.T

    a_t = columns(jnp.exp(la_ref[...]))
    k_t = columns(k_ref[...])
    q_t = columns(q_ref[...] * (GLA_DK ** -0.5))
    gain = gain_ref[...]
    for i in range(grp):
        rows = []
        for h in range(GLA_HEADS):
            r = slice(GLA_DK * h, GLA_DK * (h + 1))
            v_h = jnp.broadcast_to(v_ref[i, h:h + 1, :], (GLA_DK, GLA_DV))
            s_new = a_t[r, i:i + 1] * s_ref[i, h] + k_t[r, i:i + 1] * v_h
            sn_ref[i, h] = s_new
            rows.append(jnp.sum(q_t[r, i:i + 1] * s_new, axis=0, keepdims=True))
        o = jnp.concatenate(rows, axis=0)
        ms = jnp.mean(o * o, axis=-1, keepdims=True)
        o = o * lax.rsqrt(ms + NORM_EPS) * gain
        o_ref[i] = (o * _silu(gg_ref[i])).astype(o_ref.dtype)


def _gla_decode(q2, k2, la2, v3, gg3, gain, state):
    db = q2.shape[0]
    grp = _GLA_DEC_GROUP
    assert db % grp == 0
    vec = pl.BlockSpec((grp, GLA_KW), lambda g: (g, 0))
    hd3 = pl.BlockSpec((grp, GLA_HEADS, GLA_DV), lambda g: (g, 0, 0))
    st = pl.BlockSpec((grp, GLA_HEADS, GLA_DK, GLA_DV), lambda g: (g, 0, 0, 0))
    return pl.pallas_call(
        _gla_decode_kernel,
        grid=(db // grp,),
        in_specs=[vec, vec, vec, hd3, hd3, pl.BlockSpec((1, GLA_DV), lambda g: (0, 0)), st],
        out_specs=[hd3, st],
        out_shape=[jax.ShapeDtypeStruct((db, GLA_HEADS, GLA_DV), BF16),
                   jax.ShapeDtypeStruct(state.shape, F32)],
        compiler_params=_params("arbitrary"),
        name="gla_decode",
    )(q2, k2, la2, v3, gg3, gain, state)


def _tile(n, prefs):
    for p in prefs:
        if n % p == 0:
            return p
    return n


def kernel(x_prompt, x_sample, cache_k, cache_v, state_gla, page_table,
           w_in, w_a2, b_a, gla_gain, w_out, ln_g, ln_b):
    bsz, t, d = x_prompt.shape
    db, dt, _ = x_sample.shape
    depth = w_in.shape[0]
    assert bsz == 1 and dt == 1
    alpha = (2.0 * depth) ** 0.25
    y_p = x_prompt.reshape(t, d)
    y_s = x_sample.reshape(db, d)
    kp_l, vp_l, sp_l, ks_l, vs_l, ss_l = [], [], [], [], [], []
    for l in range(depth):
        w_in_t = jnp.swapaxes(w_in[l], 0, 1)
        w_lr_t = jnp.zeros((LANE, d), BF16).at[:GLA_LOWRANK, :].set(w_in_t[OFF_GA:].astype(BF16))
        w_a2p = jnp.zeros((LANE, GLA_KW), F32).at[:GLA_LOWRANK, :].set(w_a2[l])
        w_out_bf = w_out[l].astype(BF16)
        gain = gla_gain[l].reshape(1, GLA_DV)
        lg, lb = ln_g[l].reshape(1, d), ln_b[l].reshape(1, d)
        x_all, ga = _stack_cast(y_p, y_s, w_lr_t, _tile(math.gcd(t, db), (128,)))
        m_all = t + db
        proj = _in_proj(x_all, w_in_t, MAIN_W, _tile(m_all, (520, 512, 256, 128)), 1024)
        ba2 = b_a[l].reshape(1, GLA_KW)
        decay = _log_decay(ga, w_a2p, ba2, 0, t, _tile(math.gcd(t, 512), (512,)), True)
        log_a_s = _log_decay(ga, w_a2p, ba2, t, db, db, False)
        o_m = _moba_prompt(proj, t)
        o_g, s_p = _gla_prompt(proj, decay, gain, t)
        k_rows, v_rows = _kv_rows(proj, t, _tile(t, (512, 256, 128)))
        kp_l.append(k_rows.reshape(bsz, t, MOBA_KV_HEADS, MOBA_HD))
        vp_l.append(v_rows.reshape(bsz, t, MOBA_KV_HEADS, MOBA_HD))
        sp_l.append(s_p.reshape(bsz, GLA_HEADS, GLA_DK, GLA_DV))
        y_p_new = _out_proj(o_g, o_m, w_out_bf, y_p, lg, lb, alpha,
                            _tile(t, (512, 256, 128)), _tile(d, (1024, 512, 256, 128)))
        proj_s = proj[t:]
        k_new = proj_s[:, OFF_MK:OFF_MV].reshape(db, MOBA_KV_HEADS, MOBA_HD)
        v_new = proj_s[:, OFF_MV:OFF_MG].reshape(db, MOBA_KV_HEADS, MOBA_HD)
        o_m_s = _moba_decode(page_table,
                             proj_s[:, OFF_MQ:OFF_MK].reshape(db, MOBA_HEADS, MOBA_HD),
                             k_new, v_new,
                             proj_s[:, OFF_MG:OFF_GQ].reshape(db, MOBA_HEADS, MOBA_HD),
                             cache_k, cache_v, l)
        o_g_s, s_s = _gla_decode(proj_s[:, OFF_GQ:OFF_GK], proj_s[:, OFF_GK:OFF_GV], log_a_s,
                                 proj_s[:, OFF_GV:OFF_GG].reshape(db, GLA_HEADS, GLA_DV),
                                 proj_s[:, OFF_GG:OFF_GA].reshape(db, GLA_HEADS, GLA_DV),
                                 gain, state_gla[l])
        ks_l.append(k_new.reshape(db, dt, MOBA_KV_HEADS, MOBA_HD))
        vs_l.append(v_new.reshape(db, dt, MOBA_KV_HEADS, MOBA_HD))
        ss_l.append(s_s)
        y_s = _out_proj(o_g_s.reshape(db, GLA_VW), o_m_s.reshape(db, MOBA_QW), w_out_bf, y_s, lg, lb,
                        alpha, _tile(db, (128,)), _tile(d, (1024, 512, 256, 128)))
        y_p = y_p_new
    return (y_p.reshape(bsz, t, d), y_s.reshape(db, dt, d),
            jnp.stack(kp_l), jnp.stack(vp_l), jnp.stack(sp_l),
            jnp.stack(ks_l), jnp.stack(vs_l), jnp.stack(ss_l))
```

```python
import functools
import math

import jax
import jax.numpy as jnp
from jax import lax
from jax.experimental import pallas as pl
from jax.experimental.pallas import tpu as pltpu

F32 = jnp.float32
BF16 = jnp.bfloat16
NEG_INF = float("-inf")
MASK_BIAS = -1e30
LOG2E = 1.4426950408889634

MOBA_HEADS = 16
MOBA_KV_HEADS = 4
MOBA_GROUP = MOBA_HEADS // MOBA_KV_HEADS
MOBA_HD = 128
MOBA_BLOCK = 256
MOBA_TOPK = 3
GLA_HEADS = 16
GLA_DK = 64
GLA_DV = 128
GLA_LOWRANK = 16
GLA_GATE_TAU = 16.0
GLA_SUB = 16
GLA_PAIRS = GLA_HEADS // 2
NORM_EPS = 1e-5

MOBA_QW = MOBA_HEADS * MOBA_HD
MOBA_KVW = MOBA_KV_HEADS * MOBA_HD
GLA_KW = GLA_HEADS * GLA_DK
GLA_VW = GLA_HEADS * GLA_DV
OFF_MQ = 0
OFF_MK = OFF_MQ + MOBA_QW
OFF_MV = OFF_MK + MOBA_KVW
OFF_MG = OFF_MV + MOBA_KVW
OFF_GQ = OFF_MG + MOBA_QW
OFF_GK = OFF_GQ + GLA_KW
OFF_GV = OFF_GK + GLA_KW
OFF_GG = OFF_GV + GLA_VW
OFF_GA = OFF_GG + GLA_VW
MAIN_W = OFF_GA

LANE = 128
BF16_SUBLANES = 16
VMEM_LIMIT_BYTES = 60 * 1024 * 1024

_NT = (((1,), (1,)), ((), ()))
_TN = (((0,), (0,)), ((), ()))


def _params(*sem):
    return pltpu.CompilerParams(dimension_semantics=sem, vmem_limit_bytes=VMEM_LIMIT_BYTES)


def _silu(x):
    return x * (1.0 / (1.0 + jnp.exp(-x)))


def _split_bf16(x):
    hi = x.astype(BF16)
    return hi, (x - hi.astype(F32)).astype(BF16)


def _stack_cast_kernel(xp_ref, xs_ref, wlr_ref, o_ref, ga_ref, *, n_prompt_tiles):
    i = pl.program_id(0)

    def emit(x):
        xb = x.astype(o_ref.dtype)
        o_ref[...] = xb
        ga_ref[...] = jnp.dot(xb, wlr_ref[...], preferred_element_type=F32)

    @pl.when(i < n_prompt_tiles)
    def _prompt_rows():
        emit(xp_ref[...])

    @pl.when(i >= n_prompt_tiles)
    def _decode_rows():
        emit(xs_ref[...])


def _stack_cast(xp, xs, w_lr, tr):
    (t, d), db = xp.shape, xs.shape[0]
    assert t % tr == 0 and db % tr == 0
    npt, nst = t // tr, db // tr
    return pl.pallas_call(
        functools.partial(_stack_cast_kernel, n_prompt_tiles=npt),
        grid=(npt + nst,),
        in_specs=[pl.BlockSpec((tr, d), lambda i: (jnp.minimum(i, npt - 1), 0)),
                  pl.BlockSpec((tr, d), lambda i: (jnp.maximum(i - npt, 0), 0)),
                  pl.BlockSpec((d, LANE), lambda i: (0, 0))],
        out_specs=[pl.BlockSpec((tr, d), lambda i: (i, 0)),
                   pl.BlockSpec((tr, LANE), lambda i: (i, 0))],
        out_shape=[jax.ShapeDtypeStruct((t + db, d), BF16),
                   jax.ShapeDtypeStruct((t + db, LANE), F32)],
        compiler_params=_params("arbitrary"),
        name="stack_cast",
    )(xp, xs, w_lr)


def _in_proj_kernel(x_ref, wt_ref, o_ref, wbf_ref):
    @pl.when(pl.program_id(1) == 0)
    def _cast_weight_tile():
        wbf_ref[...] = wt_ref[...].astype(BF16)

    o_ref[...] = lax.dot_general(x_ref[...], wbf_ref[...], _NT, preferred_element_type=F32)


def _in_proj(x, wt, n_out, tm, tn):
    m, k = x.shape
    assert n_out % tn == 0 and m % tm == 0 and wt.shape[0] >= n_out
    return pl.pallas_call(
        _in_proj_kernel,
        grid=(n_out // tn, m // tm),
        in_specs=[pl.BlockSpec((tm, k), lambda j, i: (i, 0)),
                  pl.BlockSpec((tn, k), lambda j, i: (j, 0))],
        out_specs=pl.BlockSpec((tm, tn), lambda j, i: (i, j)),
        out_shape=jax.ShapeDtypeStruct((m, n_out), F32),
        scratch_shapes=[pltpu.VMEM((tn, k), BF16)],
        compiler_params=_params("arbitrary", "arbitrary"),
        name="in_proj",
    )(x, wt)


def _kv_rows_kernel(k_ref, v_ref, ko_ref, vo_ref):
    tok = k_ref.shape[0]
    for h in range(MOBA_KV_HEADS):
        cols = slice(MOBA_HD * h, MOBA_HD * (h + 1))
        ko_ref[pl.ds(h, tok, stride=MOBA_KV_HEADS), :] = k_ref[:, cols]
        vo_ref[pl.ds(h, tok, stride=MOBA_KV_HEADS), :] = v_ref[:, cols]


def _kv_rows(proj, t, tr):
    assert t % tr == 0
    out = jax.ShapeDtypeStruct((t * MOBA_KV_HEADS, MOBA_HD), F32)
    return pl.pallas_call(
        _kv_rows_kernel,
        grid=(t // tr,),
        in_specs=[pl.BlockSpec((tr, MOBA_KVW), lambda i: (i, OFF_MK // MOBA_KVW)),
                  pl.BlockSpec((tr, MOBA_KVW), lambda i: (i, OFF_MV // MOBA_KVW))],
        out_specs=[pl.BlockSpec((tr * MOBA_KV_HEADS, MOBA_HD), lambda i: (i, 0)),
                   pl.BlockSpec((tr * MOBA_KV_HEADS, MOBA_HD), lambda i: (i, 0))],
        out_shape=[out, out],
        compiler_params=_params("arbitrary"),
        name="kv_rows",
    )(proj, proj)


def _log_decay_kernel(ga_ref, wa2_ref, ba_ref, o_ref, *, cumulative):
    ga_hi, ga_lo = _split_bf16(ga_ref[...])
    wa_hi, wa_lo = _split_bf16(wa2_ref[...])
    z = (jnp.dot(ga_hi, wa_hi, preferred_element_type=F32)
         + (jnp.dot(ga_lo, wa_hi, preferred_element_type=F32)
            + jnp.dot(ga_hi, wa_lo, preferred_element_type=F32))) + ba_ref[...]
    log_a = (jnp.minimum(z, 0.0) - jnp.log(1.0 + jnp.exp(-jnp.abs(z)))) * (1.0 / GLA_GATE_TAU)
    if cumulative:
        pos = lax.broadcasted_iota(jnp.int32, log_a.shape, 0) % GLA_SUB
        step = 1
        while step < GLA_SUB:
            log_a = log_a + jnp.where(pos >= step, pltpu.roll(log_a, step, axis=0), 0.0)
            step *= 2
    o_ref[...] = log_a


def _log_decay(ga, w_a2p, b_a, row0, n_rows, tm, cumulative):
    assert row0 % tm == 0 and n_rows % tm == 0 and tm % GLA_SUB == 0
    t0 = row0 // tm
    return pl.pallas_call(
        functools.partial(_log_decay_kernel, cumulative=cumulative),
        grid=(n_rows // tm,),
        in_specs=[pl.BlockSpec((tm, LANE), lambda i: (t0 + i, 0)),
                  pl.BlockSpec((LANE, GLA_KW), lambda i: (0, 0)),
                  pl.BlockSpec((1, GLA_KW), lambda i: (0, 0))],
        out_specs=pl.BlockSpec((tm, GLA_KW), lambda i: (i, 0)),
        out_shape=jax.ShapeDtypeStruct((n_rows, GLA_KW), F32),
        compiler_params=_params("arbitrary"),
        name="gla_log_decay",
    )(ga, w_a2p, b_a)


def _topk_bias(gate, n_valid, n_rows):
    blk = lax.broadcasted_iota(jnp.int32, gate.shape, 0)
    cur = jnp.where(blk < n_valid, gate, NEG_INF)
    sel = jnp.zeros(gate.shape, jnp.bool_)
    for _ in range(MOBA_TOPK):
        mx = jnp.max(cur, axis=0, keepdims=True)
        hit = jnp.logical_and(cur == mx, mx > NEG_INF)
        first = jnp.min(jnp.where(hit, blk, n_rows), axis=0, keepdims=True)
        pick = blk == first
        sel = jnp.logical_or(sel, pick)
        cur = jnp.where(pick, NEG_INF, cur)
    return jnp.where(sel, 0.0, MASK_BIAS)


_MOBA_KV_PER_STEP = 2


def _moba_prompt_kernel(q_ref, k_ref, v_ref, g_ref, o_ref,
                        kaug_ref, vt_ref, kmean_ref, qaug_ref, m_ref, acc_ref, sa_ref, sb_ref,
                        *, n_blocks):
    qi = pl.program_id(1)
    blk_sz = MOBA_BLOCK
    hd = MOBA_HD
    rows = MOBA_GROUP * blk_sz
    gw = MOBA_GROUP * hd
    chains = range(_MOBA_KV_PER_STEP)

    @pl.when(qi == 0)
    def _prepare_kv_heads():
        kmean_ref[...] = jnp.zeros(kmean_ref.shape, F32)
        vt_ref[:, hd:, :] = jnp.ones((vt_ref.shape[0], vt_ref.shape[1] - hd, vt_ref.shape[2]), BF16)
        lane_blk = lax.broadcasted_iota(jnp.int32, (blk_sz, LANE), 1)

        def body(j, carry):
            r0 = pl.multiple_of(j * blk_sz, blk_sz)
            for c in chains:
                kb = k_ref[pl.ds(r0, blk_sz), hd * c:hd * (c + 1)]
                kaug_ref[c, pl.ds(r0, blk_sz), 0:hd] = kb.astype(BF16)
                kaug_ref[c, pl.ds(r0, blk_sz), hd:hd + LANE] = jnp.where(lane_blk == j, 1.0, 0.0).astype(BF16)
                kmean_ref[c, pl.ds(j, 1), :] = jnp.sum(kb, axis=0, keepdims=True) * (1.0 / blk_sz)
                vt_ref[c, 0:hd, pl.ds(r0, blk_sz)] = v_ref[pl.ds(r0, blk_sz), hd * c:hd * (c + 1)].T.astype(BF16)
            return carry

        lax.fori_loop(0, n_blocks, body, 0)

    nb_pad = kmean_ref.shape[1]
    for c in chains:
        q = q_ref[:, gw * c:gw * (c + 1)]
        qs = jnp.concatenate([q[:, hd * g:hd * (g + 1)] for g in range(MOBA_GROUP)], axis=0)
        gate = lax.dot_general(kmean_ref[c], qs, _NT, precision=lax.Precision.HIGHEST,
                               preferred_element_type=F32)
        bias = _topk_bias(gate, qi, nb_pad)
        qaug_ref[c, 0:hd, :] = (qs * (hd ** -0.5 * LOG2E)).T.astype(BF16)
        qaug_ref[c, hd:hd + nb_pad, :] = bias.astype(BF16)
        qaug_ref[c, hd + nb_pad:, :] = jnp.full((LANE - nb_pad, rows), MASK_BIAS, BF16)

    def scores(c, r0):
        return jnp.dot(kaug_ref[c, pl.ds(r0, blk_sz), :], qaug_ref[c],
                       preferred_element_type=F32)

    def accumulate(c, s, r0):
        m_prev = m_ref[c]
        m_new = jnp.maximum(m_prev, jnp.max(s, axis=0, keepdims=True))
        p = jnp.exp2(s - m_new).astype(BF16)
        pv = jnp.dot(vt_ref[c, :, pl.ds(r0, blk_sz)], p, preferred_element_type=F32)
        acc_ref[c] = jnp.exp2(m_prev - m_new) * acc_ref[c] + pv
        m_ref[c] = m_new

    for c in chains:
        sa_ref[c] = scores(c, 0)

    r_own = pl.multiple_of(qi * blk_sz, blk_sz)
    key_t = lax.broadcasted_iota(jnp.int32, (blk_sz, rows), 0)
    row_t = lax.broadcasted_iota(jnp.int32, (blk_sz, rows), 1) % blk_sz
    for c in chains:
        s = jnp.dot(kaug_ref[c, pl.ds(r_own, blk_sz), 0:hd], qaug_ref[c, 0:hd, :],
                    preferred_element_type=F32)
        s = jnp.where(key_t <= row_t, s, NEG_INF)
        m0 = jnp.max(s, axis=0, keepdims=True)
        m_ref[c] = m0
        acc_ref[c] = jnp.dot(vt_ref[c, :, pl.ds(r_own, blk_sz)], jnp.exp2(s - m0).astype(BF16),
                             preferred_element_type=F32)

    n_pairs = (qi + 1) // 2

    def past_pair(jj, carry):
        r0 = pl.multiple_of(jj * (2 * blk_sz), 2 * blk_sz)
        r1 = pl.multiple_of(r0 + blk_sz, blk_sz)
        r2 = pl.multiple_of(jnp.minimum(r0 + 2 * blk_sz, (n_blocks - 1) * blk_sz), blk_sz)
        for c in chains:
            sb_ref[c] = scores(c, r1)
        for c in chains:
            accumulate(c, sa_ref[c], r0)
        for c in chains:
            sa_ref[c] = scores(c, r2)
        for c in chains:
            accumulate(c, sb_ref[c], r1)
        return carry

    lax.fori_loop(0, n_pairs, past_pair, 0)

    for c in chains:
        acc = acc_ref[c]
        o = (acc[0:hd, :] * (1.0 / acc[hd:hd + 1, :])).T
        o = jnp.concatenate([o[blk_sz * g:blk_sz * (g + 1), :] for g in range(MOBA_GROUP)], axis=1)
        o_ref[:, gw * c:gw * (c + 1)] = (o * _silu(g_ref[:, gw * c:gw * (c + 1)])).astype(o_ref.dtype)


def _moba_prompt(proj, t):
    assert t % (2 * MOBA_BLOCK) == 0
    nb = t // MOBA_BLOCK
    nb_pad = -(-nb // BF16_SUBLANES) * BF16_SUBLANES
    assert nb_pad <= LANE
    rows = MOBA_GROUP * MOBA_BLOCK
    nc = _MOBA_KV_PER_STEP
    gw = MOBA_GROUP * MOBA_HD * nc
    kvw = MOBA_HD * nc
    assert MOBA_KV_HEADS % nc == 0
    assert OFF_MQ % gw == 0 and OFF_MG % gw == 0 and OFF_MK % kvw == 0 and OFF_MV % kvw == 0
    kv_spec = functools.partial(pl.BlockSpec, (t, kvw), pipeline_mode=pl.Buffered(1))
    return pl.pallas_call(
        functools.partial(_moba_prompt_kernel, n_blocks=nb),
        grid=(MOBA_KV_HEADS // nc, nb),
        in_specs=[pl.BlockSpec((MOBA_BLOCK, gw), lambda h, i: (i, OFF_MQ // gw + h)),
                  kv_spec(lambda h, i: (0, OFF_MK // kvw + h)),
                  kv_spec(lambda h, i: (0, OFF_MV // kvw + h)),
                  pl.BlockSpec((MOBA_BLOCK, gw), lambda h, i: (i, OFF_MG // gw + h))],
        out_specs=pl.BlockSpec((MOBA_BLOCK, gw), lambda h, i: (i, h)),
        out_shape=jax.ShapeDtypeStruct((t, MOBA_QW), BF16),
        scratch_shapes=[pltpu.VMEM((nc, t, MOBA_HD + LANE), BF16),
                        pltpu.VMEM((nc, MOBA_HD + BF16_SUBLANES, t), BF16),
                        pltpu.VMEM((nc, nb_pad, MOBA_HD), F32),
                        pltpu.VMEM((nc, MOBA_HD + LANE, rows), BF16),
                        pltpu.VMEM((nc, 1, rows), F32),
                        pltpu.VMEM((nc, MOBA_HD + BF16_SUBLANES, rows), F32),
                        pltpu.VMEM((nc, MOBA_BLOCK, rows), F32),
                        pltpu.VMEM((nc, MOBA_BLOCK, rows), F32)],
        compiler_params=_params("arbitrary", "arbitrary"),
        name="moba_prompt",
    )(proj, proj, proj, proj)


_GLA_GROUP = 8
_GLA_PAIRS_PER_STEP = 8
_GLA_PAIRS_PER_VCHUNK = 4


def _gla_prompt_kernel(q_ref, k_ref, b_ref, *rest, n_tblocks, tb):
    ti = pl.program_id(1)
    sub = GLA_SUB
    kw = 2 * GLA_DK
    vw = 2 * GLA_DV
    npp = _GLA_PAIRS_PER_STEP
    nvc = npp // _GLA_PAIRS_PER_VCHUNK
    v_refs, gg_refs = rest[:nvc], rest[nvc:2 * nvc]
    gain_ref, o_ref, sout_ref, s_ref, oacc_ref = rest[2 * nvc:]

    @pl.when(ti == 0)
    def _zero_state():
        s_ref[...] = jnp.zeros(s_ref.shape, F32)

    row_h = lax.broadcasted_iota(jnp.int32, (kw, vw), 0) // GLA_DK
    col_h = lax.broadcasted_iota(jnp.int32, (kw, vw), 1) // GLA_DV
    same_head = row_h == col_h
    head_sum = jnp.where(same_head, 1.0, 0.0).astype(BF16)
    j_idx = lax.broadcasted_iota(jnp.int32, (sub, kw), 0)
    grp = _GLA_GROUP
    lane_head = lax.broadcasted_iota(jnp.int32, (grp * sub, kw), 1) // GLA_DK
    gt = grp * sub
    half = sub // 2
    sub_rows = half * half + half * sub
    rs_c = lax.broadcasted_iota(jnp.int32, (gt, grp * sub_rows), 1)
    rs_in = rs_c % sub_rows
    rs_tok = (rs_c // sub_rows) * sub + jnp.where(rs_in < half * half, rs_in // half,
                                                  half + (rs_in - half * half) // sub)
    row_sum = jnp.where(lax.broadcasted_iota(jnp.int32, (gt, grp * sub_rows), 0) == rs_tok,
                        1.0, 0.0).astype(BF16)

    def pair_group(q, k, b, v, state):
        w_rows, v_rows = [], []
        for s in range(grp):
            lo = s * sub
            b_s, k_s, v_s = b[lo:lo + sub, :], k[lo:lo + sub, :], v[lo:lo + sub, :]
            for i in range(sub):
                nj = half if i < half else sub
                d = b_s[i:i + 1, :] - b_s[0:nj, :]
                dec = jnp.exp(jnp.where(j_idx[0:nj, :] < i, d, NEG_INF))
                w_rows.append(dec * k_s[0:nj, :] * q[lo + i:lo + i + 1, :])
                v_rows.append(v_s[0:nj, :])
        w = jnp.concatenate(w_rows, axis=0).astype(BF16)
        sc = jnp.dot(w, head_sum, preferred_element_type=F32)
        z = (sc * jnp.concatenate(v_rows, axis=0)).astype(BF16)
        o_diag = jnp.dot(row_sum, z, preferred_element_type=F32)
        qk = q * k
        o_self = jnp.concatenate(
            [jnp.sum(jnp.where(lane_head == h, qk, 0.0), axis=1, keepdims=True)
             * v[:, GLA_DV * h:GLA_DV * (h + 1)] for h in range(2)], axis=1)
        o_diag = o_diag + o_self
        q_dec = (q * jnp.exp(b)).astype(BF16)
        upds, decays = [], []
        for s in range(grp):
            lo = s * sub
            b_s, k_s, v_s = b[lo:lo + sub, :], k[lo:lo + sub, :], v[lo:lo + sub, :]
            b_last = b_s[sub - 1:sub, :]
            k_dec = (k_s * jnp.exp(b_last - b_s)).astype(BF16)
            upd = lax.dot_general(k_dec, v_s.astype(BF16), _TN, preferred_element_type=F32)
            upds.append(jnp.where(same_head, upd, 0.0))
            e_col = jnp.broadcast_to(jnp.exp(b_last), (kw, kw)).T
            decays.append(jnp.concatenate([e_col, e_col], axis=1))
        o_inter = []
        for s in range(grp):
            lo = s * sub
            o_inter.append(jnp.dot(q_dec[lo:lo + sub, :], state.astype(BF16),
                                   preferred_element_type=F32))
            state = state * decays[s] + upds[s]
        return jnp.concatenate(o_inter, axis=0) + o_diag, state

    def group(g, carry):
        r0 = pl.multiple_of(g * gt, gt)
        for pp in range(npp):
            ks, vs = slice(kw * pp, kw * (pp + 1)), slice(vw * pp, vw * (pp + 1))
            pc = pp % _GLA_PAIRS_PER_VCHUNK
            v_pair = v_refs[pp // _GLA_PAIRS_PER_VCHUNK][pl.ds(r0, gt), vw * pc:vw * (pc + 1)]
            o, state = pair_group(q_ref[pl.ds(r0, gt), ks] * (GLA_DK ** -0.5), k_ref[pl.ds(r0, gt), ks],
                                  b_ref[pl.ds(r0, gt), ks], v_pair, s_ref[pp])
            s_ref[pp] = state
            oacc_ref[pl.ds(r0, gt), vs] = o
        return carry

    lax.fori_loop(0, tb // gt, group, 0)

    o = oacc_ref[...]
    gain = gain_ref[...]
    heads = []
    for h in range(2 * npp):
        oh = o[:, GLA_DV * h:GLA_DV * (h + 1)]
        ms = jnp.mean(oh * oh, axis=-1, keepdims=True)
        heads.append(oh * lax.rsqrt(ms + NORM_EPS) * gain)
    o = jnp.concatenate(heads, axis=1)
    gate = jnp.concatenate([r[...] for r in gg_refs], axis=1)
    o_ref[...] = (o * _silu(gate)).astype(o_ref.dtype)

    @pl.when(ti == n_tblocks - 1)
    def _emit_state():
        for pp in range(npp):
            st = s_ref[pp]
            sout_ref[2 * pp] = st[0:GLA_DK, 0:GLA_DV]
            sout_ref[2 * pp + 1] = st[GLA_DK:2 * GLA_DK, GLA_DV:2 * GLA_DV]


def _gla_prompt(proj, decay, gain, t):
    tb = 256 if t % 256 == 0 else t
    assert t % tb == 0 and tb % (GLA_SUB * _GLA_GROUP) == 0
    nt = t // tb
    npp = _GLA_PAIRS_PER_STEP
    kw, vw = 2 * GLA_DK * npp, 2 * GLA_DV * npp
    vcw = 2 * GLA_DV * _GLA_PAIRS_PER_VCHUNK
    nvc = npp // _GLA_PAIRS_PER_VCHUNK
    assert GLA_PAIRS % npp == 0 and npp % _GLA_PAIRS_PER_VCHUNK == 0
    assert OFF_GQ % kw == 0 and OFF_GK % kw == 0 and OFF_GV % vcw == 0 and OFF_GG % vcw == 0

    def vchunk(off, c):
        return pl.BlockSpec((tb, vcw), lambda p, i: (i, off // vcw + p * nvc + c))

    return pl.pallas_call(
        functools.partial(_gla_prompt_kernel, n_tblocks=nt, tb=tb),
        grid=(GLA_PAIRS // npp, nt),
        in_specs=[pl.BlockSpec((tb, kw), lambda p, i: (i, OFF_GQ // kw + p)),
                  pl.BlockSpec((tb, kw), lambda p, i: (i, OFF_GK // kw + p)),
                  pl.BlockSpec((tb, kw), lambda p, i: (i, p))]
                 + [vchunk(OFF_GV, c) for c in range(nvc)]
                 + [vchunk(OFF_GG, c) for c in range(nvc)]
                 + [pl.BlockSpec((1, GLA_DV), lambda p, i: (0, 0))],
        out_specs=[pl.BlockSpec((tb, vw), lambda p, i: (i, p)),
                   pl.BlockSpec((2 * npp, GLA_DK, GLA_DV), lambda p, i: (p, 0, 0))],
        out_shape=[jax.ShapeDtypeStruct((t, GLA_VW), BF16),
                   jax.ShapeDtypeStruct((GLA_HEADS, GLA_DK, GLA_DV), F32)],
        scratch_shapes=[pltpu.VMEM((npp, 2 * GLA_DK, 2 * GLA_DV), F32),
                        pltpu.VMEM((tb, vw), F32)],
        compiler_params=_params("arbitrary", "arbitrary"),
        name="gla_prompt",
    )(proj, proj, decay, *([proj] * (2 * nvc)), gain)


def _out_proj_kernel(og_ref, om_ref, wg_ref, wm_ref, x_ref, lg_ref, lb_ref, o_ref,
                     *, n_ctiles, tn, alpha):
    j = pl.program_id(1)
    mixed = jnp.dot(og_ref[...], wg_ref[...], preferred_element_type=F32)
    mixed = mixed + jnp.dot(om_ref[...], wm_ref[...], preferred_element_type=F32)
    c0 = pl.multiple_of(j * tn, tn)
    o_ref[:, pl.ds(c0, tn)] = alpha * x_ref[...] + mixed

    @pl.when(j == n_ctiles - 1)
    def _layer_norm():
        h = o_ref[...]
        mu = jnp.mean(h, axis=-1, keepdims=True)
        hc = h - mu
        var = jnp.mean(hc * hc, axis=-1, keepdims=True)
        o_ref[...] = hc * lax.rsqrt(var + NORM_EPS) * lg_ref[...] + lb_ref[...]


def _out_proj(og, om, w_out_bf, x, ln_g, ln_b, alpha, tm, tn):
    m, d = x.shape
    kg, km = og.shape[1], om.shape[1]
    assert kg == km and w_out_bf.shape[0] == kg + km
    nct = d // tn
    return pl.pallas_call(
        functools.partial(_out_proj_kernel, n_ctiles=nct, tn=tn, alpha=alpha),
        grid=(m // tm, nct),
        in_specs=[pl.BlockSpec((tm, kg), lambda i, j: (i, 0)),
                  pl.BlockSpec((tm, km), lambda i, j: (i, 0)),
                  pl.BlockSpec((kg, tn), lambda i, j: (0, j)),
                  pl.BlockSpec((km, tn), lambda i, j: (1, j)),
                  pl.BlockSpec((tm, tn), lambda i, j: (i, j)),
                  pl.BlockSpec((1, d), lambda i, j: (0, 0)),
                  pl.BlockSpec((1, d), lambda i, j: (0, 0))],
        out_specs=pl.BlockSpec((tm, d), lambda i, j: (i, 0)),
        out_shape=jax.ShapeDtypeStruct((m, d), F32),
        compiler_params=_params("arbitrary", "arbitrary"),
        name="out_proj_ln",
    )(og, om, w_out_bf, w_out_bf, x, ln_g, ln_b)


_MOBA_DEC_SEQS = 2


def _moba_decode_kernel(pt_ref, q_ref, kn_ref, vn_ref, g_ref, *rest, n_pages, page):
    del pt_ref
    n_seq = _MOBA_DEC_SEQS
    o_ref = rest[2 * n_seq * n_pages]
    k_refs = [rest[sq * n_pages:(sq + 1) * n_pages] for sq in range(n_seq)]
    v_refs = [rest[(n_seq + sq) * n_pages:(n_seq + sq + 1) * n_pages] for sq in range(n_seq)]
    qs = [q_ref[sq] * (MOBA_HD ** -0.5) for sq in range(n_seq)]
    ss = [_moba_decode_scores(qs[sq], k_refs[sq], n_pages) for sq in range(n_seq)]
    ps = [_moba_decode_select(ss[sq], qs[sq], kn_ref[sq], n_pages, page) for sq in range(n_seq)]
    for sq in range(n_seq):
        pb, p_new, denom = ps[sq]
        o = _moba_decode_values(pb, p_new, denom, vn_ref[sq], v_refs[sq], n_pages, page)
        o_ref[sq] = (o * _silu(g_ref[sq])).astype(o_ref.dtype)


def _moba_decode_scores(q, k_refs, n_pages):
    qb = q.astype(BF16)
    return jnp.concatenate(
        [lax.dot_general(qb, k_refs[p][0, 0].astype(BF16), _NT, preferred_element_type=F32)
         for p in range(n_pages)], axis=1)


def _expand_kv_rows(x):
    return jnp.concatenate([jnp.broadcast_to(x[i:i + 1, :], (MOBA_GROUP, MOBA_HD))
                            for i in range(MOBA_KV_HEADS)], axis=0)


def _moba_decode_select(s, q, kn, n_pages, page):
    heads = MOBA_HEADS
    brow = MOBA_BLOCK * MOBA_KV_HEADS
    n_blk = (n_pages * page) // MOBA_BLOCK
    lane = lax.broadcasted_iota(jnp.int32, s.shape, 1)
    head_kv = lax.broadcasted_iota(jnp.int32, s.shape, 0) // MOBA_GROUP
    own_kv = (lane % MOBA_KV_HEADS) == head_kv
    s_own = jnp.where(own_kv, s, 0.0)

    assert n_blk <= LANE
    blk_lane = lax.broadcasted_iota(jnp.int32, (heads, LANE), 1)
    cur = jnp.full((heads, LANE), NEG_INF, F32)
    for j in range(n_blk):
        mean_j = jnp.sum(s_own[:, brow * j:brow * (j + 1)], axis=1, keepdims=True) * (1.0 / MOBA_BLOCK)
        cur = jnp.where(blk_lane == j, mean_j, cur)
    sel = jnp.zeros((heads, LANE), F32)
    for _ in range(min(MOBA_TOPK, n_blk)):
        mx = jnp.max(cur, axis=1, keepdims=True)
        first = jnp.min(jnp.where(cur == mx, blk_lane, n_blk), axis=1, keepdims=True)
        pick = blk_lane == first
        sel = jnp.where(pick, 1.0, sel)
        cur = jnp.where(pick, NEG_INF, cur)
    bias = jnp.concatenate(
        [jnp.broadcast_to(jnp.where(jnp.sum(jnp.where(blk_lane == j, sel, 0.0), axis=1, keepdims=True) > 0.5,
                                    0.0, NEG_INF), (heads, brow)) for j in range(n_blk)], axis=1)

    s_new = jnp.sum(q * _expand_kv_rows(kn), axis=1, keepdims=True)
    sm = jnp.where(own_kv, s + bias, NEG_INF)
    m = jnp.maximum(jnp.max(sm, axis=1, keepdims=True), s_new)
    p = jnp.exp(sm - m)
    p_new = jnp.exp(s_new - m)
    denom = jnp.sum(p, axis=1, keepdims=True) + p_new
    return p.astype(BF16), p_new, denom


def _moba_decode_values(pb, p_new, denom, vn, v_refs, n_pages, page):
    prow = page * MOBA_KV_HEADS
    o = p_new * _expand_kv_rows(vn)
    for pg in range(n_pages):
        o = o + jnp.dot(pb[:, prow * pg:prow * (pg + 1)], v_refs[pg][0, 0].astype(BF16),
                        preferred_element_type=F32)
    return o * (1.0 / denom)


def _moba_decode(page_table, q3, kn3, vn3, g3, cache_k, cache_v, layer):
    db, n_pages = page_table.shape
    depth, n_pool, page = cache_k.shape[0], cache_k.shape[1], cache_k.shape[2]
    assert (n_pages * page) % MOBA_BLOCK == 0
    prow = page * MOBA_KV_HEADS
    ck = cache_k.reshape(depth, n_pool, prow, MOBA_HD)
    cv = cache_v.reshape(depth, n_pool, prow, MOBA_HD)

    n_seq = _MOBA_DEC_SEQS
    assert db % n_seq == 0

    def page_spec(sq, pg):
        return pl.BlockSpec((1, 1, prow, MOBA_HD), lambda b, pt: (layer, pt[b * n_seq + sq, pg], 0, 0))

    def row_spec(r):
        return pl.BlockSpec((n_seq, r, MOBA_HD), lambda b, pt: (b, 0, 0))

    pages = [page_spec(sq, pg) for sq in range(n_seq) for pg in range(n_pages)]
    grid_spec = pltpu.PrefetchScalarGridSpec(
        num_scalar_prefetch=1,
        grid=(db // n_seq,),
        in_specs=[row_spec(MOBA_HEADS), row_spec(MOBA_KV_HEADS), row_spec(MOBA_KV_HEADS),
                  row_spec(MOBA_HEADS)] + pages + pages,
        out_specs=pl.BlockSpec((n_seq, MOBA_HEADS, MOBA_HD), lambda b, pt: (b, 0, 0)),
    )
    n_ops = n_seq * n_pages
    return pl.pallas_call(
        functools.partial(_moba_decode_kernel, n_pages=n_pages, page=page),
        grid_spec=grid_spec,
        out_shape=jax.ShapeDtypeStruct((db, MOBA_HEADS, MOBA_HD), BF16),
        compiler_params=_params("arbitrary"),
        name="moba_decode",
    )(page_table, q3, kn3, vn3, g3, *([ck] * n_ops), *([cv] * n_ops))


_GLA_DEC_GROUP = 8


def _gla_decode_kernel(q_ref, k_ref, la_ref, v_ref, gg_ref, gain_ref, s_ref, o_ref, sn_ref):
    grp = _GLA_DEC_GROUP

    def columns(x):
        xp = jnp.concatenate([x, jnp.zeros((LANE - grp, x.shape[1]), F32)], axis=0)
        return xp.T

    a_t = columns(jnp.exp(la_ref[...]))
    k_t = columns(k_ref[...])
    q_t = columns(q_ref[...] * (GLA_DK ** -0.5))
    gain = gain_ref[...]
    for i in range(grp):
        rows = []
        for h in range(GLA_HEADS):
            r = slice(GLA_DK * h, GLA_DK * (h + 1))
            v_h = jnp.broadcast_to(v_ref[i, h:h + 1, :], (GLA_DK, GLA_DV))
            s_new = a_t[r, i:i + 1] * s_ref[i, h] + k_t[r, i:i + 1] * v_h
            sn_ref[i, h] = s_new
            rows.append(jnp.sum(q_t[r, i:i + 1] * s_new, axis=0, keepdims=True))
        o = jnp.concatenate(rows, axis=0)
        ms = jnp.mean(o * o, axis=-1, keepdims=True)
        o = o * lax.rsqrt(ms + NORM_EPS) * gain
        o_ref[i] = (o * _silu(gg_ref[i])).astype(o_ref.dtype)


def _gla_decode(q2, k2, la2, v3, gg3, gain, state):
    db = q2.shape[0]
    grp = _GLA_DEC_GROUP
    assert db % grp == 0
    vec = pl.BlockSpec((grp, GLA_KW), lambda g: (g, 0))
    hd3 = pl.BlockSpec((grp, GLA_HEADS, GLA_DV), lambda g: (g, 0, 0))
    st = pl.BlockSpec((grp, GLA_HEADS, GLA_DK, GLA_DV), lambda g: (g, 0, 0, 0))
    return pl.pallas_call(
        _gla_decode_kernel,
        grid=(db // grp,),
        in_specs=[vec, vec, vec, hd3, hd3, pl.BlockSpec((1, GLA_DV), lambda g: (0, 0)), st],
        out_specs=[hd3, st],
        out_shape=[jax.ShapeDtypeStruct((db, GLA_HEADS, GLA_DV), BF16),
                   jax.ShapeDtypeStruct(state.shape, F32)],
        compiler_params=_params("arbitrary"),
        name="gla_decode",
    )(q2, k2, la2, v3, gg3, gain, state)


def _tile(n, prefs):
    for p in prefs:
        if n % p == 0:
            return p
    return n


def kernel(x_prompt, x_sample, cache_k, cache_v, state_gla, page_table,
           w_in, w_a2, b_a, gla_gain, w_out, ln_g, ln_b):
    bsz, t, d = x_prompt.shape
    db, dt, _ = x_sample.shape
    depth = w_in.shape[0]
    assert bsz == 1 and dt == 1
    alpha = (2.0 * depth) ** 0.25
    y_p = x_prompt.reshape(t, d)
    y_s = x_sample.reshape(db, d)
    kp_l, vp_l, sp_l, ks_l, vs_l, ss_l = [], [], [], [], [], []
    for l in range(depth):
        w_in_t = jnp.swapaxes(w_in[l], 0, 1)
        w_lr = jnp.zeros((d, LANE), BF16).at[:, :GLA_LOWRANK].set(w_in[l][:, OFF_GA:].astype(BF16))
        w_a2p = jnp.zeros((LANE, GLA_KW), F32).at[:GLA_LOWRANK, :].set(w_a2[l])
        w_out_bf = w_out[l].astype(BF16)
        gain = gla_gain[l].reshape(1, GLA_DV)
        lg, lb = ln_g[l].reshape(1, d), ln_b[l].reshape(1, d)
        x_all, ga = _stack_cast(y_p, y_s, w_lr, _tile(math.gcd(t, db), (128,)))
        m_all = t + db
        proj = _in_proj(x_all, w_in_t, MAIN_W, _tile(m_all, (520, 512, 256, 128)), 1024)
        ba2 = b_a[l].reshape(1, GLA_KW)
        decay = _log_decay(ga, w_a2p, ba2, 0, t, _tile(math.gcd(t, 512), (512,)), True)
        log_a_s = _log_decay(ga, w_a2p, ba2, t, db, db, False)
        o_m = _moba_prompt(proj, t)
        o_g, s_p = _gla_prompt(proj, decay, gain, t)
        k_rows, v_rows = _kv_rows(proj, t, _tile(t, (512, 256, 128)))
        kp_l.append(k_rows.reshape(bsz, t, MOBA_KV_HEADS, MOBA_HD))
        vp_l.append(v_rows.reshape(bsz, t, MOBA_KV_HEADS, MOBA_HD))
        sp_l.append(s_p.reshape(bsz, GLA_HEADS, GLA_DK, GLA_DV))
        y_p_new = _out_proj(o_g, o_m, w_out_bf, y_p, lg, lb, alpha,
                            _tile(t, (512, 256, 128)), _tile(d, (1024, 512, 256, 128)))
        proj_s = proj[t:]
        k_new = proj_s[:, OFF_MK:OFF_MV].reshape(db, MOBA_KV_HEADS, MOBA_HD)
        v_new = proj_s[:, OFF_MV:OFF_MG].reshape(db, MOBA_KV_HEADS, MOBA_HD)
        o_m_s = _moba_decode(page_table,
                             proj_s[:, OFF_MQ:OFF_MK].reshape(db, MOBA_HEADS, MOBA_HD),
                             k_new, v_new,
                             proj_s[:, OFF_MG:OFF_GQ].reshape(db, MOBA_HEADS, MOBA_HD),
                             cache_k, cache_v, l)
        o_g_s, s_s = _gla_decode(proj_s[:, OFF_GQ:OFF_GK], proj_s[:, OFF_GK:OFF_GV], log_a_s,
                                 proj_s[:, OFF_GV:OFF_GG].reshape(db, GLA_HEADS, GLA_DV),
                                 proj_s[:, OFF_GG:OFF_GA].reshape(db, GLA_HEADS, GLA_DV),
                                 gain, state_gla[l])
        ks_l.append(k_new.reshape(db, dt, MOBA_KV_HEADS, MOBA_HD))
        vs_l.append(v_new.reshape(db, dt, MOBA_KV_HEADS, MOBA_HD))
        ss_l.append(s_s)
        y_s = _out_proj(o_g_s.reshape(db, GLA_VW), o_m_s.reshape(db, MOBA_QW), w_out_bf, y_s, lg, lb,
                        alpha, _tile(db, (128,)), _tile(d, (1024, 512, 256, 128)))
        y_p = y_p_new
    return (y_p.reshape(bsz, t, d), y_s.reshape(db, dt, d),
            jnp.stack(kp_l), jnp.stack(vp_l), jnp.stack(sp_l),
            jnp.stack(ks_l), jnp.stack(vs_l), jnp.stack(ss_l))
```

```python
import functools
import math

import jax
import jax.numpy as jnp
from jax import lax
from jax.experimental import pallas as pl
from jax.experimental.pallas import tpu as pltpu

F32 = jnp.float32
BF16 = jnp.bfloat16
NEG_INF = float("-inf")
MASK_BIAS = -1e30
LOG2E = 1.4426950408889634

MOBA_HEADS = 16
MOBA_KV_HEADS = 4
MOBA_GROUP = MOBA_HEADS // MOBA_KV_HEADS
MOBA_HD = 128
MOBA_BLOCK = 256
MOBA_TOPK = 3
GLA_HEADS = 16
GLA_DK = 64
GLA_DV = 128
GLA_LOWRANK = 16
GLA_GATE_TAU = 16.0
GLA_SUB = 16
GLA_PAIRS = GLA_HEADS // 2
NORM_EPS = 1e-5

MOBA_QW = MOBA_HEADS * MOBA_HD
MOBA_KVW = MOBA_KV_HEADS * MOBA_HD
GLA_KW = GLA_HEADS * GLA_DK
GLA_VW = GLA_HEADS * GLA_DV
OFF_MQ = 0
OFF_MK = OFF_MQ + MOBA_QW
OFF_MV = OFF_MK + MOBA_KVW
OFF_MG = OFF_MV + MOBA_KVW
OFF_GQ = OFF_MG + MOBA_QW
OFF_GK = OFF_GQ + GLA_KW
OFF_GV = OFF_GK + GLA_KW
OFF_GG = OFF_GV + GLA_VW
OFF_GA = OFF_GG + GLA_VW
MAIN_W = OFF_GA

LANE = 128
BF16_SUBLANES = 16
VMEM_LIMIT_BYTES = 60 * 1024 * 1024

_NT = (((1,), (1,)), ((), ()))
_TN = (((0,), (0,)), ((), ()))


def _params(*sem):
    return pltpu.CompilerParams(dimension_semantics=sem, vmem_limit_bytes=VMEM_LIMIT_BYTES)


def _silu(x):
    return x * (1.0 / (1.0 + jnp.exp(-x)))


def _split_bf16(x):
    hi = x.astype(BF16)
    return hi, (x - hi.astype(F32)).astype(BF16)


def _stack_cast_kernel(xp_ref, xs_ref, wlr_ref, o_ref, ga_ref, *, n_prompt_tiles):
    i = pl.program_id(0)

    def emit(x):
        xb = x.astype(o_ref.dtype)
        o_ref[...] = xb
        ga_ref[...] = jnp.dot(xb, wlr_ref[...], preferred_element_type=F32)

    @pl.when(i < n_prompt_tiles)
    def _prompt_rows():
        emit(xp_ref[...])

    @pl.when(i >= n_prompt_tiles)
    def _decode_rows():
        emit(xs_ref[...])


def _stack_cast(xp, xs, w_lr, tr):
    (t, d), db = xp.shape, xs.shape[0]
    assert t % tr == 0 and db % tr == 0
    npt, nst = t // tr, db // tr
    return pl.pallas_call(
        functools.partial(_stack_cast_kernel, n_prompt_tiles=npt),
        grid=(npt + nst,),
        in_specs=[pl.BlockSpec((tr, d), lambda i: (jnp.minimum(i, npt - 1), 0)),
                  pl.BlockSpec((tr, d), lambda i: (jnp.maximum(i - npt, 0), 0)),
                  pl.BlockSpec((d, LANE), lambda i: (0, 0))],
        out_specs=[pl.BlockSpec((tr, d), lambda i: (i, 0)),
                   pl.BlockSpec((tr, LANE), lambda i: (i, 0))],
        out_shape=[jax.ShapeDtypeStruct((t + db, d), BF16),
                   jax.ShapeDtypeStruct((t + db, LANE), F32)],
        compiler_params=_params("arbitrary"),
        name="stack_cast",
    )(xp, xs, w_lr)


def _in_proj_kernel(x_ref, wt_ref, o_ref, wbf_ref):
    @pl.when(pl.program_id(1) == 0)
    def _cast_weight_tile():
        wbf_ref[...] = wt_ref[...].astype(BF16)

    o_ref[...] = lax.dot_general(x_ref[...], wbf_ref[...], _NT, preferred_element_type=F32)


def _in_proj(x, wt, n_out, tm, tn):
    m, k = x.shape
    assert n_out % tn == 0 and m % tm == 0 and wt.shape[0] >= n_out
    return pl.pallas_call(
        _in_proj_kernel,
        grid=(n_out // tn, m // tm),
        in_specs=[pl.BlockSpec((tm, k), lambda j, i: (i, 0)),
                  pl.BlockSpec((tn, k), lambda j, i: (j, 0))],
        out_specs=pl.BlockSpec((tm, tn), lambda j, i: (i, j)),
        out_shape=jax.ShapeDtypeStruct((m, n_out), F32),
        scratch_shapes=[pltpu.VMEM((tn, k), BF16)],
        compiler_params=_params("arbitrary", "arbitrary"),
        name="in_proj",
    )(x, wt)


def _kv_rows_kernel(k_ref, v_ref, ko_ref, vo_ref):
    tok = k_ref.shape[0]
    for h in range(MOBA_KV_HEADS):
        cols = slice(MOBA_HD * h, MOBA_HD * (h + 1))
        ko_ref[pl.ds(h, tok, stride=MOBA_KV_HEADS), :] = k_ref[:, cols]
        vo_ref[pl.ds(h, tok, stride=MOBA_KV_HEADS), :] = v_ref[:, cols]


def _kv_rows(proj, t, tr):
    assert t % tr == 0
    out = jax.ShapeDtypeStruct((t * MOBA_KV_HEADS, MOBA_HD), F32)
    return pl.pallas_call(
        _kv_rows_kernel,
        grid=(t // tr,),
        in_specs=[pl.BlockSpec((tr, MOBA_KVW), lambda i: (i, OFF_MK // MOBA_KVW)),
                  pl.BlockSpec((tr, MOBA_KVW), lambda i: (i, OFF_MV // MOBA_KVW))],
        out_specs=[pl.BlockSpec((tr * MOBA_KV_HEADS, MOBA_HD), lambda i: (i, 0)),
                   pl.BlockSpec((tr * MOBA_KV_HEADS, MOBA_HD), lambda i: (i, 0))],
        out_shape=[out, out],
        compiler_params=_params("arbitrary"),
        name="kv_rows",
    )(proj, proj)


def _log_decay_kernel(ga_ref, wa2_ref, ba_ref, o_ref, *, cumulative):
    ga_hi, ga_lo = _split_bf16(ga_ref[...])
    wa_hi, wa_lo = _split_bf16(wa2_ref[...])
    z = (jnp.dot(ga_hi, wa_hi, preferred_element_type=F32)
         + (jnp.dot(ga_lo, wa_hi, preferred_element_type=F32)
            + jnp.dot(ga_hi, wa_lo, preferred_element_type=F32))) + ba_ref[...]
    log_a = (jnp.minimum(z, 0.0) - jnp.log(1.0 + jnp.exp(-jnp.abs(z)))) * (1.0 / GLA_GATE_TAU)
    if cumulative:
        pos = lax.broadcasted_iota(jnp.int32, log_a.shape, 0) % GLA_SUB
        step = 1
        while step < GLA_SUB:
            log_a = log_a + jnp.where(pos >= step, pltpu.roll(log_a, step, axis=0), 0.0)
            step *= 2
    o_ref[...] = log_a


def _log_decay(ga, w_a2p, b_a, row0, n_rows, tm, cumulative):
    assert row0 % tm == 0 and n_rows % tm == 0 and tm % GLA_SUB == 0
    t0 = row0 // tm
    return pl.pallas_call(
        functools.partial(_log_decay_kernel, cumulative=cumulative),
        grid=(n_rows // tm,),
        in_specs=[pl.BlockSpec((tm, LANE), lambda i: (t0 + i, 0)),
                  pl.BlockSpec((LANE, GLA_KW), lambda i: (0, 0)),
                  pl.BlockSpec((1, GLA_KW), lambda i: (0, 0))],
        out_specs=pl.BlockSpec((tm, GLA_KW), lambda i: (i, 0)),
        out_shape=jax.ShapeDtypeStruct((n_rows, GLA_KW), F32),
        compiler_params=_params("arbitrary"),
        name="gla_log_decay",
    )(ga, w_a2p, b_a)


def _topk_bias(gate, n_valid, n_rows):
    blk = lax.broadcasted_iota(jnp.int32, gate.shape, 0)
    cur = jnp.where(blk < n_valid, gate, NEG_INF)
    sel = jnp.zeros(gate.shape, jnp.bool_)
    for _ in range(MOBA_TOPK):
        mx = jnp.max(cur, axis=0, keepdims=True)
        hit = jnp.logical_and(cur == mx, mx > NEG_INF)
        first = jnp.min(jnp.where(hit, blk, n_rows), axis=0, keepdims=True)
        pick = blk == first
        sel = jnp.logical_or(sel, pick)
        cur = jnp.where(pick, NEG_INF, cur)
    return jnp.where(sel, 0.0, MASK_BIAS)


_MOBA_KV_PER_STEP = 2


def _moba_prompt_kernel(q_ref, k_ref, v_ref, g_ref, o_ref,
                        kaug_ref, vt_ref, kmean_ref, qaug_ref, m_ref, acc_ref, sa_ref, sb_ref,
                        *, n_blocks):
    qi = pl.program_id(1)
    blk_sz = MOBA_BLOCK
    hd = MOBA_HD
    rows = MOBA_GROUP * blk_sz
    gw = MOBA_GROUP * hd
    chains = range(_MOBA_KV_PER_STEP)

    @pl.when(qi == 0)
    def _prepare_kv_heads():
        kmean_ref[...] = jnp.zeros(kmean_ref.shape, F32)
        vt_ref[:, hd:, :] = jnp.ones((vt_ref.shape[0], vt_ref.shape[1] - hd, vt_ref.shape[2]), BF16)
        lane_blk = lax.broadcasted_iota(jnp.int32, (blk_sz, LANE), 1)

        def body(j, carry):
            r0 = pl.multiple_of(j * blk_sz, blk_sz)
            for c in chains:
                kb = k_ref[pl.ds(r0, blk_sz), hd * c:hd * (c + 1)]
                kaug_ref[c, pl.ds(r0, blk_sz), 0:hd] = kb.astype(BF16)
                kaug_ref[c, pl.ds(r0, blk_sz), hd:hd + LANE] = jnp.where(lane_blk == j, 1.0, 0.0).astype(BF16)
                kmean_ref[c, pl.ds(j, 1), :] = jnp.sum(kb, axis=0, keepdims=True) * (1.0 / blk_sz)
                vt_ref[c, 0:hd, pl.ds(r0, blk_sz)] = v_ref[pl.ds(r0, blk_sz), hd * c:hd * (c + 1)].T.astype(BF16)
            return carry

        lax.fori_loop(0, n_blocks, body, 0)

    nb_pad = kmean_ref.shape[1]
    for c in chains:
        q = q_ref[:, gw * c:gw * (c + 1)]
        qs = jnp.concatenate([q[:, hd * g:hd * (g + 1)] for g in range(MOBA_GROUP)], axis=0)
        gate = lax.dot_general(kmean_ref[c], qs, _NT, precision=lax.Precision.HIGHEST,
                               preferred_element_type=F32)
        bias = _topk_bias(gate, qi, nb_pad)
        qaug_ref[c, 0:hd, :] = (qs * (hd ** -0.5 * LOG2E)).T.astype(BF16)
        qaug_ref[c, hd:hd + nb_pad, :] = bias.astype(BF16)
        qaug_ref[c, hd + nb_pad:, :] = jnp.full((LANE - nb_pad, rows), MASK_BIAS, BF16)

    def scores(c, r0):
        return jnp.dot(kaug_ref[c, pl.ds(r0, blk_sz), :], qaug_ref[c],
                       preferred_element_type=F32)

    def accumulate(c, s, r0):
        m_prev = m_ref[c]
        m_new = jnp.maximum(m_prev, jnp.max(s, axis=0, keepdims=True))
        p = jnp.exp2(s - m_new).astype(BF16)
        pv = jnp.dot(vt_ref[c, :, pl.ds(r0, blk_sz)], p, preferred_element_type=F32)
        acc_ref[c] = jnp.exp2(m_prev - m_new) * acc_ref[c] + pv
        m_ref[c] = m_new

    for c in chains:
        sa_ref[c] = scores(c, 0)

    r_own = pl.multiple_of(qi * blk_sz, blk_sz)
    key_t = lax.broadcasted_iota(jnp.int32, (blk_sz, rows), 0)
    row_t = lax.broadcasted_iota(jnp.int32, (blk_sz, rows), 1) % blk_sz
    for c in chains:
        s = jnp.dot(kaug_ref[c, pl.ds(r_own, blk_sz), 0:hd], qaug_ref[c, 0:hd, :],
                    preferred_element_type=F32)
        s = jnp.where(key_t <= row_t, s, NEG_INF)
        m0 = jnp.max(s, axis=0, keepdims=True)
        m_ref[c] = m0
        acc_ref[c] = jnp.dot(vt_ref[c, :, pl.ds(r_own, blk_sz)], jnp.exp2(s - m0).astype(BF16),
                             preferred_element_type=F32)

    n_pairs = (qi + 1) // 2

    def past_pair(jj, carry):
        r0 = pl.multiple_of(jj * (2 * blk_sz), 2 * blk_sz)
        r1 = pl.multiple_of(r0 + blk_sz, blk_sz)
        r2 = pl.multiple_of(jnp.minimum(r0 + 2 * blk_sz, (n_blocks - 1) * blk_sz), blk_sz)
        for c in chains:
            sb_ref[c] = scores(c, r1)
        for c in chains:
            accumulate(c, sa_ref[c], r0)
        for c in chains:
            sa_ref[c] = scores(c, r2)
        for c in chains:
            accumulate(c, sb_ref[c], r1)
        return carry

    lax.fori_loop(0, n_pairs, past_pair, 0)

    for c in chains:
        acc = acc_ref[c]
        o = (acc[0:hd, :] * (1.0 / acc[hd:hd + 1, :])).T
        o = jnp.concatenate([o[blk_sz * g:blk_sz * (g + 1), :] for g in range(MOBA_GROUP)], axis=1)
        o_ref[:, gw * c:gw * (c + 1)] = (o * _silu(g_ref[:, gw * c:gw * (c + 1)])).astype(o_ref.dtype)


def _moba_prompt(proj, t):
    assert t % (2 * MOBA_BLOCK) == 0
    nb = t // MOBA_BLOCK
    nb_pad = -(-nb // BF16_SUBLANES) * BF16_SUBLANES
    assert nb_pad <= LANE
    rows = MOBA_GROUP * MOBA_BLOCK
    nc = _MOBA_KV_PER_STEP
    gw = MOBA_GROUP * MOBA_HD * nc
    kvw = MOBA_HD * nc
    assert MOBA_KV_HEADS % nc == 0
    assert OFF_MQ % gw == 0 and OFF_MG % gw == 0 and OFF_MK % kvw == 0 and OFF_MV % kvw == 0
    kv_spec = functools.partial(pl.BlockSpec, (t, kvw), pipeline_mode=pl.Buffered(1))
    return pl.pallas_call(
        functools.partial(_moba_prompt_kernel, n_blocks=nb),
        grid=(MOBA_KV_HEADS // nc, nb),
        in_specs=[pl.BlockSpec((MOBA_BLOCK, gw), lambda h, i: (i, OFF_MQ // gw + h)),
                  kv_spec(lambda h, i: (0, OFF_MK // kvw + h)),
                  kv_spec(lambda h, i: (0, OFF_MV // kvw + h)),
                  pl.BlockSpec((MOBA_BLOCK, gw), lambda h, i: (i, OFF_MG // gw + h))],
        out_specs=pl.BlockSpec((MOBA_BLOCK, gw), lambda h, i: (i, h)),
        out_shape=jax.ShapeDtypeStruct((t, MOBA_QW), BF16),
        scratch_shapes=[pltpu.VMEM((nc, t, MOBA_HD + LANE), BF16),
                        pltpu.VMEM((nc, MOBA_HD + BF16_SUBLANES, t), BF16),
                        pltpu.VMEM((nc, nb_pad, MOBA_HD), F32),
                        pltpu.VMEM((nc, MOBA_HD + LANE, rows), BF16),
                        pltpu.VMEM((nc, 1, rows), F32),
                        pltpu.VMEM((nc, MOBA_HD + BF16_SUBLANES, rows), F32),
                        pltpu.VMEM((nc, MOBA_BLOCK, rows), F32),
                        pltpu.VMEM((nc, MOBA_BLOCK, rows), F32)],
        compiler_params=_params("arbitrary", "arbitrary"),
        name="moba_prompt",
    )(proj, proj, proj, proj)


_GLA_GROUP = 8
_GLA_PAIRS_PER_STEP = 8
_GLA_PAIRS_PER_VCHUNK = 4


def _gla_prompt_kernel(q_ref, k_ref, b_ref, *rest, n_tblocks, tb):
    ti = pl.program_id(1)
    sub = GLA_SUB
    kw = 2 * GLA_DK
    vw = 2 * GLA_DV
    npp = _GLA_PAIRS_PER_STEP
    nvc = npp // _GLA_PAIRS_PER_VCHUNK
    v_refs, gg_refs = rest[:nvc], rest[nvc:2 * nvc]
    gain_ref, o_ref, sout_ref, s_ref, oacc_ref = rest[2 * nvc:]

    @pl.when(ti == 0)
    def _zero_state():
        s_ref[...] = jnp.zeros(s_ref.shape, F32)

    row_h = lax.broadcasted_iota(jnp.int32, (kw, vw), 0) // GLA_DK
    col_h = lax.broadcasted_iota(jnp.int32, (kw, vw), 1) // GLA_DV
    same_head = row_h == col_h
    head_sum = jnp.where(same_head, 1.0, 0.0).astype(BF16)
    j_idx = lax.broadcasted_iota(jnp.int32, (sub, kw), 0)
    grp = _GLA_GROUP
    lane_head = lax.broadcasted_iota(jnp.int32, (grp * sub, kw), 1) // GLA_DK
    gt = grp * sub
    half = sub // 2
    sub_rows = half * half + half * sub
    rs_in = lax.broadcasted_iota(jnp.int32, (sub, sub_rows), 1)
    rs_tok = jnp.where(rs_in < half * half, rs_in // half, half + (rs_in - half * half) // sub)
    row_sum = jnp.where(lax.broadcasted_iota(jnp.int32, (sub, sub_rows), 0) == rs_tok,
                        1.0, 0.0).astype(BF16)

    def pair_group(q, k, b, v, state):
        w_rows, v_rows = [], []
        for s in range(grp):
            lo = s * sub
            b_s, k_s, v_s = b[lo:lo + sub, :], k[lo:lo + sub, :], v[lo:lo + sub, :]
            for i in range(sub):
                nj = half if i < half else sub
                d = b_s[i:i + 1, :] - b_s[0:nj, :]
                dec = jnp.exp(jnp.where(j_idx[0:nj, :] < i, d, NEG_INF))
                w_rows.append(dec * k_s[0:nj, :] * q[lo + i:lo + i + 1, :])
                v_rows.append(v_s[0:nj, :])
        w = jnp.concatenate(w_rows, axis=0).astype(BF16)
        sc = jnp.dot(w, head_sum, preferred_element_type=F32)
        z = (sc * jnp.concatenate(v_rows, axis=0)).astype(BF16)
        o_diag = jnp.concatenate(
            [jnp.dot(row_sum, z[sub_rows * s:sub_rows * (s + 1), :], preferred_element_type=F32)
             for s in range(grp)], axis=0)
        qk = q * k
        o_self = jnp.concatenate(
            [jnp.sum(jnp.where(lane_head == h, qk, 0.0), axis=1, keepdims=True)
             * v[:, GLA_DV * h:GLA_DV * (h + 1)] for h in range(2)], axis=1)
        o_diag = o_diag + o_self
        q_dec = (q * jnp.exp(b)).astype(BF16)
        upds, decays = [], []
        for s in range(grp):
            lo = s * sub
            b_s, k_s, v_s = b[lo:lo + sub, :], k[lo:lo + sub, :], v[lo:lo + sub, :]
            b_last = b_s[sub - 1:sub, :]
            k_dec = (k_s * jnp.exp(b_last - b_s)).astype(BF16)
            upd = lax.dot_general(k_dec, v_s.astype(BF16), _TN, preferred_element_type=F32)
            upds.append(jnp.where(same_head, upd, 0.0))
            e_col = jnp.broadcast_to(jnp.exp(b_last), (kw, kw)).T
            decays.append(jnp.concatenate([e_col, e_col], axis=1))
        o_inter = []
        for s in range(grp):
            lo = s * sub
            o_inter.append(jnp.dot(q_dec[lo:lo + sub, :], state.astype(BF16),
                                   preferred_element_type=F32))
            state = state * decays[s] + upds[s]
        return jnp.concatenate(o_inter, axis=0) + o_diag, state

    def group(g, carry):
        r0 = pl.multiple_of(g * gt, gt)
        for pp in range(npp):
            ks, vs = slice(kw * pp, kw * (pp + 1)), slice(vw * pp, vw * (pp + 1))
            pc = pp % _GLA_PAIRS_PER_VCHUNK
            v_pair = v_refs[pp // _GLA_PAIRS_PER_VCHUNK][pl.ds(r0, gt), vw * pc:vw * (pc + 1)]
            o, state = pair_group(q_ref[pl.ds(r0, gt), ks] * (GLA_DK ** -0.5), k_ref[pl.ds(r0, gt), ks],
                                  b_ref[pl.ds(r0, gt), ks], v_pair, s_ref[pp])
            s_ref[pp] = state
            oacc_ref[pl.ds(r0, gt), vs] = o
        return carry

    lax.fori_loop(0, tb // gt, group, 0)

    o = oacc_ref[...]
    gain = gain_ref[...]
    heads = []
    for h in range(2 * npp):
        oh = o[:, GLA_DV * h:GLA_DV * (h + 1)]
        ms = jnp.mean(oh * oh, axis=-1, keepdims=True)
        heads.append(oh * lax.rsqrt(ms + NORM_EPS) * gain)
    o = jnp.concatenate(heads, axis=1)
    gate = jnp.concatenate([r[...] for r in gg_refs], axis=1)
    o_ref[...] = (o * _silu(gate)).astype(o_ref.dtype)

    @pl.when(ti == n_tblocks - 1)
    def _emit_state():
        for pp in range(npp):
            st = s_ref[pp]
            sout_ref[2 * pp] = st[0:GLA_DK, 0:GLA_DV]
            sout_ref[2 * pp + 1] = st[GLA_DK:2 * GLA_DK, GLA_DV:2 * GLA_DV]


def _gla_prompt(proj, decay, gain, t):
    tb = 256 if t % 256 == 0 else t
    assert t % tb == 0 and tb % (GLA_SUB * _GLA_GROUP) == 0
    nt = t // tb
    npp = _GLA_PAIRS_PER_STEP
    kw, vw = 2 * GLA_DK * npp, 2 * GLA_DV * npp
    vcw = 2 * GLA_DV * _GLA_PAIRS_PER_VCHUNK
    nvc = npp // _GLA_PAIRS_PER_VCHUNK
    assert GLA_PAIRS % npp == 0 and npp % _GLA_PAIRS_PER_VCHUNK == 0
    assert OFF_GQ % kw == 0 and OFF_GK % kw == 0 and OFF_GV % vcw == 0 and OFF_GG % vcw == 0

    def vchunk(off, c):
        return pl.BlockSpec((tb, vcw), lambda p, i: (i, off // vcw + p * nvc + c))

    return pl.pallas_call(
        functools.partial(_gla_prompt_kernel, n_tblocks=nt, tb=tb),
        grid=(GLA_PAIRS // npp, nt),
        in_specs=[pl.BlockSpec((tb, kw), lambda p, i: (i, OFF_GQ // kw + p)),
                  pl.BlockSpec((tb, kw), lambda p, i: (i, OFF_GK // kw + p)),
                  pl.BlockSpec((tb, kw), lambda p, i: (i, p))]
                 + [vchunk(OFF_GV, c) for c in range(nvc)]
                 + [vchunk(OFF_GG, c) for c in range(nvc)]
                 + [pl.BlockSpec((1, GLA_DV), lambda p, i: (0, 0))],
        out_specs=[pl.BlockSpec((tb, vw), lambda p, i: (i, p)),
                   pl.BlockSpec((2 * npp, GLA_DK, GLA_DV), lambda p, i: (p, 0, 0))],
        out_shape=[jax.ShapeDtypeStruct((t, GLA_VW), BF16),
                   jax.ShapeDtypeStruct((GLA_HEADS, GLA_DK, GLA_DV), F32)],
        scratch_shapes=[pltpu.VMEM((npp, 2 * GLA_DK, 2 * GLA_DV), F32),
                        pltpu.VMEM((tb, vw), F32)],
        compiler_params=_params("arbitrary", "arbitrary"),
        name="gla_prompt",
    )(proj, proj, decay, *([proj] * (2 * nvc)), gain)


def _out_proj_kernel(og_ref, om_ref, wg_ref, wm_ref, x_ref, lg_ref, lb_ref, o_ref,
                     *, n_ctiles, tn, alpha):
    j = pl.program_id(1)
    mixed = jnp.dot(og_ref[...], wg_ref[...], preferred_element_type=F32)
    mixed = mixed + jnp.dot(om_ref[...], wm_ref[...], preferred_element_type=F32)
    c0 = pl.multiple_of(j * tn, tn)
    o_ref[:, pl.ds(c0, tn)] = alpha * x_ref[...] + mixed

    @pl.when(j == n_ctiles - 1)
    def _layer_norm():
        h = o_ref[...]
        mu = jnp.mean(h, axis=-1, keepdims=True)
        hc = h - mu
        var = jnp.mean(hc * hc, axis=-1, keepdims=True)
        o_ref[...] = hc * lax.rsqrt(var + NORM_EPS) * lg_ref[...] + lb_ref[...]


def _out_proj(og, om, w_out_bf, x, ln_g, ln_b, alpha, tm, tn):
    m, d = x.shape
    kg, km = og.shape[1], om.shape[1]
    assert kg == km and w_out_bf.shape[0] == kg + km
    nct = d // tn
    return pl.pallas_call(
        functools.partial(_out_proj_kernel, n_ctiles=nct, tn=tn, alpha=alpha),
        grid=(m // tm, nct),
        in_specs=[pl.BlockSpec((tm, kg), lambda i, j: (i, 0)),
                  pl.BlockSpec((tm, km), lambda i, j: (i, 0)),
                  pl.BlockSpec((kg, tn), lambda i, j: (0, j)),
                  pl.BlockSpec((km, tn), lambda i, j: (1, j)),
                  pl.BlockSpec((tm, tn), lambda i, j: (i, j)),
                  pl.BlockSpec((1, d), lambda i, j: (0, 0)),
                  pl.BlockSpec((1, d), lambda i, j: (0, 0))],
        out_specs=pl.BlockSpec((tm, d), lambda i, j: (i, 0)),
        out_shape=jax.ShapeDtypeStruct((m, d), F32),
        compiler_params=_params("arbitrary", "arbitrary"),
        name="out_proj_ln",
    )(og, om, w_out_bf, w_out_bf, x, ln_g, ln_b)


_MOBA_DEC_SEQS = 2


def _moba_decode_kernel(pt_ref, q_ref, kn_ref, vn_ref, g_ref, *rest, n_pages, page):
    del pt_ref
    n_seq = _MOBA_DEC_SEQS
    o_ref = rest[2 * n_seq * n_pages]
    k_refs = [rest[sq * n_pages:(sq + 1) * n_pages] for sq in range(n_seq)]
    v_refs = [rest[(n_seq + sq) * n_pages:(n_seq + sq + 1) * n_pages] for sq in range(n_seq)]
    qs = [q_ref[sq] * (MOBA_HD ** -0.5) for sq in range(n_seq)]
    ss = [_moba_decode_scores(qs[sq], k_refs[sq], n_pages) for sq in range(n_seq)]
    ps = [_moba_decode_select(ss[sq], qs[sq], kn_ref[sq], n_pages, page) for sq in range(n_seq)]
    for sq in range(n_seq):
        pb, p_new, denom = ps[sq]
        o = _moba_decode_values(pb, p_new, denom, vn_ref[sq], v_refs[sq], n_pages, page)
        o_ref[sq] = (o * _silu(g_ref[sq])).astype(o_ref.dtype)


def _moba_decode_scores(q, k_refs, n_pages):
    qb = q.astype(BF16)
    return jnp.concatenate(
        [lax.dot_general(qb, k_refs[p][0, 0].astype(BF16), _NT, preferred_element_type=F32)
         for p in range(n_pages)], axis=1)


def _expand_kv_rows(x):
    return jnp.concatenate([jnp.broadcast_to(x[i:i + 1, :], (MOBA_GROUP, MOBA_HD))
                            for i in range(MOBA_KV_HEADS)], axis=0)


def _moba_decode_select(s, q, kn, n_pages, page):
    heads = MOBA_HEADS
    brow = MOBA_BLOCK * MOBA_KV_HEADS
    n_blk = (n_pages * page) // MOBA_BLOCK
    lane = lax.broadcasted_iota(jnp.int32, s.shape, 1)
    head_kv = lax.broadcasted_iota(jnp.int32, s.shape, 0) // MOBA_GROUP
    own_kv = (lane % MOBA_KV_HEADS) == head_kv
    s_own = jnp.where(own_kv, s, 0.0)

    assert n_blk <= LANE
    blk_lane = lax.broadcasted_iota(jnp.int32, (heads, LANE), 1)
    cur = jnp.full((heads, LANE), NEG_INF, F32)
    for j in range(n_blk):
        mean_j = jnp.sum(s_own[:, brow * j:brow * (j + 1)], axis=1, keepdims=True) * (1.0 / MOBA_BLOCK)
        cur = jnp.where(blk_lane == j, mean_j, cur)
    sel = jnp.zeros((heads, LANE), F32)
    for _ in range(min(MOBA_TOPK, n_blk)):
        mx = jnp.max(cur, axis=1, keepdims=True)
        first = jnp.min(jnp.where(cur == mx, blk_lane, n_blk), axis=1, keepdims=True)
        pick = blk_lane == first
        sel = jnp.where(pick, 1.0, sel)
        cur = jnp.where(pick, NEG_INF, cur)
    bias = jnp.concatenate(
        [jnp.broadcast_to(jnp.where(jnp.sum(jnp.where(blk_lane == j, sel, 0.0), axis=1, keepdims=True) > 0.5,
                                    0.0, NEG_INF), (heads, brow)) for j in range(n_blk)], axis=1)

    s_new = jnp.sum(q * _expand_kv_rows(kn), axis=1, keepdims=True)
    sm = jnp.where(own_kv, s + bias, NEG_INF)
    m = jnp.maximum(jnp.max(sm, axis=1, keepdims=True), s_new)
    p = jnp.exp(sm - m)
    p_new = jnp.exp(s_new - m)
    denom = jnp.sum(p, axis=1, keepdims=True) + p_new
    return p.astype(BF16), p_new, denom


def _moba_decode_values(pb, p_new, denom, vn, v_refs, n_pages, page):
    prow = page * MOBA_KV_HEADS
    o = p_new * _expand_kv_rows(vn)
    for pg in range(n_pages):
        o = o + jnp.dot(pb[:, prow * pg:prow * (pg + 1)], v_refs[pg][0, 0].astype(BF16),
                        preferred_element_type=F32)
    return o * (1.0 / denom)


def _moba_decode(page_table, q3, kn3, vn3, g3, cache_k, cache_v, layer):
    db, n_pages = page_table.shape
    depth, n_pool, page = cache_k.shape[0], cache_k.shape[1], cache_k.shape[2]
    assert (n_pages * page) % MOBA_BLOCK == 0
    prow = page * MOBA_KV_HEADS
    ck = cache_k.reshape(depth, n_pool, prow, MOBA_HD)
    cv = cache_v.reshape(depth, n_pool, prow, MOBA_HD)

    n_seq = _MOBA_DEC_SEQS
    assert db % n_seq == 0

    def page_spec(sq, pg):
        return pl.BlockSpec((1, 1, prow, MOBA_HD), lambda b, pt: (layer, pt[b * n_seq + sq, pg], 0, 0))

    def row_spec(r):
        return pl.BlockSpec((n_seq, r, MOBA_HD), lambda b, pt: (b, 0, 0))

    pages = [page_spec(sq, pg) for sq in range(n_seq) for pg in range(n_pages)]
    grid_spec = pltpu.PrefetchScalarGridSpec(
        num_scalar_prefetch=1,
        grid=(db // n_seq,),
        in_specs=[row_spec(MOBA_HEADS), row_spec(MOBA_KV_HEADS), row_spec(MOBA_KV_HEADS),
                  row_spec(MOBA_HEADS)] + pages + pages,
        out_specs=pl.BlockSpec((n_seq, MOBA_HEADS, MOBA_HD), lambda b, pt: (b, 0, 0)),
    )
    n_ops = n_seq * n_pages
    return pl.pallas_call(
        functools.partial(_moba_decode_kernel, n_pages=n_pages, page=page),
        grid_spec=grid_spec,
        out_shape=jax.ShapeDtypeStruct((db, MOBA_HEADS, MOBA_HD), BF16),
        compiler_params=_params("arbitrary"),
        name="moba_decode",
    )(page_table, q3, kn3, vn3, g3, *([ck] * n_ops), *([cv] * n_ops))


_GLA_DEC_GROUP = 8


def _gla_decode_kernel(q_ref, k_ref, la_ref, v_ref, gg_ref, gain_ref, s_ref, o_ref, sn_ref):
    grp = _GLA_DEC_GROUP

    def columns(x):
        xp = jnp.concatenate([x, jnp.zeros((LANE - grp, x.shape[1]), F32)], axis=0)
        return xp.T

    a_t = columns(jnp.exp(la_ref[...]))
    k_t = columns(k_ref[...])
    q_t = columns(q_ref[...] * (GLA_DK ** -0.5))
    gain = gain_ref[...]
    for i in range(grp):
        rows = []
        for h in range(GLA_HEADS):
            r = slice(GLA_DK * h, GLA_DK * (h + 1))
            v_h = jnp.broadcast_to(v_ref[i, h:h + 1, :], (GLA_DK, GLA_DV))
            s_new = a_t[r, i:i + 1] * s_ref[i, h] + k_t[r, i:i + 1] * v_h
            sn_ref[i, h] = s_new
            rows.append(jnp.sum(q_t[r, i:i + 1] * s_new, axis=0, keepdims=True))
        o = jnp.concatenate(rows, axis=0)
        ms = jnp.mean(o * o, axis=-1, keepdims=True)
        o = o * lax.rsqrt(ms + NORM_EPS) * gain
        o_ref[i] = (o * _silu(gg_ref[i])).astype(o_ref.dtype)


def _gla_decode(q2, k2, la2, v3, gg3, gain, state):
    db = q2.shape[0]
    grp = _GLA_DEC_GROUP
    assert db % grp == 0
    vec = pl.BlockSpec((grp, GLA_KW), lambda g: (g, 0))
    hd3 = pl.BlockSpec((grp, GLA_HEADS, GLA_DV), lambda g: (g, 0, 0))
    st = pl.BlockSpec((grp, GLA_HEADS, GLA_DK, GLA_DV), lambda g: (g, 0, 0, 0))
    return pl.pallas_call(
        _gla_decode_kernel,
        grid=(db // grp,),
        in_specs=[vec, vec, vec, hd3, hd3, pl.BlockSpec((1, GLA_DV), lambda g: (0, 0)), st],
        out_specs=[hd3, st],
        out_shape=[jax.ShapeDtypeStruct((db, GLA_HEADS, GLA_DV), BF16),
                   jax.ShapeDtypeStruct(state.shape, F32)],
        compiler_params=_params("arbitrary"),
        name="gla_decode",
    )(q2, k2, la2, v3, gg3, gain, state)


def _tile(n, prefs):
    for p in prefs:
        if n % p == 0:
            return p
    return n


def kernel(x_prompt, x_sample, cache_k, cache_v, state_gla, page_table,
           w_in, w_a2, b_a, gla_gain, w_out, ln_g, ln_b):
    bsz, t, d = x_prompt.shape
    db, dt, _ = x_sample.shape
    depth = w_in.shape[0]
    assert bsz == 1 and dt == 1
    alpha = (2.0 * depth) ** 0.25
    y_p = x_prompt.reshape(t, d)
    y_s = x_sample.reshape(db, d)
    kp_l, vp_l, sp_l, ks_l, vs_l, ss_l = [], [], [], [], [], []
    for l in range(depth):
        w_in_t = jnp.swapaxes(w_in[l], 0, 1)
        w_lr = jnp.zeros((d, LANE), BF16).at[:, :GLA_LOWRANK].set(w_in[l][:, OFF_GA:].astype(BF16))
        w_a2p = jnp.zeros((LANE, GLA_KW), F32).at[:GLA_LOWRANK, :].set(w_a2[l])
        w_out_bf = w_out[l].astype(BF16)
        gain = gla_gain[l].reshape(1, GLA_DV)
        lg, lb = ln_g[l].reshape(1, d), ln_b[l].reshape(1, d)
        x_all, ga = _stack_cast(y_p, y_s, w_lr, _tile(math.gcd(t, db), (128,)))
        m_all = t + db
        proj = _in_proj(x_all, w_in_t, MAIN_W, _tile(m_all, (640, 512, 256, 128)), 1024)
        ba2 = b_a[l].reshape(1, GLA_KW)
        decay = _log_decay(ga, w_a2p, ba2, 0, t, _tile(math.gcd(t, 512), (512,)), True)
        log_a_s = _log_decay(ga, w_a2p, ba2, t, db, db, False)
        o_m = _moba_prompt(proj, t)
        o_g, s_p = _gla_prompt(proj, decay, gain, t)
        k_rows, v_rows = _kv_rows(proj, t, _tile(t, (512, 256, 128)))
        kp_l.append(k_rows.reshape(bsz, t, MOBA_KV_HEADS, MOBA_HD))
        vp_l.append(v_rows.reshape(bsz, t, MOBA_KV_HEADS, MOBA_HD))
        sp_l.append(s_p.reshape(bsz, GLA_HEADS, GLA_DK, GLA_DV))
        y_p_new = _out_proj(o_g, o_m, w_out_bf, y_p, lg, lb, alpha,
                            _tile(t, (512, 256, 128)), _tile(d, (1024, 512, 256, 128)))
        proj_s = proj[t:]
        k_new = proj_s[:, OFF_MK:OFF_MV].reshape(db, MOBA_KV_HEADS, MOBA_HD)
        v_new = proj_s[:, OFF_MV:OFF_MG].reshape(db, MOBA_KV_HEADS, MOBA_HD)
        o_m_s = _moba_decode(page_table,
                             proj_s[:, OFF_MQ:OFF_MK].reshape(db, MOBA_HEADS, MOBA_HD),
                             k_new, v_new,
                             proj_s[:, OFF_MG:OFF_GQ].reshape(db, MOBA_HEADS, MOBA_HD),
                             cache_k, cache_v, l)
        o_g_s, s_s = _gla_decode(proj_s[:, OFF_GQ:OFF_GK], proj_s[:, OFF_GK:OFF_GV], log_a_s,
                                 proj_s[:, OFF_GV:OFF_GG].reshape(db, GLA_HEADS, GLA_DV),
                                 proj_s[:, OFF_GG:OFF_GA].reshape(db, GLA_HEADS, GLA_DV),
                                 gain, state_gla[l])
        ks_l.append(k_new.reshape(db, dt, MOBA_KV_HEADS, MOBA_HD))
        vs_l.append(v_new.reshape(db, dt, MOBA_KV_HEADS, MOBA_HD))
        ss_l.append(s_s)
        y_s = _out_proj(o_g_s.reshape(db, GLA_VW), o_m_s.reshape(db, MOBA_QW), w_out_bf, y_s, lg, lb,
                        alpha, _tile(db, (128,)), _tile(d, (1024, 512, 256, 128)))
        y_p = y_p_new
    return (y_p.reshape(bsz, t, d), y_s.reshape(db, dt, d),
            jnp.stack(kp_l), jnp.stack(vp_l), jnp.stack(sp_l),
            jnp.stack(ks_l), jnp.stack(vs_l), jnp.stack(ss_l))
```

```python
import functools
import math

import jax
import jax.numpy as jnp
from jax import lax
from jax.experimental import pallas as pl
from jax.experimental.pallas import tpu as pltpu

F32 = jnp.float32
BF16 = jnp.bfloat16
NEG_INF = float("-inf")
MASK_BIAS = -1e30
LOG2E = 1.4426950408889634

MOBA_HEADS = 16
MOBA_KV_HEADS = 4
MOBA_GROUP = MOBA_HEADS // MOBA_KV_HEADS
MOBA_HD = 128
MOBA_BLOCK = 256
MOBA_TOPK = 3
GLA_HEADS = 16
GLA_DK = 64
GLA_DV = 128
GLA_LOWRANK = 16
GLA_GATE_TAU = 16.0
GLA_SUB = 16
GLA_PAIRS = GLA_HEADS // 2
NORM_EPS = 1e-5

MOBA_QW = MOBA_HEADS * MOBA_HD
MOBA_KVW = MOBA_KV_HEADS * MOBA_HD
GLA_KW = GLA_HEADS * GLA_DK
GLA_VW = GLA_HEADS * GLA_DV
OFF_MQ = 0
OFF_MK = OFF_MQ + MOBA_QW
OFF_MV = OFF_MK + MOBA_KVW
OFF_MG = OFF_MV + MOBA_KVW
OFF_GQ = OFF_MG + MOBA_QW
OFF_GK = OFF_GQ + GLA_KW
OFF_GV = OFF_GK + GLA_KW
OFF_GG = OFF_GV + GLA_VW
OFF_GA = OFF_GG + GLA_VW
MAIN_W = OFF_GA

LANE = 128
BF16_SUBLANES = 16
VMEM_LIMIT_BYTES = 60 * 1024 * 1024

_NT = (((1,), (1,)), ((), ()))
_TN = (((0,), (0,)), ((), ()))


def _params(*sem):
    return pltpu.CompilerParams(dimension_semantics=sem, vmem_limit_bytes=VMEM_LIMIT_BYTES)


def _silu(x):
    return x * (1.0 / (1.0 + jnp.exp(-x)))


def _split_bf16(x):
    hi = x.astype(BF16)
    return hi, (x - hi.astype(F32)).astype(BF16)


def _stack_cast_kernel(xp_ref, xs_ref, wlr_ref, o_ref, ga_ref, *, n_prompt_tiles):
    i = pl.program_id(0)

    def emit(x):
        xb = x.astype(o_ref.dtype)
        o_ref[...] = xb
        ga_ref[...] = jnp.dot(xb, wlr_ref[...], preferred_element_type=F32)

    @pl.when(i < n_prompt_tiles)
    def _prompt_rows():
        emit(xp_ref[...])

    @pl.when(i >= n_prompt_tiles)
    def _decode_rows():
        emit(xs_ref[...])


def _stack_cast(xp, xs, w_lr, tr):
    (t, d), db = xp.shape, xs.shape[0]
    assert t % tr == 0 and db % tr == 0
    npt, nst = t // tr, db // tr
    return pl.pallas_call(
        functools.partial(_stack_cast_kernel, n_prompt_tiles=npt),
        grid=(npt + nst,),
        in_specs=[pl.BlockSpec((tr, d), lambda i: (jnp.minimum(i, npt - 1), 0)),
                  pl.BlockSpec((tr, d), lambda i: (jnp.maximum(i - npt, 0), 0)),
                  pl.BlockSpec((d, LANE), lambda i: (0, 0))],
        out_specs=[pl.BlockSpec((tr, d), lambda i: (i, 0)),
                   pl.BlockSpec((tr, LANE), lambda i: (i, 0))],
        out_shape=[jax.ShapeDtypeStruct((t + db, d), BF16),
                   jax.ShapeDtypeStruct((t + db, LANE), F32)],
        compiler_params=_params("arbitrary"),
        name="stack_cast",
    )(xp, xs, w_lr)


def _in_proj_kernel(x_ref, wt_ref, o_ref, wbf_ref):
    @pl.when(pl.program_id(1) == 0)
    def _cast_weight_tile():
        wbf_ref[...] = wt_ref[...].astype(BF16)

    o_ref[...] = lax.dot_general(x_ref[...], wbf_ref[...], _NT, preferred_element_type=F32)


def _in_proj(x, wt, n_out, tm, tn):
    m, k = x.shape
    assert n_out % tn == 0 and m % tm == 0 and wt.shape[0] >= n_out
    return pl.pallas_call(
        _in_proj_kernel,
        grid=(n_out // tn, m // tm),
        in_specs=[pl.BlockSpec((tm, k), lambda j, i: (i, 0)),
                  pl.BlockSpec((tn, k), lambda j, i: (j, 0))],
        out_specs=pl.BlockSpec((tm, tn), lambda j, i: (i, j)),
        out_shape=jax.ShapeDtypeStruct((m, n_out), F32),
        scratch_shapes=[pltpu.VMEM((tn, k), BF16)],
        compiler_params=_params("arbitrary", "arbitrary"),
        name="in_proj",
    )(x, wt)


def _kv_rows_kernel(k_ref, v_ref, ko_ref, vo_ref):
    tok = k_ref.shape[0]
    for h in range(MOBA_KV_HEADS):
        cols = slice(MOBA_HD * h, MOBA_HD * (h + 1))
        ko_ref[pl.ds(h, tok, stride=MOBA_KV_HEADS), :] = k_ref[:, cols]
        vo_ref[pl.ds(h, tok, stride=MOBA_KV_HEADS), :] = v_ref[:, cols]


def _kv_rows(proj, t, tr):
    assert t % tr == 0
    out = jax.ShapeDtypeStruct((t * MOBA_KV_HEADS, MOBA_HD), F32)
    return pl.pallas_call(
        _kv_rows_kernel,
        grid=(t // tr,),
        in_specs=[pl.BlockSpec((tr, MOBA_KVW), lambda i: (i, OFF_MK // MOBA_KVW)),
                  pl.BlockSpec((tr, MOBA_KVW), lambda i: (i, OFF_MV // MOBA_KVW))],
        out_specs=[pl.BlockSpec((tr * MOBA_KV_HEADS, MOBA_HD), lambda i: (i, 0)),
                   pl.BlockSpec((tr * MOBA_KV_HEADS, MOBA_HD), lambda i: (i, 0))],
        out_shape=[out, out],
        compiler_params=_params("arbitrary"),
        name="kv_rows",
    )(proj, proj)


def _log_decay_kernel(ga_ref, wa2_ref, ba_ref, o_ref, *, cumulative):
    ga_hi, ga_lo = _split_bf16(ga_ref[...])
    wa_hi, wa_lo = _split_bf16(wa2_ref[...])
    z = (jnp.dot(ga_hi, wa_hi, preferred_element_type=F32)
         + (jnp.dot(ga_lo, wa_hi, preferred_element_type=F32)
            + jnp.dot(ga_hi, wa_lo, preferred_element_type=F32))) + ba_ref[...]
    log_a = (jnp.minimum(z, 0.0) - jnp.log(1.0 + jnp.exp(-jnp.abs(z)))) * (1.0 / GLA_GATE_TAU)
    if cumulative:
        pos = lax.broadcasted_iota(jnp.int32, log_a.shape, 0) % GLA_SUB
        step = 1
        while step < GLA_SUB:
            log_a = log_a + jnp.where(pos >= step, pltpu.roll(log_a, step, axis=0), 0.0)
            step *= 2
    o_ref[...] = log_a


def _log_decay(ga, w_a2p, b_a, row0, n_rows, tm, cumulative):
    assert row0 % tm == 0 and n_rows % tm == 0 and tm % GLA_SUB == 0
    t0 = row0 // tm
    return pl.pallas_call(
        functools.partial(_log_decay_kernel, cumulative=cumulative),
        grid=(n_rows // tm,),
        in_specs=[pl.BlockSpec((tm, LANE), lambda i: (t0 + i, 0)),
                  pl.BlockSpec((LANE, GLA_KW), lambda i: (0, 0)),
                  pl.BlockSpec((1, GLA_KW), lambda i: (0, 0))],
        out_specs=pl.BlockSpec((tm, GLA_KW), lambda i: (i, 0)),
        out_shape=jax.ShapeDtypeStruct((n_rows, GLA_KW), F32),
        compiler_params=_params("arbitrary"),
        name="gla_log_decay",
    )(ga, w_a2p, b_a)


def _topk_bias(gate, n_valid, n_rows):
    blk = lax.broadcasted_iota(jnp.int32, gate.shape, 0)
    cur = jnp.where(blk < n_valid, gate, NEG_INF)
    sel = jnp.zeros(gate.shape, jnp.bool_)
    for _ in range(MOBA_TOPK):
        mx = jnp.max(cur, axis=0, keepdims=True)
        hit = jnp.logical_and(cur == mx, mx > NEG_INF)
        first = jnp.min(jnp.where(hit, blk, n_rows), axis=0, keepdims=True)
        pick = blk == first
        sel = jnp.logical_or(sel, pick)
        cur = jnp.where(pick, NEG_INF, cur)
    return jnp.where(sel, 0.0, MASK_BIAS)


_MOBA_KV_PER_STEP = 2


def _moba_prompt_kernel(q_ref, k_ref, v_ref, g_ref, o_ref,
                        kaug_ref, vt_ref, kmean_ref, qaug_ref, m_ref, acc_ref, sa_ref, sb_ref,
                        *, n_blocks):
    qi = pl.program_id(1)
    blk_sz = MOBA_BLOCK
    hd = MOBA_HD
    rows = MOBA_GROUP * blk_sz
    gw = MOBA_GROUP * hd
    chains = range(_MOBA_KV_PER_STEP)

    @pl.when(qi == 0)
    def _prepare_kv_heads():
        kmean_ref[...] = jnp.zeros(kmean_ref.shape, F32)
        vt_ref[:, hd:, :] = jnp.ones((vt_ref.shape[0], vt_ref.shape[1] - hd, vt_ref.shape[2]), BF16)
        lane_blk = lax.broadcasted_iota(jnp.int32, (blk_sz, LANE), 1)

        def body(j, carry):
            r0 = pl.multiple_of(j * blk_sz, blk_sz)
            for c in chains:
                kb = k_ref[pl.ds(r0, blk_sz), hd * c:hd * (c + 1)]
                kaug_ref[c, pl.ds(r0, blk_sz), 0:hd] = kb.astype(BF16)
                kaug_ref[c, pl.ds(r0, blk_sz), hd:hd + LANE] = jnp.where(lane_blk == j, 1.0, 0.0).astype(BF16)
                kmean_ref[c, pl.ds(j, 1), :] = jnp.sum(kb, axis=0, keepdims=True) * (1.0 / blk_sz)
                vt_ref[c, 0:hd, pl.ds(r0, blk_sz)] = v_ref[pl.ds(r0, blk_sz), hd * c:hd * (c + 1)].T.astype(BF16)
            return carry

        lax.fori_loop(0, n_blocks, body, 0)

    nb_pad = kmean_ref.shape[1]
    for c in chains:
        q = q_ref[:, gw * c:gw * (c + 1)]
        qs_t = jnp.concatenate([q[:, hd * g:hd * (g + 1)].T for g in range(MOBA_GROUP)], axis=1)
        gate = jnp.dot(kmean_ref[c], qs_t, precision=lax.Precision.HIGHEST,
                       preferred_element_type=F32)
        bias = _topk_bias(gate, qi, nb_pad)
        qaug_ref[c, 0:hd, :] = (qs_t * (hd ** -0.5 * LOG2E)).astype(BF16)
        qaug_ref[c, hd:hd + nb_pad, :] = bias.astype(BF16)
        qaug_ref[c, hd + nb_pad:, :] = jnp.full((LANE - nb_pad, rows), MASK_BIAS, BF16)

    def scores(c, r0):
        return jnp.dot(kaug_ref[c, pl.ds(r0, blk_sz), :], qaug_ref[c],
                       preferred_element_type=F32)

    def accumulate(c, s, r0):
        m_prev = m_ref[c]
        m_new = jnp.maximum(m_prev, jnp.max(s, axis=0, keepdims=True))
        p = jnp.exp2(s - m_new).astype(BF16)
        pv = jnp.dot(vt_ref[c, :, pl.ds(r0, blk_sz)], p, preferred_element_type=F32)
        acc_ref[c] = jnp.exp2(m_prev - m_new) * acc_ref[c] + pv
        m_ref[c] = m_new

    for c in chains:
        sa_ref[c] = scores(c, 0)

    r_own = pl.multiple_of(qi * blk_sz, blk_sz)
    key_t = lax.broadcasted_iota(jnp.int32, (blk_sz, rows), 0)
    row_t = lax.broadcasted_iota(jnp.int32, (blk_sz, rows), 1) % blk_sz
    for c in chains:
        s = jnp.dot(kaug_ref[c, pl.ds(r_own, blk_sz), 0:hd], qaug_ref[c, 0:hd, :],
                    preferred_element_type=F32)
        s = jnp.where(key_t <= row_t, s, NEG_INF)
        m0 = jnp.max(s, axis=0, keepdims=True)
        m_ref[c] = m0
        acc_ref[c] = jnp.dot(vt_ref[c, :, pl.ds(r_own, blk_sz)], jnp.exp2(s - m0).astype(BF16),
                             preferred_element_type=F32)

    n_pairs = (qi + 1) // 2

    def past_pair(jj, carry):
        r0 = pl.multiple_of(jj * (2 * blk_sz), 2 * blk_sz)
        r1 = pl.multiple_of(r0 + blk_sz, blk_sz)
        r2 = pl.multiple_of(jnp.minimum(r0 + 2 * blk_sz, (n_blocks - 1) * blk_sz), blk_sz)
        for c in chains:
            sb_ref[c] = scores(c, r1)
        for c in chains:
            accumulate(c, sa_ref[c], r0)
        for c in chains:
            sa_ref[c] = scores(c, r2)
        for c in chains:
            accumulate(c, sb_ref[c], r1)
        return carry

    lax.fori_loop(0, n_pairs, past_pair, 0)

    for c in chains:
        acc = acc_ref[c]
        o = (acc[0:hd, :] * (1.0 / acc[hd:hd + 1, :])).T
        o = jnp.concatenate([o[blk_sz * g:blk_sz * (g + 1), :] for g in range(MOBA_GROUP)], axis=1)
        o_ref[:, gw * c:gw * (c + 1)] = (o * _silu(g_ref[:, gw * c:gw * (c + 1)])).astype(o_ref.dtype)


def _moba_prompt(proj, t):
    assert t % (2 * MOBA_BLOCK) == 0
    nb = t // MOBA_BLOCK
    nb_pad = -(-nb // BF16_SUBLANES) * BF16_SUBLANES
    assert nb_pad <= LANE
    rows = MOBA_GROUP * MOBA_BLOCK
    nc = _MOBA_KV_PER_STEP
    gw = MOBA_GROUP * MOBA_HD * nc
    kvw = MOBA_HD * nc
    assert MOBA_KV_HEADS % nc == 0
    assert OFF_MQ % gw == 0 and OFF_MG % gw == 0 and OFF_MK % kvw == 0 and OFF_MV % kvw == 0
    kv_spec = functools.partial(pl.BlockSpec, (t, kvw), pipeline_mode=pl.Buffered(1))
    return pl.pallas_call(
        functools.partial(_moba_prompt_kernel, n_blocks=nb),
        grid=(MOBA_KV_HEADS // nc, nb),
        in_specs=[pl.BlockSpec((MOBA_BLOCK, gw), lambda h, i: (i, OFF_MQ // gw + h)),
                  kv_spec(lambda h, i: (0, OFF_MK // kvw + h)),
                  kv_spec(lambda h, i: (0, OFF_MV // kvw + h)),
                  pl.BlockSpec((MOBA_BLOCK, gw), lambda h, i: (i, OFF_MG // gw + h))],
        out_specs=pl.BlockSpec((MOBA_BLOCK, gw), lambda h, i: (i, h)),
        out_shape=jax.ShapeDtypeStruct((t, MOBA_QW), BF16),
        scratch_shapes=[pltpu.VMEM((nc, t, MOBA_HD + LANE), BF16),
                        pltpu.VMEM((nc, MOBA_HD + BF16_SUBLANES, t), BF16),
                        pltpu.VMEM((nc, nb_pad, MOBA_HD), F32),
                        pltpu.VMEM((nc, MOBA_HD + LANE, rows), BF16),
                        pltpu.VMEM((nc, 1, rows), F32),
                        pltpu.VMEM((nc, MOBA_HD + BF16_SUBLANES, rows), F32),
                        pltpu.VMEM((nc, MOBA_BLOCK, rows), F32),
                        pltpu.VMEM((nc, MOBA_BLOCK, rows), F32)],
        compiler_params=_params("arbitrary", "arbitrary"),
        name="moba_prompt",
    )(proj, proj, proj, proj)


_GLA_GROUP = 8
_GLA_PAIRS_PER_STEP = 8
_GLA_PAIRS_PER_VCHUNK = 4


def _gla_prompt_kernel(q_ref, k_ref, b_ref, *rest, n_tblocks, tb):
    ti = pl.program_id(1)
    sub = GLA_SUB
    kw = 2 * GLA_DK
    vw = 2 * GLA_DV
    npp = _GLA_PAIRS_PER_STEP
    nvc = npp // _GLA_PAIRS_PER_VCHUNK
    v_refs, gg_refs = rest[:nvc], rest[nvc:2 * nvc]
    gain_ref, o_ref, sout_ref, s_ref, oacc_ref = rest[2 * nvc:]

    @pl.when(ti == 0)
    def _zero_state():
        s_ref[...] = jnp.zeros(s_ref.shape, F32)

    row_h = lax.broadcasted_iota(jnp.int32, (kw, vw), 0) // GLA_DK
    col_h = lax.broadcasted_iota(jnp.int32, (kw, vw), 1) // GLA_DV
    same_head = row_h == col_h
    head_sum = jnp.where(same_head, 1.0, 0.0).astype(BF16)
    j_idx = lax.broadcasted_iota(jnp.int32, (sub, kw), 0)
    grp = _GLA_GROUP
    lane_head = lax.broadcasted_iota(jnp.int32, (grp * sub, kw), 1) // GLA_DK
    gt = grp * sub
    half = sub // 2
    sub_rows = half * half + half * sub
    rs_in = lax.broadcasted_iota(jnp.int32, (sub, sub_rows), 1)
    rs_tok = jnp.where(rs_in < half * half, rs_in // half, half + (rs_in - half * half) // sub)
    row_sum = jnp.where(lax.broadcasted_iota(jnp.int32, (sub, sub_rows), 0) == rs_tok,
                        1.0, 0.0).astype(BF16)

    def pair_group(q, k, b, v, state):
        w_rows, v_rows = [], []
        for s in range(grp):
            lo = s * sub
            b_s, k_s, v_s = b[lo:lo + sub, :], k[lo:lo + sub, :], v[lo:lo + sub, :]
            for i in range(sub):
                nj = half if i < half else sub
                d = b_s[i:i + 1, :] - b_s[0:nj, :]
                dec = jnp.exp(jnp.where(j_idx[0:nj, :] < i, d, NEG_INF))
                w_rows.append(dec * k_s[0:nj, :] * q[lo + i:lo + i + 1, :])
                v_rows.append(v_s[0:nj, :])
        w = jnp.concatenate(w_rows, axis=0).astype(BF16)
        sc = jnp.dot(w, head_sum, preferred_element_type=F32)
        z = (sc * jnp.concatenate(v_rows, axis=0)).astype(BF16)
        o_diag = jnp.concatenate(
            [jnp.dot(row_sum, z[sub_rows * s:sub_rows * (s + 1), :], preferred_element_type=F32)
             for s in range(grp)], axis=0)
        qk = q * k
        o_self = jnp.concatenate(
            [jnp.sum(jnp.where(lane_head == h, qk, 0.0), axis=1, keepdims=True)
             * v[:, GLA_DV * h:GLA_DV * (h + 1)] for h in range(2)], axis=1)
        o_diag = o_diag + o_self
        q_dec = (q * jnp.exp(b)).astype(BF16)
        upds, decays = [], []
        for s in range(grp):
            lo = s * sub
            b_s, k_s, v_s = b[lo:lo + sub, :], k[lo:lo + sub, :], v[lo:lo + sub, :]
            b_last = b_s[sub - 1:sub, :]
            k_dec = (k_s * jnp.exp(b_last - b_s)).astype(BF16)
            upd = lax.dot_general(k_dec, v_s.astype(BF16), _TN, preferred_element_type=F32)
            upds.append(jnp.where(same_head, upd, 0.0))
            e_col = jnp.broadcast_to(jnp.exp(b_last), (kw, kw)).T
            decays.append(jnp.concatenate([e_col, e_col], axis=1))
        o_inter = []
        for s in range(grp):
            lo = s * sub
            o_inter.append(jnp.dot(q_dec[lo:lo + sub, :], state.astype(BF16),
                                   preferred_element_type=F32))
            state = state * decays[s] + upds[s]
        return jnp.concatenate(o_inter, axis=0) + o_diag, state

    def group(g, carry):
        r0 = pl.multiple_of(g * gt, gt)
        for pp in range(npp):
            ks, vs = slice(kw * pp, kw * (pp + 1)), slice(vw * pp, vw * (pp + 1))
            pc = pp % _GLA_PAIRS_PER_VCHUNK
            v_pair = v_refs[pp // _GLA_PAIRS_PER_VCHUNK][pl.ds(r0, gt), vw * pc:vw * (pc + 1)]
            o, state = pair_group(q_ref[pl.ds(r0, gt), ks] * (GLA_DK ** -0.5), k_ref[pl.ds(r0, gt), ks],
                                  b_ref[pl.ds(r0, gt), ks], v_pair, s_ref[pp])
            s_ref[pp] = state
            oacc_ref[pl.ds(r0, gt), vs] = o
        return carry

    lax.fori_loop(0, tb // gt, group, 0)

    o = oacc_ref[...]
    gain = gain_ref[...]
    heads = []
    for h in range(2 * npp):
        oh = o[:, GLA_DV * h:GLA_DV * (h + 1)]
        ms = jnp.mean(oh * oh, axis=-1, keepdims=True)
        heads.append(oh * lax.rsqrt(ms + NORM_EPS) * gain)
    o = jnp.concatenate(heads, axis=1)
    gate = jnp.concatenate([r[...] for r in gg_refs], axis=1)
    o_ref[...] = (o * _silu(gate)).astype(o_ref.dtype)

    @pl.when(ti == n_tblocks - 1)
    def _emit_state():
        for pp in range(npp):
            st = s_ref[pp]
            sout_ref[2 * pp] = st[0:GLA_DK, 0:GLA_DV]
            sout_ref[2 * pp + 1] = st[GLA_DK:2 * GLA_DK, GLA_DV:2 * GLA_DV]


def _gla_prompt(proj, decay, gain, t):
    tb = 256 if t % 256 == 0 else t
    assert t % tb == 0 and tb % (GLA_SUB * _GLA_GROUP) == 0
    nt = t // tb
    npp = _GLA_PAIRS_PER_STEP
    kw, vw = 2 * GLA_DK * npp, 2 * GLA_DV * npp
    vcw = 2 * GLA_DV * _GLA_PAIRS_PER_VCHUNK
    nvc = npp // _GLA_PAIRS_PER_VCHUNK
    assert GLA_PAIRS % npp == 0 and npp % _GLA_PAIRS_PER_VCHUNK == 0
    assert OFF_GQ % kw == 0 and OFF_GK % kw == 0 and OFF_GV % vcw == 0 and OFF_GG % vcw == 0

    def vchunk(off, c):
        return pl.BlockSpec((tb, vcw), lambda p, i: (i, off // vcw + p * nvc + c))

    return pl.pallas_call(
        functools.partial(_gla_prompt_kernel, n_tblocks=nt, tb=tb),
        grid=(GLA_PAIRS // npp, nt),
        in_specs=[pl.BlockSpec((tb, kw), lambda p, i: (i, OFF_GQ // kw + p)),
                  pl.BlockSpec((tb, kw), lambda p, i: (i, OFF_GK // kw + p)),
                  pl.BlockSpec((tb, kw), lambda p, i: (i, p))]
                 + [vchunk(OFF_GV, c) for c in range(nvc)]
                 + [vchunk(OFF_GG, c) for c in range(nvc)]
                 + [pl.BlockSpec((1, GLA_DV), lambda p, i: (0, 0))],
        out_specs=[pl.BlockSpec((tb, vw), lambda p, i: (i, p)),
                   pl.BlockSpec((2 * npp, GLA_DK, GLA_DV), lambda p, i: (p, 0, 0))],
        out_shape=[jax.ShapeDtypeStruct((t, GLA_VW), BF16),
                   jax.ShapeDtypeStruct((GLA_HEADS, GLA_DK, GLA_DV), F32)],
        scratch_shapes=[pltpu.VMEM((npp, 2 * GLA_DK, 2 * GLA_DV), F32),
                        pltpu.VMEM((tb, vw), F32)],
        compiler_params=_params("arbitrary", "arbitrary"),
        name="gla_prompt",
    )(proj, proj, decay, *([proj] * (2 * nvc)), gain)


def _out_proj_kernel(og_ref, om_ref, wg_ref, wm_ref, x_ref, lg_ref, lb_ref, o_ref,
                     *, n_ctiles, tn, alpha):
    j = pl.program_id(1)
    mixed = jnp.dot(og_ref[...], wg_ref[...], preferred_element_type=F32)
    mixed = mixed + jnp.dot(om_ref[...], wm_ref[...], preferred_element_type=F32)
    c0 = pl.multiple_of(j * tn, tn)
    o_ref[:, pl.ds(c0, tn)] = alpha * x_ref[...] + mixed

    @pl.when(j == n_ctiles - 1)
    def _layer_norm():
        h = o_ref[...]
        mu = jnp.mean(h, axis=-1, keepdims=True)
        hc = h - mu
        var = jnp.mean(hc * hc, axis=-1, keepdims=True)
        o_ref[...] = hc * lax.rsqrt(var + NORM_EPS) * lg_ref[...] + lb_ref[...]


def _out_proj(og, om, w_out_bf, x, ln_g, ln_b, alpha, tm, tn):
    m, d = x.shape
    kg, km = og.shape[1], om.shape[1]
    assert kg == km and w_out_bf.shape[0] == kg + km
    nct = d // tn
    w_mode = dict(pipeline_mode=pl.Buffered(1)) if nct == 1 else {}
    return pl.pallas_call(
        functools.partial(_out_proj_kernel, n_ctiles=nct, tn=tn, alpha=alpha),
        grid=(m // tm, nct),
        in_specs=[pl.BlockSpec((tm, kg), lambda i, j: (i, 0)),
                  pl.BlockSpec((tm, km), lambda i, j: (i, 0)),
                  pl.BlockSpec((kg, tn), lambda i, j: (0, j), **w_mode),
                  pl.BlockSpec((km, tn), lambda i, j: (1, j), **w_mode),
                  pl.BlockSpec((tm, tn), lambda i, j: (i, j)),
                  pl.BlockSpec((1, d), lambda i, j: (0, 0)),
                  pl.BlockSpec((1, d), lambda i, j: (0, 0))],
        out_specs=pl.BlockSpec((tm, d), lambda i, j: (i, 0)),
        out_shape=jax.ShapeDtypeStruct((m, d), F32),
        compiler_params=_params("arbitrary", "arbitrary"),
        name="out_proj_ln",
    )(og, om, w_out_bf, w_out_bf, x, ln_g, ln_b)


_MOBA_DEC_SEQS = 2


def _moba_decode_kernel(pt_ref, q_ref, kn_ref, vn_ref, g_ref, *rest, n_pages, page):
    del pt_ref
    n_seq = _MOBA_DEC_SEQS
    o_ref = rest[2 * n_seq * n_pages]
    k_refs = [rest[sq * n_pages:(sq + 1) * n_pages] for sq in range(n_seq)]
    v_refs = [rest[(n_seq + sq) * n_pages:(n_seq + sq + 1) * n_pages] for sq in range(n_seq)]
    qs = [q_ref[sq] * (MOBA_HD ** -0.5) for sq in range(n_seq)]
    ss = [_moba_decode_scores(qs[sq], k_refs[sq], n_pages) for sq in range(n_seq)]
    ps = [_moba_decode_select(ss[sq], qs[sq], kn_ref[sq], n_pages, page) for sq in range(n_seq)]
    for sq in range(n_seq):
        pb, p_new, denom = ps[sq]
        o = _moba_decode_values(pb, p_new, denom, vn_ref[sq], v_refs[sq], n_pages, page)
        o_ref[sq] = (o * _silu(g_ref[sq])).astype(o_ref.dtype)


def _moba_decode_scores(q, k_refs, n_pages):
    qb = q.astype(BF16)
    return jnp.concatenate(
        [lax.dot_general(qb, k_refs[p][0, 0].astype(BF16), _NT, preferred_element_type=F32)
         for p in range(n_pages)], axis=1)


def _expand_kv_rows(x):
    return jnp.concatenate([jnp.broadcast_to(x[i:i + 1, :], (MOBA_GROUP, MOBA_HD))
                            for i in range(MOBA_KV_HEADS)], axis=0)


def _moba_decode_select(s, q, kn, n_pages, page):
    heads = MOBA_HEADS
    brow = MOBA_BLOCK * MOBA_KV_HEADS
    n_blk = (n_pages * page) // MOBA_BLOCK
    lane = lax.broadcasted_iota(jnp.int32, s.shape, 1)
    head_kv = lax.broadcasted_iota(jnp.int32, s.shape, 0) // MOBA_GROUP
    own_kv = (lane % MOBA_KV_HEADS) == head_kv
    s_own = jnp.where(own_kv, s, 0.0)

    assert n_blk <= LANE
    blk_lane = lax.broadcasted_iota(jnp.int32, (heads, LANE), 1)
    cur = jnp.full((heads, LANE), NEG_INF, F32)
    for j in range(n_blk):
        mean_j = jnp.sum(s_own[:, brow * j:brow * (j + 1)], axis=1, keepdims=True) * (1.0 / MOBA_BLOCK)
        cur = jnp.where(blk_lane == j, mean_j, cur)
    sel = jnp.zeros((heads, LANE), F32)
    for _ in range(min(MOBA_TOPK, n_blk)):
        mx = jnp.max(cur, axis=1, keepdims=True)
        first = jnp.min(jnp.where(cur == mx, blk_lane, n_blk), axis=1, keepdims=True)
        pick = blk_lane == first
        sel = jnp.where(pick, 1.0, sel)
        cur = jnp.where(pick, NEG_INF, cur)
    bias = jnp.concatenate(
        [jnp.broadcast_to(jnp.where(jnp.sum(jnp.where(blk_lane == j, sel, 0.0), axis=1, keepdims=True) > 0.5,
                                    0.0, NEG_INF), (heads, brow)) for j in range(n_blk)], axis=1)

    s_new = jnp.sum(q * _expand_kv_rows(kn), axis=1, keepdims=True)
    sm = jnp.where(own_kv, s + bias, NEG_INF)
    m = jnp.maximum(jnp.max(sm, axis=1, keepdims=True), s_new)
    p = jnp.exp(sm - m)
    p_new = jnp.exp(s_new - m)
    denom = jnp.sum(p, axis=1, keepdims=True) + p_new
    return p.astype(BF16), p_new, denom


def _moba_decode_values(pb, p_new, denom, vn, v_refs, n_pages, page):
    prow = page * MOBA_KV_HEADS
    o = p_new * _expand_kv_rows(vn)
    for pg in range(n_pages):
        o = o + jnp.dot(pb[:, prow * pg:prow * (pg + 1)], v_refs[pg][0, 0].astype(BF16),
                        preferred_element_type=F32)
    return o * (1.0 / denom)


def _moba_decode(page_table, q3, kn3, vn3, g3, cache_k, cache_v, layer):
    db, n_pages = page_table.shape
    depth, n_pool, page = cache_k.shape[0], cache_k.shape[1], cache_k.shape[2]
    assert (n_pages * page) % MOBA_BLOCK == 0
    prow = page * MOBA_KV_HEADS
    ck = cache_k.reshape(depth, n_pool, prow, MOBA_HD)
    cv = cache_v.reshape(depth, n_pool, prow, MOBA_HD)

    n_seq = _MOBA_DEC_SEQS
    assert db % n_seq == 0

    def page_spec(sq, pg):
        return pl.BlockSpec((1, 1, prow, MOBA_HD), lambda b, pt: (layer, pt[b * n_seq + sq, pg], 0, 0))

    def row_spec(r):
        return pl.BlockSpec((n_seq, r, MOBA_HD), lambda b, pt: (b, 0, 0))

    pages = [page_spec(sq, pg) for sq in range(n_seq) for pg in range(n_pages)]
    grid_spec = pltpu.PrefetchScalarGridSpec(
        num_scalar_prefetch=1,
        grid=(db // n_seq,),
        in_specs=[row_spec(MOBA_HEADS), row_spec(MOBA_KV_HEADS), row_spec(MOBA_KV_HEADS),
                  row_spec(MOBA_HEADS)] + pages + pages,
        out_specs=pl.BlockSpec((n_seq, MOBA_HEADS, MOBA_HD), lambda b, pt: (b, 0, 0)),
    )
    n_ops = n_seq * n_pages
    return pl.pallas_call(
        functools.partial(_moba_decode_kernel, n_pages=n_pages, page=page),
        grid_spec=grid_spec,
        out_shape=jax.ShapeDtypeStruct((db, MOBA_HEADS, MOBA_HD), BF16),
        compiler_params=_params("arbitrary"),
        name="moba_decode",
    )(page_table, q3, kn3, vn3, g3, *([ck] * n_ops), *([cv] * n_ops))


_GLA_DEC_GROUP = 8


def _gla_decode_kernel(q_ref, k_ref, la_ref, v_ref, gg_ref, gain_ref, s_ref, o_ref, sn_ref):
    grp = _GLA_DEC_GROUP

    def columns(x):
        xp = jnp.concatenate([x, jnp.zeros((LANE - grp, x.shape[1]), F32)], axis=0)
        return xp.T

    a_t = columns(jnp.exp(la_ref[...]))
    k_t = columns(k_ref[...])
    q_t = columns(q_ref[...] * (GLA_DK ** -0.5))
    gain = gain_ref[...]
    for i in range(grp):
        rows = []
        for h in range(GLA_HEADS):
            r = slice(GLA_DK * h, GLA_DK * (h + 1))
            v_h = jnp.broadcast_to(v_ref[i, h:h + 1, :], (GLA_DK, GLA_DV))
            s_new = a_t[r, i:i + 1] * s_ref[i, h] + k_t[r, i:i + 1] * v_h
            sn_ref[i, h] = s_new
            rows.append(jnp.sum(q_t[r, i:i + 1] * s_new, axis=0, keepdims=True))
        o = jnp.concatenate(rows, axis=0)
        ms = jnp.mean(o * o, axis=-1, keepdims=True)
        o = o * lax.rsqrt(ms + NORM_EPS) * gain
        o_ref[i] = (o * _silu(gg_ref[i])).astype(o_ref.dtype)


def _gla_decode(q2, k2, la2, v3, gg3, gain, state):
    db = q2.shape[0]
    grp = _GLA_DEC_GROUP
    assert db % grp == 0
    vec = pl.BlockSpec((grp, GLA_KW), lambda g: (g, 0))
    hd3 = pl.BlockSpec((grp, GLA_HEADS, GLA_DV), lambda g: (g, 0, 0))
    st = pl.BlockSpec((grp, GLA_HEADS, GLA_DK, GLA_DV), lambda g: (g, 0, 0, 0))
    return pl.pallas_call(
        _gla_decode_kernel,
        grid=(db // grp,),
        in_specs=[vec, vec, vec, hd3, hd3, pl.BlockSpec((1, GLA_DV), lambda g: (0, 0)), st],
        out_specs=[hd3, st],
        out_shape=[jax.ShapeDtypeStruct((db, GLA_HEADS, GLA_DV), BF16),
                   jax.ShapeDtypeStruct(state.shape, F32)],
        compiler_params=_params("arbitrary"),
        name="gla_decode",
    )(q2, k2, la2, v3, gg3, gain, state)


def _tile(n, prefs):
    for p in prefs:
        if n % p == 0:
            return p
    return n


def kernel(x_prompt, x_sample, cache_k, cache_v, state_gla, page_table,
           w_in, w_a2, b_a, gla_gain, w_out, ln_g, ln_b):
    bsz, t, d = x_prompt.shape
    db, dt, _ = x_sample.shape
    depth = w_in.shape[0]
    assert bsz == 1 and dt == 1
    alpha = (2.0 * depth) ** 0.25
    y_p = x_prompt.reshape(t, d)
    y_s = x_sample.reshape(db, d)
    kp_l, vp_l, sp_l, ks_l, vs_l, ss_l = [], [], [], [], [], []
    for l in range(depth):
        w_in_t = jnp.swapaxes(w_in[l], 0, 1)
        w_lr = jnp.zeros((d, LANE), BF16).at[:, :GLA_LOWRANK].set(w_in[l][:, OFF_GA:].astype(BF16))
        w_a2p = jnp.zeros((LANE, GLA_KW), F32).at[:GLA_LOWRANK, :].set(w_a2[l])
        w_out_bf = w_out[l].astype(BF16)
        gain = gla_gain[l].reshape(1, GLA_DV)
        lg, lb = ln_g[l].reshape(1, d), ln_b[l].reshape(1, d)
        x_all, ga = _stack_cast(y_p, y_s, w_lr, _tile(math.gcd(t, db), (128,)))
        m_all = t + db
        proj = _in_proj(x_all, w_in_t, MAIN_W, _tile(m_all, (640, 512, 256, 128)), 1024)
        ba2 = b_a[l].reshape(1, GLA_KW)
        decay = _log_decay(ga, w_a2p, ba2, 0, t, _tile(math.gcd(t, 512), (512,)), True)
        log_a_s = _log_decay(ga, w_a2p, ba2, t, db, db, False)
        o_m = _moba_prompt(proj, t)
        o_g, s_p = _gla_prompt(proj, decay, gain, t)
        k_rows, v_rows = _kv_rows(proj, t, _tile(t, (512, 256, 128)))
        kp_l.append(k_rows.reshape(bsz, t, MOBA_KV_HEADS, MOBA_HD))
        vp_l.append(v_rows.reshape(bsz, t, MOBA_KV_HEADS, MOBA_HD))
        sp_l.append(s_p.reshape(bsz, GLA_HEADS, GLA_DK, GLA_DV))
        y_p_new = _out_proj(o_g, o_m, w_out_bf, y_p, lg, lb, alpha, _tile(t, (256, 128)), d)
        proj_s = proj[t:]
        k_new = proj_s[:, OFF_MK:OFF_MV].reshape(db, MOBA_KV_HEADS, MOBA_HD)
        v_new = proj_s[:, OFF_MV:OFF_MG].reshape(db, MOBA_KV_HEADS, MOBA_HD)
        o_m_s = _moba_decode(page_table,
                             proj_s[:, OFF_MQ:OFF_MK].reshape(db, MOBA_HEADS, MOBA_HD),
                             k_new, v_new,
                             proj_s[:, OFF_MG:OFF_GQ].reshape(db, MOBA_HEADS, MOBA_HD),
                             cache_k, cache_v, l)
        o_g_s, s_s = _gla_decode(proj_s[:, OFF_GQ:OFF_GK], proj_s[:, OFF_GK:OFF_GV], log_a_s,
                                 proj_s[:, OFF_GV:OFF_GG].reshape(db, GLA_HEADS, GLA_DV),
                                 proj_s[:, OFF_GG:OFF_GA].reshape(db, GLA_HEADS, GLA_DV),
                                 gain, state_gla[l])
        ks_l.append(k_new.reshape(db, dt, MOBA_KV_HEADS, MOBA_HD))
        vs_l.append(v_new.reshape(db, dt, MOBA_KV_HEADS, MOBA_HD))
        ss_l.append(s_s)
        y_s = _out_proj(o_g_s.reshape(db, GLA_VW), o_m_s.reshape(db, MOBA_QW), w_out_bf, y_s, lg, lb,
                        alpha, _tile(db, (128,)), _tile(d, (1024, 512, 256, 128)))
        y_p = y_p_new
    return (y_p.reshape(bsz, t, d), y_s.reshape(db, dt, d),
            jnp.stack(kp_l), jnp.stack(vp_l), jnp.stack(sp_l),
            jnp.stack(ks_l), jnp.stack(vs_l), jnp.stack(ss_l))
```

```python
import functools
import math

import jax
import jax.numpy as jnp
from jax import lax
from jax.experimental import pallas as pl
from jax.experimental.pallas import tpu as pltpu

F32 = jnp.float32
BF16 = jnp.bfloat16
NEG_INF = float("-inf")
MASK_BIAS = -1e30
LOG2E = 1.4426950408889634

MOBA_HEADS = 16
MOBA_KV_HEADS = 4
MOBA_GROUP = MOBA_HEADS // MOBA_KV_HEADS
MOBA_HD = 128
MOBA_BLOCK = 256
MOBA_TOPK = 3
GLA_HEADS = 16
GLA_DK = 64
GLA_DV = 128
GLA_LOWRANK = 16
GLA_GATE_TAU = 16.0
GLA_SUB = 16
GLA_PAIRS = GLA_HEADS // 2
NORM_EPS = 1e-5

MOBA_QW = MOBA_HEADS * MOBA_HD
MOBA_KVW = MOBA_KV_HEADS * MOBA_HD
GLA_KW = GLA_HEADS * GLA_DK
GLA_VW = GLA_HEADS * GLA_DV
OFF_MQ = 0
OFF_MK = OFF_MQ + MOBA_QW
OFF_MV = OFF_MK + MOBA_KVW
OFF_MG = OFF_MV + MOBA_KVW
OFF_GQ = OFF_MG + MOBA_QW
OFF_GK = OFF_GQ + GLA_KW
OFF_GV = OFF_GK + GLA_KW
OFF_GG = OFF_GV + GLA_VW
OFF_GA = OFF_GG + GLA_VW
MAIN_W = OFF_GA

LANE = 128
BF16_SUBLANES = 16
VMEM_LIMIT_BYTES = 60 * 1024 * 1024

_NT = (((1,), (1,)), ((), ()))
_TN = (((0,), (0,)), ((), ()))


def _params(*sem):
    return pltpu.CompilerParams(dimension_semantics=sem, vmem_limit_bytes=VMEM_LIMIT_BYTES)


def _silu(x):
    return x * (1.0 / (1.0 + jnp.exp(-x)))


def _split_bf16(x):
    hi = x.astype(BF16)
    return hi, (x - hi.astype(F32)).astype(BF16)


def _stack_cast_kernel(xp_ref, xs_ref, wlr_ref, o_ref, ga_ref, *, n_prompt_tiles):
    i = pl.program_id(0)

    def emit(x):
        xb = x.astype(o_ref.dtype)
        o_ref[...] = xb
        ga_ref[...] = jnp.dot(xb, wlr_ref[...], preferred_element_type=F32)

    @pl.when(i < n_prompt_tiles)
    def _prompt_rows():
        emit(xp_ref[...])

    @pl.when(i >= n_prompt_tiles)
    def _decode_rows():
        emit(xs_ref[...])


def _stack_cast(xp, xs, w_lr, tr):
    (t, d), db = xp.shape, xs.shape[0]
    assert t % tr == 0 and db % tr == 0
    npt, nst = t // tr, db // tr
    return pl.pallas_call(
        functools.partial(_stack_cast_kernel, n_prompt_tiles=npt),
        grid=(npt + nst,),
        in_specs=[pl.BlockSpec((tr, d), lambda i: (jnp.minimum(i, npt - 1), 0)),
                  pl.BlockSpec((tr, d), lambda i: (jnp.maximum(i - npt, 0), 0)),
                  pl.BlockSpec((d, LANE), lambda i: (0, 0))],
        out_specs=[pl.BlockSpec((tr, d), lambda i: (i, 0)),
                   pl.BlockSpec((tr, LANE), lambda i: (i, 0))],
        out_shape=[jax.ShapeDtypeStruct((t + db, d), BF16),
                   jax.ShapeDtypeStruct((t + db, LANE), F32)],
        compiler_params=_params("arbitrary"),
        name="stack_cast",
    )(xp, xs, w_lr)


def _in_proj_kernel(x_ref, wt_ref, o_ref, wbf_ref):
    @pl.when(pl.program_id(1) == 0)
    def _cast_weight_tile():
        wbf_ref[...] = wt_ref[...].astype(BF16)

    o_ref[...] = lax.dot_general(x_ref[...], wbf_ref[...], _NT, preferred_element_type=F32)


def _in_proj(x, wt, n_out, tm, tn):
    m, k = x.shape
    assert n_out % tn == 0 and m % tm == 0 and wt.shape[0] >= n_out
    return pl.pallas_call(
        _in_proj_kernel,
        grid=(n_out // tn, m // tm),
        in_specs=[pl.BlockSpec((tm, k), lambda j, i: (i, 0)),
                  pl.BlockSpec((tn, k), lambda j, i: (j, 0))],
        out_specs=pl.BlockSpec((tm, tn), lambda j, i: (i, j)),
        out_shape=jax.ShapeDtypeStruct((m, n_out), F32),
        scratch_shapes=[pltpu.VMEM((tn, k), BF16)],
        compiler_params=_params("arbitrary", "arbitrary"),
        name="in_proj",
    )(x, wt)


def _kv_rows_kernel(k_ref, v_ref, ko_ref, vo_ref):
    tok = k_ref.shape[0]
    for h in range(MOBA_KV_HEADS):
        cols = slice(MOBA_HD * h, MOBA_HD * (h + 1))
        ko_ref[pl.ds(h, tok, stride=MOBA_KV_HEADS), :] = k_ref[:, cols]
        vo_ref[pl.ds(h, tok, stride=MOBA_KV_HEADS), :] = v_ref[:, cols]


def _kv_rows(proj, t, tr):
    assert t % tr == 0
    out = jax.ShapeDtypeStruct((t * MOBA_KV_HEADS, MOBA_HD), F32)
    return pl.pallas_call(
        _kv_rows_kernel,
        grid=(t // tr,),
        in_specs=[pl.BlockSpec((tr, MOBA_KVW), lambda i: (i, OFF_MK // MOBA_KVW)),
                  pl.BlockSpec((tr, MOBA_KVW), lambda i: (i, OFF_MV // MOBA_KVW))],
        out_specs=[pl.BlockSpec((tr * MOBA_KV_HEADS, MOBA_HD), lambda i: (i, 0)),
                   pl.BlockSpec((tr * MOBA_KV_HEADS, MOBA_HD), lambda i: (i, 0))],
        out_shape=[out, out],
        compiler_params=_params("arbitrary"),
        name="kv_rows",
    )(proj, proj)


def _log_decay_kernel(ga_ref, wa2_ref, ba_ref, o_ref, *, cumulative):
    ga_hi, ga_lo = _split_bf16(ga_ref[...])
    wa_hi, wa_lo = _split_bf16(wa2_ref[...])
    z = (jnp.dot(ga_hi, wa_hi, preferred_element_type=F32)
         + (jnp.dot(ga_lo, wa_hi, preferred_element_type=F32)
            + jnp.dot(ga_hi, wa_lo, preferred_element_type=F32))) + ba_ref[...]
    log_a = (jnp.minimum(z, 0.0) - jnp.log(1.0 + jnp.exp(-jnp.abs(z)))) * (1.0 / GLA_GATE_TAU)
    if cumulative:
        pos = lax.broadcasted_iota(jnp.int32, log_a.shape, 0) % GLA_SUB
        step = 1
        while step < GLA_SUB:
            log_a = log_a + jnp.where(pos >= step, pltpu.roll(log_a, step, axis=0), 0.0)
            step *= 2
    o_ref[...] = log_a


def _log_decay(ga, w_a2p, b_a, row0, n_rows, tm, cumulative):
    assert row0 % tm == 0 and n_rows % tm == 0 and tm % GLA_SUB == 0
    t0 = row0 // tm
    return pl.pallas_call(
        functools.partial(_log_decay_kernel, cumulative=cumulative),
        grid=(n_rows // tm,),
        in_specs=[pl.BlockSpec((tm, LANE), lambda i: (t0 + i, 0)),
                  pl.BlockSpec((LANE, GLA_KW), lambda i: (0, 0)),
                  pl.BlockSpec((1, GLA_KW), lambda i: (0, 0))],
        out_specs=pl.BlockSpec((tm, GLA_KW), lambda i: (i, 0)),
        out_shape=jax.ShapeDtypeStruct((n_rows, GLA_KW), F32),
        compiler_params=_params("arbitrary"),
        name="gla_log_decay",
    )(ga, w_a2p, b_a)


def _topk_bias(gate, n_valid, n_rows):
    blk = lax.broadcasted_iota(jnp.int32, gate.shape, 0)
    cur = jnp.where(blk < n_valid, gate, NEG_INF)
    sel = jnp.zeros(gate.shape, jnp.bool_)
    for _ in range(MOBA_TOPK):
        mx = jnp.max(cur, axis=0, keepdims=True)
        hit = jnp.logical_and(cur == mx, mx > NEG_INF)
        first = jnp.min(jnp.where(hit, blk, n_rows), axis=0, keepdims=True)
        pick = blk == first
        sel = jnp.logical_or(sel, pick)
        cur = jnp.where(pick, NEG_INF, cur)
    return jnp.where(sel, 0.0, MASK_BIAS)


_MOBA_KV_PER_STEP = 2


def _moba_prompt_kernel(q_ref, k_ref, v_ref, g_ref, o_ref,
                        kaug_ref, vt_ref, kmean_ref, qaug_ref, m_ref, acc_ref, sa_ref, sb_ref,
                        *, n_blocks):
    qi = pl.program_id(1)
    blk_sz = MOBA_BLOCK
    hd = MOBA_HD
    rows = MOBA_GROUP * blk_sz
    gw = MOBA_GROUP * hd
    chains = range(_MOBA_KV_PER_STEP)

    @pl.when(qi == 0)
    def _prepare_kv_heads():
        kmean_ref[...] = jnp.zeros(kmean_ref.shape, F32)
        vt_ref[:, hd:, :] = jnp.ones((vt_ref.shape[0], vt_ref.shape[1] - hd, vt_ref.shape[2]), BF16)
        lane_blk = lax.broadcasted_iota(jnp.int32, (blk_sz, LANE), 1)

        def body(j, carry):
            r0 = pl.multiple_of(j * blk_sz, blk_sz)
            for c in chains:
                kb = k_ref[pl.ds(r0, blk_sz), hd * c:hd * (c + 1)]
                kaug_ref[c, pl.ds(r0, blk_sz), 0:hd] = kb.astype(BF16)
                kaug_ref[c, pl.ds(r0, blk_sz), hd:hd + LANE] = jnp.where(lane_blk == j, 1.0, 0.0).astype(BF16)
                kmean_ref[c, pl.ds(j, 1), :] = jnp.sum(kb, axis=0, keepdims=True) * (1.0 / blk_sz)
                vt_ref[c, 0:hd, pl.ds(r0, blk_sz)] = v_ref[pl.ds(r0, blk_sz), hd * c:hd * (c + 1)].T.astype(BF16)
            return carry

        lax.fori_loop(0, n_blocks, body, 0)

    nb_pad = kmean_ref.shape[1]
    for c in chains:
        q = q_ref[:, gw * c:gw * (c + 1)]
        qs_t = jnp.concatenate([q[:, hd * g:hd * (g + 1)].T for g in range(MOBA_GROUP)], axis=1)
        gate = jnp.dot(kmean_ref[c], qs_t, precision=lax.Precision.HIGHEST,
                       preferred_element_type=F32)
        bias = _topk_bias(gate, qi, nb_pad)
        qaug_ref[c, 0:hd, :] = (qs_t * (hd ** -0.5 * LOG2E)).astype(BF16)
        qaug_ref[c, hd:hd + nb_pad, :] = bias.astype(BF16)
        qaug_ref[c, hd + nb_pad:, :] = jnp.full((LANE - nb_pad, rows), MASK_BIAS, BF16)

    def scores(c, r0):
        return jnp.dot(kaug_ref[c, pl.ds(r0, blk_sz), :], qaug_ref[c],
                       preferred_element_type=F32)

    def accumulate(c, s, r0):
        m_prev = m_ref[c]
        m_new = jnp.maximum(m_prev, jnp.max(s, axis=0, keepdims=True))
        p = jnp.exp2(s - m_new).astype(BF16)
        pv = jnp.dot(vt_ref[c, :, pl.ds(r0, blk_sz)], p, preferred_element_type=F32)
        acc_ref[c] = jnp.exp2(m_prev - m_new) * acc_ref[c] + pv
        m_ref[c] = m_new

    for c in chains:
        sa_ref[c] = scores(c, 0)

    r_own = pl.multiple_of(qi * blk_sz, blk_sz)
    key_t = lax.broadcasted_iota(jnp.int32, (blk_sz, rows), 0)
    row_t = lax.broadcasted_iota(jnp.int32, (blk_sz, rows), 1) % blk_sz
    for c in chains:
        s = jnp.dot(kaug_ref[c, pl.ds(r_own, blk_sz), 0:hd], qaug_ref[c, 0:hd, :],
                    preferred_element_type=F32)
        s = jnp.where(key_t <= row_t, s, NEG_INF)
        m0 = jnp.max(s, axis=0, keepdims=True)
        m_ref[c] = m0
        acc_ref[c] = jnp.dot(vt_ref[c, :, pl.ds(r_own, blk_sz)], jnp.exp2(s - m0).astype(BF16),
                             preferred_element_type=F32)

    n_pairs = (qi + 1) // 2

    def past_pair(jj, carry):
        r0 = pl.multiple_of(jj * (2 * blk_sz), 2 * blk_sz)
        r1 = pl.multiple_of(r0 + blk_sz, blk_sz)
        r2 = pl.multiple_of(jnp.minimum(r0 + 2 * blk_sz, (n_blocks - 1) * blk_sz), blk_sz)
        for c in chains:
            sb_ref[c] = scores(c, r1)
        for c in chains:
            accumulate(c, sa_ref[c], r0)
        for c in chains:
            sa_ref[c] = scores(c, r2)
        for c in chains:
            accumulate(c, sb_ref[c], r1)
        return carry

    lax.fori_loop(0, n_pairs, past_pair, 0)

    for c in chains:
        acc = acc_ref[c]
        o = (acc[0:hd, :] * (1.0 / acc[hd:hd + 1, :])).T
        o = jnp.concatenate([o[blk_sz * g:blk_sz * (g + 1), :] for g in range(MOBA_GROUP)], axis=1)
        o_ref[:, gw * c:gw * (c + 1)] = (o * _silu(g_ref[:, gw * c:gw * (c + 1)])).astype(o_ref.dtype)


def _moba_prompt(proj, t):
    assert t % (2 * MOBA_BLOCK) == 0
    nb = t // MOBA_BLOCK
    nb_pad = -(-nb // BF16_SUBLANES) * BF16_SUBLANES
    assert nb_pad <= LANE
    rows = MOBA_GROUP * MOBA_BLOCK
    nc = _MOBA_KV_PER_STEP
    gw = MOBA_GROUP * MOBA_HD * nc
    kvw = MOBA_HD * nc
    assert MOBA_KV_HEADS % nc == 0
    assert OFF_MQ % gw == 0 and OFF_MG % gw == 0 and OFF_MK % kvw == 0 and OFF_MV % kvw == 0
    kv_spec = functools.partial(pl.BlockSpec, (t, kvw), pipeline_mode=pl.Buffered(1))
    return pl.pallas_call(
        functools.partial(_moba_prompt_kernel, n_blocks=nb),
        grid=(MOBA_KV_HEADS // nc, nb),
        in_specs=[pl.BlockSpec((MOBA_BLOCK, gw), lambda h, i: (i, OFF_MQ // gw + h)),
                  kv_spec(lambda h, i: (0, OFF_MK // kvw + h)),
                  kv_spec(lambda h, i: (0, OFF_MV // kvw + h)),
                  pl.BlockSpec((MOBA_BLOCK, gw), lambda h, i: (i, OFF_MG // gw + h))],
        out_specs=pl.BlockSpec((MOBA_BLOCK, gw), lambda h, i: (i, h)),
        out_shape=jax.ShapeDtypeStruct((t, MOBA_QW), BF16),
        scratch_shapes=[pltpu.VMEM((nc, t, MOBA_HD + LANE), BF16),
                        pltpu.VMEM((nc, MOBA_HD + BF16_SUBLANES, t), BF16),
                        pltpu.VMEM((nc, nb_pad, MOBA_HD), F32),
                        pltpu.VMEM((nc, MOBA_HD + LANE, rows), BF16),
                        pltpu.VMEM((nc, 1, rows), F32),
                        pltpu.VMEM((nc, MOBA_HD + BF16_SUBLANES, rows), F32),
                        pltpu.VMEM((nc, MOBA_BLOCK, rows), F32),
                        pltpu.VMEM((nc, MOBA_BLOCK, rows), F32)],
        compiler_params=_params("arbitrary", "arbitrary"),
        name="moba_prompt",
    )(proj, proj, proj, proj)


_GLA_GROUP = 8
_GLA_PAIRS_PER_STEP = 8
_GLA_PAIRS_PER_VCHUNK = 4


def _gla_prompt_kernel(q_ref, k_ref, b_ref, *rest, n_tblocks, tb):
    ti = pl.program_id(1)
    sub = GLA_SUB
    kw = 2 * GLA_DK
    vw = 2 * GLA_DV
    npp = _GLA_PAIRS_PER_STEP
    nvc = npp // _GLA_PAIRS_PER_VCHUNK
    v_refs, gg_refs = rest[:nvc], rest[nvc:2 * nvc]
    gain_ref, o_ref, sout_ref, s_ref, oacc_ref = rest[2 * nvc:]

    @pl.when(ti == 0)
    def _zero_state():
        s_ref[...] = jnp.zeros(s_ref.shape, F32)

    row_h = lax.broadcasted_iota(jnp.int32, (kw, vw), 0) // GLA_DK
    col_h = lax.broadcasted_iota(jnp.int32, (kw, vw), 1) // GLA_DV
    same_head = row_h == col_h
    head_sum = jnp.where(same_head, 1.0, 0.0).astype(BF16)
    j_idx = lax.broadcasted_iota(jnp.int32, (sub, kw), 0)
    grp = _GLA_GROUP
    lane_head = lax.broadcasted_iota(jnp.int32, (grp * sub, kw), 1) // GLA_DK
    gt = grp * sub
    half = sub // 2
    sub_rows = half * half + half * sub
    rs_in = lax.broadcasted_iota(jnp.int32, (sub, sub_rows), 1)
    rs_tok = jnp.where(rs_in < half * half, rs_in // half, half + (rs_in - half * half) // sub)
    row_sum = jnp.where(lax.broadcasted_iota(jnp.int32, (sub, sub_rows), 0) == rs_tok,
                        1.0, 0.0).astype(BF16)

    def pair_group(q, k, b, v, state):
        w_rows, v_rows = [], []
        for s in range(grp):
            lo = s * sub
            b_s, k_s, v_s = b[lo:lo + sub, :], k[lo:lo + sub, :], v[lo:lo + sub, :]
            for i in range(sub):
                nj = half if i < half else sub
                d = b_s[i:i + 1, :] - b_s[0:nj, :]
                dec = jnp.exp(jnp.where(j_idx[0:nj, :] < i, d, NEG_INF))
                w_rows.append(dec * k_s[0:nj, :] * q[lo + i:lo + i + 1, :])
                v_rows.append(v_s[0:nj, :])
        w = jnp.concatenate(w_rows, axis=0).astype(BF16)
        sc = jnp.dot(w, head_sum, preferred_element_type=F32)
        z = (sc * jnp.concatenate(v_rows, axis=0)).astype(BF16)
        o_diag = jnp.concatenate(
            [jnp.dot(row_sum, z[sub_rows * s:sub_rows * (s + 1), :], preferred_element_type=F32)
             for s in range(grp)], axis=0)
        qk = q * k
        o_self = jnp.concatenate(
            [jnp.sum(jnp.where(lane_head == h, qk, 0.0), axis=1, keepdims=True)
             * v[:, GLA_DV * h:GLA_DV * (h + 1)] for h in range(2)], axis=1)
        o_diag = o_diag + o_self
        q_dec = (q * jnp.exp(b)).astype(BF16)
        upds, decays = [], []
        for s in range(grp):
            lo = s * sub
            b_s, k_s, v_s = b[lo:lo + sub, :], k[lo:lo + sub, :], v[lo:lo + sub, :]
            b_last = b_s[sub - 1:sub, :]
            k_dec = (k_s * jnp.exp(b_last - b_s)).astype(BF16)
            upd = lax.dot_general(k_dec, v_s.astype(BF16), _TN, preferred_element_type=F32)
            upds.append(jnp.where(same_head, upd, 0.0))
            e_col = jnp.broadcast_to(jnp.exp(b_last), (kw, kw)).T
            decays.append(jnp.concatenate([e_col, e_col], axis=1))
        o_inter = []
        for s in range(grp):
            lo = s * sub
            o_inter.append(jnp.dot(q_dec[lo:lo + sub, :], state.astype(BF16),
                                   preferred_element_type=F32))
            state = state * decays[s] + upds[s]
        return jnp.concatenate(o_inter, axis=0) + o_diag, state

    def group(g, carry):
        r0 = pl.multiple_of(g * gt, gt)
        for pp in range(npp):
            ks, vs = slice(kw * pp, kw * (pp + 1)), slice(vw * pp, vw * (pp + 1))
            pc = pp % _GLA_PAIRS_PER_VCHUNK
            v_pair = v_refs[pp // _GLA_PAIRS_PER_VCHUNK][pl.ds(r0, gt), vw * pc:vw * (pc + 1)]
            o, state = pair_group(q_ref[pl.ds(r0, gt), ks] * (GLA_DK ** -0.5), k_ref[pl.ds(r0, gt), ks],
                                  b_ref[pl.ds(r0, gt), ks], v_pair, s_ref[pp])
            s_ref[pp] = state
            oacc_ref[pl.ds(r0, gt), vs] = o
        return carry

    lax.fori_loop(0, tb // gt, group, 0)

    o = oacc_ref[...]
    gain = gain_ref[...]
    heads = []
    for h in range(2 * npp):
        oh = o[:, GLA_DV * h:GLA_DV * (h + 1)]
        ms = jnp.mean(oh * oh, axis=-1, keepdims=True)
        heads.append(oh * lax.rsqrt(ms + NORM_EPS) * gain)
    o = jnp.concatenate(heads, axis=1)
    gate = jnp.concatenate([r[...] for r in gg_refs], axis=1)
    o_ref[...] = (o * _silu(gate)).astype(o_ref.dtype)

    @pl.when(ti == n_tblocks - 1)
    def _emit_state():
        for pp in range(npp):
            st = s_ref[pp]
            sout_ref[2 * pp] = st[0:GLA_DK, 0:GLA_DV]
            sout_ref[2 * pp + 1] = st[GLA_DK:2 * GLA_DK, GLA_DV:2 * GLA_DV]


def _gla_prompt(proj, decay, gain, t):
    tb = _tile(t, (512, 256))
    assert t % tb == 0 and tb % (GLA_SUB * _GLA_GROUP) == 0
    nt = t // tb
    npp = _GLA_PAIRS_PER_STEP
    kw, vw = 2 * GLA_DK * npp, 2 * GLA_DV * npp
    vcw = 2 * GLA_DV * _GLA_PAIRS_PER_VCHUNK
    nvc = npp // _GLA_PAIRS_PER_VCHUNK
    assert GLA_PAIRS % npp == 0 and npp % _GLA_PAIRS_PER_VCHUNK == 0
    assert OFF_GQ % kw == 0 and OFF_GK % kw == 0 and OFF_GV % vcw == 0 and OFF_GG % vcw == 0

    def vchunk(off, c):
        return pl.BlockSpec((tb, vcw), lambda p, i: (i, off // vcw + p * nvc + c))

    return pl.pallas_call(
        functools.partial(_gla_prompt_kernel, n_tblocks=nt, tb=tb),
        grid=(GLA_PAIRS // npp, nt),
        in_specs=[pl.BlockSpec((tb, kw), lambda p, i: (i, OFF_GQ // kw + p)),
                  pl.BlockSpec((tb, kw), lambda p, i: (i, OFF_GK // kw + p)),
                  pl.BlockSpec((tb, kw), lambda p, i: (i, p))]
                 + [vchunk(OFF_GV, c) for c in range(nvc)]
                 + [vchunk(OFF_GG, c) for c in range(nvc)]
                 + [pl.BlockSpec((1, GLA_DV), lambda p, i: (0, 0))],
        out_specs=[pl.BlockSpec((tb, vw), lambda p, i: (i, p)),
                   pl.BlockSpec((2 * npp, GLA_DK, GLA_DV), lambda p, i: (p, 0, 0))],
        out_shape=[jax.ShapeDtypeStruct((t, GLA_VW), BF16),
                   jax.ShapeDtypeStruct((GLA_HEADS, GLA_DK, GLA_DV), F32)],
        scratch_shapes=[pltpu.VMEM((npp, 2 * GLA_DK, 2 * GLA_DV), F32),
                        pltpu.VMEM((tb, vw), F32)],
        compiler_params=_params("arbitrary", "arbitrary"),
        name="gla_prompt",
    )(proj, proj, decay, *([proj] * (2 * nvc)), gain)


def _out_proj_kernel(og_ref, om_ref, wg_ref, wm_ref, x_ref, lg_ref, lb_ref, o_ref,
                     *, n_ctiles, tn, alpha):
    j = pl.program_id(1)
    mixed = jnp.dot(og_ref[...], wg_ref[...], preferred_element_type=F32)
    mixed = mixed + jnp.dot(om_ref[...], wm_ref[...], preferred_element_type=F32)
    c0 = pl.multiple_of(j * tn, tn)
    o_ref[:, pl.ds(c0, tn)] = alpha * x_ref[...] + mixed

    @pl.when(j == n_ctiles - 1)
    def _layer_norm():
        h = o_ref[...]
        mu = jnp.mean(h, axis=-1, keepdims=True)
        hc = h - mu
        var = jnp.mean(hc * hc, axis=-1, keepdims=True)
        o_ref[...] = hc * lax.rsqrt(var + NORM_EPS) * lg_ref[...] + lb_ref[...]


def _out_proj(og, om, w_out_bf, x, ln_g, ln_b, alpha, tm, tn):
    m, d = x.shape
    kg, km = og.shape[1], om.shape[1]
    assert kg == km and w_out_bf.shape[0] == kg + km
    nct = d // tn
    w_mode = dict(pipeline_mode=pl.Buffered(1)) if nct == 1 else {}
    return pl.pallas_call(
        functools.partial(_out_proj_kernel, n_ctiles=nct, tn=tn, alpha=alpha),
        grid=(m // tm, nct),
        in_specs=[pl.BlockSpec((tm, kg), lambda i, j: (i, 0)),
                  pl.BlockSpec((tm, km), lambda i, j: (i, 0)),
                  pl.BlockSpec((kg, tn), lambda i, j: (0, j), **w_mode),
                  pl.BlockSpec((km, tn), lambda i, j: (1, j), **w_mode),
                  pl.BlockSpec((tm, tn), lambda i, j: (i, j)),
                  pl.BlockSpec((1, d), lambda i, j: (0, 0)),
                  pl.BlockSpec((1, d), lambda i, j: (0, 0))],
        out_specs=pl.BlockSpec((tm, d), lambda i, j: (i, 0)),
        out_shape=jax.ShapeDtypeStruct((m, d), F32),
        compiler_params=_params("arbitrary", "arbitrary"),
        name="out_proj_ln",
    )(og, om, w_out_bf, w_out_bf, x, ln_g, ln_b)


_MOBA_DEC_SEQS = 2


def _moba_decode_kernel(pt_ref, q_ref, kn_ref, vn_ref, g_ref, *rest, n_pages, page):
    del pt_ref
    n_seq = _MOBA_DEC_SEQS
    o_ref = rest[2 * n_seq * n_pages]
    k_refs = [rest[sq * n_pages:(sq + 1) * n_pages] for sq in range(n_seq)]
    v_refs = [rest[(n_seq + sq) * n_pages:(n_seq + sq + 1) * n_pages] for sq in range(n_seq)]
    qs = [q_ref[sq] * (MOBA_HD ** -0.5) for sq in range(n_seq)]
    ss = [_moba_decode_scores(qs[sq], k_refs[sq], n_pages) for sq in range(n_seq)]
    ps = [_moba_decode_select(ss[sq], qs[sq], kn_ref[sq], n_pages, page) for sq in range(n_seq)]
    for sq in range(n_seq):
        pb, p_new, denom = ps[sq]
        o = _moba_decode_values(pb, p_new, denom, vn_ref[sq], v_refs[sq], n_pages, page)
        o_ref[sq] = (o * _silu(g_ref[sq])).astype(o_ref.dtype)


def _moba_decode_scores(q, k_refs, n_pages):
    qb = q.astype(BF16)
    return jnp.concatenate(
        [lax.dot_general(qb, k_refs[p][0, 0].astype(BF16), _NT, preferred_element_type=F32)
         for p in range(n_pages)], axis=1)


def _expand_kv_rows(x):
    return jnp.concatenate([jnp.broadcast_to(x[i:i + 1, :], (MOBA_GROUP, MOBA_HD))
                            for i in range(MOBA_KV_HEADS)], axis=0)


def _moba_decode_select(s, q, kn, n_pages, page):
    heads = MOBA_HEADS
    brow = MOBA_BLOCK * MOBA_KV_HEADS
    n_blk = (n_pages * page) // MOBA_BLOCK
    lane = lax.broadcasted_iota(jnp.int32, s.shape, 1)
    head_kv = lax.broadcasted_iota(jnp.int32, s.shape, 0) // MOBA_GROUP
    own_kv = (lane % MOBA_KV_HEADS) == head_kv
    s_own = jnp.where(own_kv, s, 0.0)

    assert n_blk <= LANE
    blk_lane = lax.broadcasted_iota(jnp.int32, (heads, LANE), 1)
    cur = jnp.full((heads, LANE), NEG_INF, F32)
    for j in range(n_blk):
        mean_j = jnp.sum(s_own[:, brow * j:brow * (j + 1)], axis=1, keepdims=True) * (1.0 / MOBA_BLOCK)
        cur = jnp.where(blk_lane == j, mean_j, cur)
    sel = jnp.zeros((heads, LANE), F32)
    for _ in range(min(MOBA_TOPK, n_blk)):
        mx = jnp.max(cur, axis=1, keepdims=True)
        first = jnp.min(jnp.where(cur == mx, blk_lane, n_blk), axis=1, keepdims=True)
        pick = blk_lane == first
        sel = jnp.where(pick, 1.0, sel)
        cur = jnp.where(pick, NEG_INF, cur)
    bias = jnp.concatenate(
        [jnp.broadcast_to(jnp.where(jnp.sum(jnp.where(blk_lane == j, sel, 0.0), axis=1, keepdims=True) > 0.5,
                                    0.0, NEG_INF), (heads, brow)) for j in range(n_blk)], axis=1)

    s_new = jnp.sum(q * _expand_kv_rows(kn), axis=1, keepdims=True)
    sm = jnp.where(own_kv, s + bias, NEG_INF)
    m = jnp.maximum(jnp.max(sm, axis=1, keepdims=True), s_new)
    p = jnp.exp(sm - m)
    p_new = jnp.exp(s_new - m)
    denom = jnp.sum(p, axis=1, keepdims=True) + p_new
    return p.astype(BF16), p_new, denom


def _moba_decode_values(pb, p_new, denom, vn, v_refs, n_pages, page):
    prow = page * MOBA_KV_HEADS
    o = p_new * _expand_kv_rows(vn)
    for pg in range(n_pages):
        o = o + jnp.dot(pb[:, prow * pg:prow * (pg + 1)], v_refs[pg][0, 0].astype(BF16),
                        preferred_element_type=F32)
    return o * (1.0 / denom)


def _moba_decode(page_table, q3, kn3, vn3, g3, cache_k, cache_v, layer):
    db, n_pages = page_table.shape
    depth, n_pool, page = cache_k.shape[0], cache_k.shape[1], cache_k.shape[2]
    assert (n_pages * page) % MOBA_BLOCK == 0
    prow = page * MOBA_KV_HEADS
    ck = cache_k.reshape(depth, n_pool, prow, MOBA_HD)
    cv = cache_v.reshape(depth, n_pool, prow, MOBA_HD)

    n_seq = _MOBA_DEC_SEQS
    assert db % n_seq == 0

    def page_spec(sq, pg):
        return pl.BlockSpec((1, 1, prow, MOBA_HD), lambda b, pt: (layer, pt[b * n_seq + sq, pg], 0, 0))

    def row_spec(r):
        return pl.BlockSpec((n_seq, r, MOBA_HD), lambda b, pt: (b, 0, 0))

    pages = [page_spec(sq, pg) for sq in range(n_seq) for pg in range(n_pages)]
    grid_spec = pltpu.PrefetchScalarGridSpec(
        num_scalar_prefetch=1,
        grid=(db // n_seq,),
        in_specs=[row_spec(MOBA_HEADS), row_spec(MOBA_KV_HEADS), row_spec(MOBA_KV_HEADS),
                  row_spec(MOBA_HEADS)] + pages + pages,
        out_specs=pl.BlockSpec((n_seq, MOBA_HEADS, MOBA_HD), lambda b, pt: (b, 0, 0)),
    )
    n_ops = n_seq * n_pages
    return pl.pallas_call(
        functools.partial(_moba_decode_kernel, n_pages=n_pages, page=page),
        grid_spec=grid_spec,
        out_shape=jax.ShapeDtypeStruct((db, MOBA_HEADS, MOBA_HD), BF16),
        compiler_params=_params("arbitrary"),
        name="moba_decode",
    )(page_table, q3, kn3, vn3, g3, *([ck] * n_ops), *([cv] * n_ops))


_GLA_DEC_GROUP = 8


def _gla_decode_kernel(q_ref, k_ref, la_ref, v_ref, gg_ref, gain_ref, s_ref, o_ref, sn_ref):
    grp = _GLA_DEC_GROUP

    def columns(x):
        xp = jnp.concatenate([x, jnp.zeros((LANE - grp, x.shape[1]), F32)], axis=0)
        return xp.T

    a_t = columns(jnp.exp(la_ref[...]))
    k_t = columns(k_ref[...])
    q_t = columns(q_ref[...] * (GLA_DK ** -0.5))
    gain = gain_ref[...]
    for i in range(grp):
        rows = []
        for h in range(GLA_HEADS):
            r = slice(GLA_DK * h, GLA_DK * (h + 1))
            v_h = jnp.broadcast_to(v_ref[i, h:h + 1, :], (GLA_DK, GLA_DV))
            s_new = a_t[r, i:i + 1] * s_ref[i, h] + k_t[r, i:i + 1] * v_h
            sn_ref[i, h] = s_new
            rows.append(jnp.sum(q_t[r, i:i + 1] * s_new, axis=0, keepdims=True))
        o = jnp.concatenate(rows, axis=0)
        ms = jnp.mean(o * o, axis=-1, keepdims=True)
        o = o * lax.rsqrt(ms + NORM_EPS) * gain
        o_ref[i] = (o * _silu(gg_ref[i])).astype(o_ref.dtype)


def _gla_decode(q2, k2, la2, v3, gg3, gain, state):
    db = q2.shape[0]
    grp = _GLA_DEC_GROUP
    assert db % grp == 0
    vec = pl.BlockSpec((grp, GLA_KW), lambda g: (g, 0))
    hd3 = pl.BlockSpec((grp, GLA_HEADS, GLA_DV), lambda g: (g, 0, 0))
    st = pl.BlockSpec((grp, GLA_HEADS, GLA_DK, GLA_DV), lambda g: (g, 0, 0, 0))
    return pl.pallas_call(
        _gla_decode_kernel,
        grid=(db // grp,),
        in_specs=[vec, vec, vec, hd3, hd3, pl.BlockSpec((1, GLA_DV), lambda g: (0, 0)), st],
        out_specs=[hd3, st],
        out_shape=[jax.ShapeDtypeStruct((db, GLA_HEADS, GLA_DV), BF16),
                   jax.ShapeDtypeStruct(state.shape, F32)],
        compiler_params=_params("arbitrary"),
        name="gla_decode",
    )(q2, k2, la2, v3, gg3, gain, state)


def _tile(n, prefs):
    for p in prefs:
        if n % p == 0:
            return p
    return n


def kernel(x_prompt, x_sample, cache_k, cache_v, state_gla, page_table,
           w_in, w_a2, b_a, gla_gain, w_out, ln_g, ln_b):
    bsz, t, d = x_prompt.shape
    db, dt, _ = x_sample.shape
    depth = w_in.shape[0]
    assert bsz == 1 and dt == 1
    alpha = (2.0 * depth) ** 0.25
    y_p = x_prompt.reshape(t, d)
    y_s = x_sample.reshape(db, d)
    kp_l, vp_l, sp_l, ks_l, vs_l, ss_l = [], [], [], [], [], []
    for l in range(depth):
        w_in_t = jnp.swapaxes(w_in[l], 0, 1)
        w_lr = jnp.zeros((d, LANE), BF16).at[:, :GLA_LOWRANK].set(w_in[l][:, OFF_GA:].astype(BF16))
        w_a2p = jnp.zeros((LANE, GLA_KW), F32).at[:GLA_LOWRANK, :].set(w_a2[l])
        w_out_bf = w_out[l].astype(BF16)
        gain = gla_gain[l].reshape(1, GLA_DV)
        lg, lb = ln_g[l].reshape(1, d), ln_b[l].reshape(1, d)
        x_all, ga = _stack_cast(y_p, y_s, w_lr, _tile(math.gcd(t, db), (128,)))
        m_all = t + db
        proj = _in_proj(x_all, w_in_t, MAIN_W, _tile(m_all, (640, 512, 256, 128)), 1024)
        ba2 = b_a[l].reshape(1, GLA_KW)
        decay = _log_decay(ga, w_a2p, ba2, 0, t, _tile(math.gcd(t, 512), (512,)), True)
        log_a_s = _log_decay(ga, w_a2p, ba2, t, db, db, False)
        o_m = _moba_prompt(proj, t)
        o_g, s_p = _gla_prompt(proj, decay, gain, t)
        k_rows, v_rows = _kv_rows(proj, t, _tile(t, (512, 256, 128)))
        kp_l.append(k_rows.reshape(bsz, t, MOBA_KV_HEADS, MOBA_HD))
        vp_l.append(v_rows.reshape(bsz, t, MOBA_KV_HEADS, MOBA_HD))
        sp_l.append(s_p.reshape(bsz, GLA_HEADS, GLA_DK, GLA_DV))
        y_p_new = _out_proj(o_g, o_m, w_out_bf, y_p, lg, lb, alpha, _tile(t, (256, 128)), d)
        proj_s = proj[t:]
        k_new = proj_s[:, OFF_MK:OFF_MV].reshape(db, MOBA_KV_HEADS, MOBA_HD)
        v_new = proj_s[:, OFF_MV:OFF_MG].reshape(db, MOBA_KV_HEADS, MOBA_HD)
        o_m_s = _moba_decode(page_table,
                             proj_s[:, OFF_MQ:OFF_MK].reshape(db, MOBA_HEADS, MOBA_HD),
                             k_new, v_new,
                             proj_s[:, OFF_MG:OFF_GQ].reshape(db, MOBA_HEADS, MOBA_HD),
                             cache_k, cache_v, l)
        o_g_s, s_s = _gla_decode(proj_s[:, OFF_GQ:OFF_GK], proj_s[:, OFF_GK:OFF_GV], log_a_s,
                                 proj_s[:, OFF_GV:OFF_GG].reshape(db, GLA_HEADS, GLA_DV),
                                 proj_s[:, OFF_GG:OFF_GA].reshape(db, GLA_HEADS, GLA_DV),
                                 gain, state_gla[l])
        ks_l.append(k_new.reshape(db, dt, MOBA_KV_HEADS, MOBA_HD))
        vs_l.append(v_new.reshape(db, dt, MOBA_KV_HEADS, MOBA_HD))
        ss_l.append(s_s)
        y_s = _out_proj(o_g_s.reshape(db, GLA_VW), o_m_s.reshape(db, MOBA_QW), w_out_bf, y_s, lg, lb,
                        alpha, _tile(db, (128,)), _tile(d, (1024, 512, 256, 128)))
        y_p = y_p_new
    return (y_p.reshape(bsz, t, d), y_s.reshape(db, dt, d),
            jnp.stack(kp_l), jnp.stack(vp_l), jnp.stack(sp_l),
            jnp.stack(ks_l), jnp.stack(vs_l), jnp.stack(ss_l))
```

```python
import functools
import math

import jax
import jax.numpy as jnp
from jax import lax
from jax.experimental import pallas as pl
from jax.experimental.pallas import tpu as pltpu

F32 = jnp.float32
BF16 = jnp.bfloat16
NEG_INF = float("-inf")
MASK_BIAS = -1e30
LOG2E = 1.4426950408889634

MOBA_HEADS = 16
MOBA_KV_HEADS = 4
MOBA_GROUP = MOBA_HEADS // MOBA_KV_HEADS
MOBA_HD = 128
MOBA_BLOCK = 256
MOBA_TOPK = 3
GLA_HEADS = 16
GLA_DK = 64
GLA_DV = 128
GLA_LOWRANK = 16
GLA_GATE_TAU = 16.0
GLA_SUB = 16
GLA_PAIRS = GLA_HEADS // 2
NORM_EPS = 1e-5

MOBA_QW = MOBA_HEADS * MOBA_HD
MOBA_KVW = MOBA_KV_HEADS * MOBA_HD
GLA_KW = GLA_HEADS * GLA_DK
GLA_VW = GLA_HEADS * GLA_DV
OFF_MQ = 0
OFF_MK = OFF_MQ + MOBA_QW
OFF_MV = OFF_MK + MOBA_KVW
OFF_MG = OFF_MV + MOBA_KVW
OFF_GQ = OFF_MG + MOBA_QW
OFF_GK = OFF_GQ + GLA_KW
OFF_GV = OFF_GK + GLA_KW
OFF_GG = OFF_GV + GLA_VW
OFF_GA = OFF_GG + GLA_VW
MAIN_W = OFF_GA

LANE = 128
BF16_SUBLANES = 16
VMEM_LIMIT_BYTES = 60 * 1024 * 1024

_NT = (((1,), (1,)), ((), ()))
_TN = (((0,), (0,)), ((), ()))


def _params(*sem):
    return pltpu.CompilerParams(dimension_semantics=sem, vmem_limit_bytes=VMEM_LIMIT_BYTES)


def _silu(x):
    return x * (1.0 / (1.0 + jnp.exp(-x)))


def _split_bf16(x):
    hi = x.astype(BF16)
    return hi, (x - hi.astype(F32)).astype(BF16)


def _stack_cast_kernel(xp_ref, xs_ref, wlr_ref, o_ref, ga_ref, *, n_prompt_tiles):
    i = pl.program_id(0)

    def emit(x):
        xb = x.astype(o_ref.dtype)
        o_ref[...] = xb
        ga_ref[...] = jnp.dot(xb, wlr_ref[...], preferred_element_type=F32)

    @pl.when(i < n_prompt_tiles)
    def _prompt_rows():
        emit(xp_ref[...])

    @pl.when(i >= n_prompt_tiles)
    def _decode_rows():
        emit(xs_ref[...])


def _stack_cast(xp, xs, w_lr, tr):
    (t, d), db = xp.shape, xs.shape[0]
    assert t % tr == 0 and db % tr == 0
    npt, nst = t // tr, db // tr
    return pl.pallas_call(
        functools.partial(_stack_cast_kernel, n_prompt_tiles=npt),
        grid=(npt + nst,),
        in_specs=[pl.BlockSpec((tr, d), lambda i: (jnp.minimum(i, npt - 1), 0)),
                  pl.BlockSpec((tr, d), lambda i: (jnp.maximum(i - npt, 0), 0)),
                  pl.BlockSpec((d, LANE), lambda i: (0, 0))],
        out_specs=[pl.BlockSpec((tr, d), lambda i: (i, 0)),
                   pl.BlockSpec((tr, LANE), lambda i: (i, 0))],
        out_shape=[jax.ShapeDtypeStruct((t + db, d), BF16),
                   jax.ShapeDtypeStruct((t + db, LANE), F32)],
        compiler_params=_params("arbitrary"),
        name="stack_cast",
    )(xp, xs, w_lr)


def _in_proj_kernel(x_ref, wt_ref, o_ref, wbf_ref):
    @pl.when(pl.program_id(1) == 0)
    def _cast_weight_tile():
        wbf_ref[...] = wt_ref[...].astype(BF16)

    o_ref[...] = lax.dot_general(x_ref[...], wbf_ref[...], _NT, preferred_element_type=F32)


def _in_proj(x, wt, n_out, tm, tn):
    m, k = x.shape
    assert n_out % tn == 0 and m % tm == 0 and wt.shape[0] >= n_out
    return pl.pallas_call(
        _in_proj_kernel,
        grid=(n_out // tn, m // tm),
        in_specs=[pl.BlockSpec((tm, k), lambda j, i: (i, 0)),
                  pl.BlockSpec((tn, k), lambda j, i: (j, 0))],
        out_specs=pl.BlockSpec((tm, tn), lambda j, i: (i, j)),
        out_shape=jax.ShapeDtypeStruct((m, n_out), F32),
        scratch_shapes=[pltpu.VMEM((tn, k), BF16)],
        compiler_params=_params("arbitrary", "arbitrary"),
        name="in_proj",
    )(x, wt)


def _kv_rows_kernel(k_ref, v_ref, ko_ref, vo_ref):
    tok = k_ref.shape[0]
    for h in range(MOBA_KV_HEADS):
        cols = slice(MOBA_HD * h, MOBA_HD * (h + 1))
        ko_ref[pl.ds(h, tok, stride=MOBA_KV_HEADS), :] = k_ref[:, cols]
        vo_ref[pl.ds(h, tok, stride=MOBA_KV_HEADS), :] = v_ref[:, cols]


def _kv_rows(proj, t, tr):
    assert t % tr == 0
    out = jax.ShapeDtypeStruct((t * MOBA_KV_HEADS, MOBA_HD), F32)
    return pl.pallas_call(
        _kv_rows_kernel,
        grid=(t // tr,),
        in_specs=[pl.BlockSpec((tr, MOBA_KVW), lambda i: (i, OFF_MK // MOBA_KVW)),
                  pl.BlockSpec((tr, MOBA_KVW), lambda i: (i, OFF_MV // MOBA_KVW))],
        out_specs=[pl.BlockSpec((tr * MOBA_KV_HEADS, MOBA_HD), lambda i: (i, 0)),
                   pl.BlockSpec((tr * MOBA_KV_HEADS, MOBA_HD), lambda i: (i, 0))],
        out_shape=[out, out],
        compiler_params=_params("arbitrary"),
        name="kv_rows",
    )(proj, proj)


def _log_decay_kernel(ga_ref, wa2_ref, ba_ref, o_ref, *, cumulative):
    ga_hi, ga_lo = _split_bf16(ga_ref[...])
    wa_hi, wa_lo = _split_bf16(wa2_ref[...])
    z = (jnp.dot(ga_hi, wa_hi, preferred_element_type=F32)
         + (jnp.dot(ga_lo, wa_hi, preferred_element_type=F32)
            + jnp.dot(ga_hi, wa_lo, preferred_element_type=F32))) + ba_ref[...]
    log_a = (jnp.minimum(z, 0.0) - jnp.log(1.0 + jnp.exp(-jnp.abs(z)))) * (1.0 / GLA_GATE_TAU)
    if cumulative:
        pos = lax.broadcasted_iota(jnp.int32, log_a.shape, 0) % GLA_SUB
        step = 1
        while step < GLA_SUB:
            log_a = log_a + jnp.where(pos >= step, pltpu.roll(log_a, step, axis=0), 0.0)
            step *= 2
    o_ref[...] = log_a


def _log_decay(ga, w_a2p, b_a, row0, n_rows, tm, cumulative):
    assert row0 % tm == 0 and n_rows % tm == 0 and tm % GLA_SUB == 0
    t0 = row0 // tm
    return pl.pallas_call(
        functools.partial(_log_decay_kernel, cumulative=cumulative),
        grid=(n_rows // tm,),
        in_specs=[pl.BlockSpec((tm, LANE), lambda i: (t0 + i, 0)),
                  pl.BlockSpec((LANE, GLA_KW), lambda i: (0, 0)),
                  pl.BlockSpec((1, GLA_KW), lambda i: (0, 0))],
        out_specs=pl.BlockSpec((tm, GLA_KW), lambda i: (i, 0)),
        out_shape=jax.ShapeDtypeStruct((n_rows, GLA_KW), F32),
        compiler_params=_params("arbitrary"),
        name="gla_log_decay",
    )(ga, w_a2p, b_a)


def _topk_bias(gate, n_valid, n_rows):
    blk = lax.broadcasted_iota(jnp.int32, gate.shape, 0)
    cur = jnp.where(blk < n_valid, gate, NEG_INF)
    sel = jnp.zeros(gate.shape, jnp.bool_)
    for _ in range(MOBA_TOPK):
        mx = jnp.max(cur, axis=0, keepdims=True)
        hit = jnp.logical_and(cur == mx, mx > NEG_INF)
        first = jnp.min(jnp.where(hit, blk, n_rows), axis=0, keepdims=True)
        pick = blk == first
        sel = jnp.logical_or(sel, pick)
        cur = jnp.where(pick, NEG_INF, cur)
    return jnp.where(sel, 0.0, MASK_BIAS)


_MOBA_KV_PER_STEP = 2


def _moba_prompt_kernel(q_ref, k_ref, v_ref, g_ref, o_ref,
                        kaug_ref, vt_ref, kmean_ref, qaug_ref, m_ref, acc_ref, sa_ref, sb_ref,
                        *, n_blocks):
    qi = pl.program_id(1)
    blk_sz = MOBA_BLOCK
    hd = MOBA_HD
    rows = MOBA_GROUP * blk_sz
    gw = MOBA_GROUP * hd
    chains = range(_MOBA_KV_PER_STEP)

    @pl.when(qi == 0)
    def _prepare_kv_heads():
        kmean_ref[...] = jnp.zeros(kmean_ref.shape, F32)
        vt_ref[:, hd:, :] = jnp.ones((vt_ref.shape[0], vt_ref.shape[1] - hd, vt_ref.shape[2]), BF16)
        lane_blk = lax.broadcasted_iota(jnp.int32, (blk_sz, LANE), 1)

        def body(j, carry):
            r0 = pl.multiple_of(j * blk_sz, blk_sz)
            for c in chains:
                kb = k_ref[pl.ds(r0, blk_sz), hd * c:hd * (c + 1)]
                kaug_ref[c, pl.ds(r0, blk_sz), 0:hd] = kb.astype(BF16)
                kaug_ref[c, pl.ds(r0, blk_sz), hd:hd + LANE] = jnp.where(lane_blk == j, 1.0, 0.0).astype(BF16)
                kmean_ref[c, pl.ds(j, 1), :] = jnp.sum(kb, axis=0, keepdims=True) * (1.0 / blk_sz)
                vt_ref[c, 0:hd, pl.ds(r0, blk_sz)] = v_ref[pl.ds(r0, blk_sz), hd * c:hd * (c + 1)].T.astype(BF16)
            return carry

        lax.fori_loop(0, n_blocks, body, 0)

    nb_pad = kmean_ref.shape[1]
    for c in chains:
        q = q_ref[:, gw * c:gw * (c + 1)]
        qs_t = jnp.concatenate([q[:, hd * g:hd * (g + 1)].T for g in range(MOBA_GROUP)], axis=1)
        gate = jnp.dot(kmean_ref[c], qs_t, precision=lax.Precision.HIGHEST,
                       preferred_element_type=F32)
        bias = _topk_bias(gate, qi, nb_pad)
        qaug_ref[c, 0:hd, :] = (qs_t * (hd ** -0.5 * LOG2E)).astype(BF16)
        qaug_ref[c, hd:hd + nb_pad, :] = bias.astype(BF16)
        qaug_ref[c, hd + nb_pad:, :] = jnp.full((LANE - nb_pad, rows), MASK_BIAS, BF16)

    def scores(c, r0):
        return jnp.dot(kaug_ref[c, pl.ds(r0, blk_sz), :], qaug_ref[c],
                       preferred_element_type=F32)

    def accumulate(c, s, r0):
        m_prev = m_ref[c]
        m_new = jnp.maximum(m_prev, jnp.max(s, axis=0, keepdims=True))
        p = jnp.exp2(s - m_new).astype(BF16)
        pv = jnp.dot(vt_ref[c, :, pl.ds(r0, blk_sz)], p, preferred_element_type=F32)
        acc_ref[c] = jnp.exp2(m_prev - m_new) * acc_ref[c] + pv
        m_ref[c] = m_new

    r_own = pl.multiple_of(qi * blk_sz, blk_sz)
    key_t = lax.broadcasted_iota(jnp.int32, (blk_sz, rows), 0)
    row_t = lax.broadcasted_iota(jnp.int32, (blk_sz, rows), 1) % blk_sz
    for c in chains:
        sa_ref[c] = scores(c, 0)
        s = jnp.dot(kaug_ref[c, pl.ds(r_own, blk_sz), 0:hd], qaug_ref[c, 0:hd, :],
                    preferred_element_type=F32)
        s = jnp.where(key_t <= row_t, s, NEG_INF)
        m0 = jnp.max(s, axis=0, keepdims=True)
        m_ref[c] = m0
        acc_ref[c] = jnp.dot(vt_ref[c, :, pl.ds(r_own, blk_sz)], jnp.exp2(s - m0).astype(BF16),
                             preferred_element_type=F32)

    n_pairs = (qi + 1) // 2

    def past_pair(jj, carry):
        r0 = pl.multiple_of(jj * (2 * blk_sz), 2 * blk_sz)
        r1 = pl.multiple_of(r0 + blk_sz, blk_sz)
        r2 = pl.multiple_of(jnp.minimum(r0 + 2 * blk_sz, (n_blocks - 1) * blk_sz), blk_sz)
        for c in chains:
            sb_ref[c] = scores(c, r1)
            accumulate(c, sa_ref[c], r0)
        for c in chains:
            sa_ref[c] = scores(c, r2)
            accumulate(c, sb_ref[c], r1)
        return carry

    lax.fori_loop(0, n_pairs, past_pair, 0)

    for c in chains:
        acc = acc_ref[c]
        o = (acc[0:hd, :] * (1.0 / acc[hd:hd + 1, :])).T
        o = jnp.concatenate([o[blk_sz * g:blk_sz * (g + 1), :] for g in range(MOBA_GROUP)], axis=1)
        o_ref[:, gw * c:gw * (c + 1)] = (o * _silu(g_ref[:, gw * c:gw * (c + 1)])).astype(o_ref.dtype)


def _moba_prompt(proj, t):
    assert t % (2 * MOBA_BLOCK) == 0
    nb = t // MOBA_BLOCK
    nb_pad = -(-nb // BF16_SUBLANES) * BF16_SUBLANES
    assert nb_pad <= LANE
    rows = MOBA_GROUP * MOBA_BLOCK
    nc = _MOBA_KV_PER_STEP
    gw = MOBA_GROUP * MOBA_HD * nc
    kvw = MOBA_HD * nc
    assert MOBA_KV_HEADS % nc == 0
    assert OFF_MQ % gw == 0 and OFF_MG % gw == 0 and OFF_MK % kvw == 0 and OFF_MV % kvw == 0
    kv_spec = functools.partial(pl.BlockSpec, (t, kvw), pipeline_mode=pl.Buffered(1))
    return pl.pallas_call(
        functools.partial(_moba_prompt_kernel, n_blocks=nb),
        grid=(MOBA_KV_HEADS // nc, nb),
        in_specs=[pl.BlockSpec((MOBA_BLOCK, gw), lambda h, i: (i, OFF_MQ // gw + h)),
                  kv_spec(lambda h, i: (0, OFF_MK // kvw + h)),
                  kv_spec(lambda h, i: (0, OFF_MV // kvw + h)),
                  pl.BlockSpec((MOBA_BLOCK, gw), lambda h, i: (i, OFF_MG // gw + h))],
        out_specs=pl.BlockSpec((MOBA_BLOCK, gw), lambda h, i: (i, h)),
        out_shape=jax.ShapeDtypeStruct((t, MOBA_QW), BF16),
        scratch_shapes=[pltpu.VMEM((nc, t, MOBA_HD + LANE), BF16),
                        pltpu.VMEM((nc, MOBA_HD + BF16_SUBLANES, t), BF16),
                        pltpu.VMEM((nc, nb_pad, MOBA_HD), F32),
                        pltpu.VMEM((nc, MOBA_HD + LANE, rows), BF16),
                        pltpu.VMEM((nc, 1, rows), F32),
                        pltpu.VMEM((nc, MOBA_HD + BF16_SUBLANES, rows), F32),
                        pltpu.VMEM((nc, MOBA_BLOCK, rows), F32),
                        pltpu.VMEM((nc, MOBA_BLOCK, rows), F32)],
        compiler_params=_params("arbitrary", "arbitrary"),
        name="moba_prompt",
    )(proj, proj, proj, proj)


_GLA_GROUP = 8
_GLA_PAIRS_PER_STEP = 8
_GLA_PAIRS_PER_VCHUNK = 4


def _gla_prompt_kernel(q_ref, k_ref, b_ref, *rest, n_tblocks, tb):
    ti = pl.program_id(1)
    sub = GLA_SUB
    kw = 2 * GLA_DK
    vw = 2 * GLA_DV
    npp = _GLA_PAIRS_PER_STEP
    nvc = npp // _GLA_PAIRS_PER_VCHUNK
    v_refs, gg_refs = rest[:nvc], rest[nvc:2 * nvc]
    gain_ref, o_ref, sout_ref, s_ref, oacc_ref = rest[2 * nvc:]

    @pl.when(ti == 0)
    def _zero_state():
        s_ref[...] = jnp.zeros(s_ref.shape, F32)

    row_h = lax.broadcasted_iota(jnp.int32, (kw, vw), 0) // GLA_DK
    col_h = lax.broadcasted_iota(jnp.int32, (kw, vw), 1) // GLA_DV
    same_head = row_h == col_h
    head_sum = jnp.where(same_head, 1.0, 0.0).astype(BF16)
    j_idx = lax.broadcasted_iota(jnp.int32, (sub, kw), 0)
    grp = _GLA_GROUP
    lane_head = lax.broadcasted_iota(jnp.int32, (grp * sub, kw), 1) // GLA_DK
    gt = grp * sub
    half = sub // 2
    sub_rows = half * half + half * sub
    rs_in = lax.broadcasted_iota(jnp.int32, (sub, sub_rows), 1)
    rs_tok = jnp.where(rs_in < half * half, rs_in // half, half + (rs_in - half * half) // sub)
    row_sum = jnp.where(lax.broadcasted_iota(jnp.int32, (sub, sub_rows), 0) == rs_tok,
                        1.0, 0.0).astype(BF16)

    def pair_group(q, k, b, v, state):
        w_rows, v_rows = [], []
        for s in range(grp):
            lo = s * sub
            b_s, k_s, v_s = b[lo:lo + sub, :], k[lo:lo + sub, :], v[lo:lo + sub, :]
            for i in range(sub):
                nj = half if i < half else sub
                d = b_s[i:i + 1, :] - b_s[0:nj, :]
                dec = jnp.exp(jnp.where(j_idx[0:nj, :] < i, d, NEG_INF))
                w_rows.append(dec * k_s[0:nj, :] * q[lo + i:lo + i + 1, :])
                v_rows.append(v_s[0:nj, :])
        w = jnp.concatenate(w_rows, axis=0).astype(BF16)
        sc = jnp.dot(w, head_sum, preferred_element_type=F32)
        z = (sc * jnp.concatenate(v_rows, axis=0)).astype(BF16)
        o_diag = jnp.concatenate(
            [jnp.dot(row_sum, z[sub_rows * s:sub_rows * (s + 1), :], preferred_element_type=F32)
             for s in range(grp)], axis=0)
        qk = q * k
        o_self = jnp.concatenate(
            [jnp.sum(jnp.where(lane_head == h, qk, 0.0), axis=1, keepdims=True)
             * v[:, GLA_DV * h:GLA_DV * (h + 1)] for h in range(2)], axis=1)
        o_diag = o_diag + o_self
        q_dec = (q * jnp.exp(b)).astype(BF16)
        upds, decays = [], []
        for s in range(grp):
            lo = s * sub
            b_s, k_s, v_s = b[lo:lo + sub, :], k[lo:lo + sub, :], v[lo:lo + sub, :]
            b_last = b_s[sub - 1:sub, :]
            k_dec = (k_s * jnp.exp(b_last - b_s)).astype(BF16)
            upd = lax.dot_general(k_dec, v_s.astype(BF16), _TN, preferred_element_type=F32)
            upds.append(jnp.where(same_head, upd, 0.0))
            e_col = jnp.broadcast_to(jnp.exp(b_last), (kw, kw)).T
            decays.append(jnp.concatenate([e_col, e_col], axis=1))
        o_inter = []
        for s in range(grp):
            lo = s * sub
            o_inter.append(jnp.dot(q_dec[lo:lo + sub, :], state.astype(BF16),
                                   preferred_element_type=F32))
            state = state * decays[s] + upds[s]
        return jnp.concatenate(o_inter, axis=0) + o_diag, state

    def group(g, carry):
        r0 = pl.multiple_of(g * gt, gt)
        for pp in range(npp):
            ks, vs = slice(kw * pp, kw * (pp + 1)), slice(vw * pp, vw * (pp + 1))
            pc = pp % _GLA_PAIRS_PER_VCHUNK
            v_pair = v_refs[pp // _GLA_PAIRS_PER_VCHUNK][pl.ds(r0, gt), vw * pc:vw * (pc + 1)]
            o, state = pair_group(q_ref[pl.ds(r0, gt), ks] * (GLA_DK ** -0.5), k_ref[pl.ds(r0, gt), ks],
                                  b_ref[pl.ds(r0, gt), ks], v_pair, s_ref[pp])
            s_ref[pp] = state
            oacc_ref[pl.ds(r0, gt), vs] = o
        return carry

    lax.fori_loop(0, tb // gt, group, 0)

    o = oacc_ref[...]
    gain = gain_ref[...]
    heads = []
    for h in range(2 * npp):
        oh = o[:, GLA_DV * h:GLA_DV * (h + 1)]
        ms = jnp.mean(oh * oh, axis=-1, keepdims=True)
        heads.append(oh * lax.rsqrt(ms + NORM_EPS) * gain)
    o = jnp.concatenate(heads, axis=1)
    gate = jnp.concatenate([r[...] for r in gg_refs], axis=1)
    o_ref[...] = (o * _silu(gate)).astype(o_ref.dtype)

    @pl.when(ti == n_tblocks - 1)
    def _emit_state():
        for pp in range(npp):
            st = s_ref[pp]
            sout_ref[2 * pp] = st[0:GLA_DK, 0:GLA_DV]
            sout_ref[2 * pp + 1] = st[GLA_DK:2 * GLA_DK, GLA_DV:2 * GLA_DV]


def _gla_prompt(proj, decay, gain, t):
    tb = _tile(t, (512, 256))
    assert t % tb == 0 and tb % (GLA_SUB * _GLA_GROUP) == 0
    nt = t // tb
    npp = _GLA_PAIRS_PER_STEP
    kw, vw = 2 * GLA_DK * npp, 2 * GLA_DV * npp
    vcw = 2 * GLA_DV * _GLA_PAIRS_PER_VCHUNK
    nvc = npp // _GLA_PAIRS_PER_VCHUNK
    assert GLA_PAIRS % npp == 0 and npp % _GLA_PAIRS_PER_VCHUNK == 0
    assert OFF_GQ % kw == 0 and OFF_GK % kw == 0 and OFF_GV % vcw == 0 and OFF_GG % vcw == 0

    def vchunk(off, c):
        return pl.BlockSpec((tb, vcw), lambda p, i: (i, off // vcw + p * nvc + c))

    return pl.pallas_call(
        functools.partial(_gla_prompt_kernel, n_tblocks=nt, tb=tb),
        grid=(GLA_PAIRS // npp, nt),
        in_specs=[pl.BlockSpec((tb, kw), lambda p, i: (i, OFF_GQ // kw + p)),
                  pl.BlockSpec((tb, kw), lambda p, i: (i, OFF_GK // kw + p)),
                  pl.BlockSpec((tb, kw), lambda p, i: (i, p))]
                 + [vchunk(OFF_GV, c) for c in range(nvc)]
                 + [vchunk(OFF_GG, c) for c in range(nvc)]
                 + [pl.BlockSpec((1, GLA_DV), lambda p, i: (0, 0))],
        out_specs=[pl.BlockSpec((tb, vw), lambda p, i: (i, p)),
                   pl.BlockSpec((2 * npp, GLA_DK, GLA_DV), lambda p, i: (p, 0, 0))],
        out_shape=[jax.ShapeDtypeStruct((t, GLA_VW), BF16),
                   jax.ShapeDtypeStruct((GLA_HEADS, GLA_DK, GLA_DV), F32)],
        scratch_shapes=[pltpu.VMEM((npp, 2 * GLA_DK, 2 * GLA_DV), F32),
                        pltpu.VMEM((tb, vw), F32)],
        compiler_params=_params("arbitrary", "arbitrary"),
        name="gla_prompt",
    )(proj, proj, decay, *([proj] * (2 * nvc)), gain)


def _out_proj_kernel(og_ref, om_ref, wg_ref, wm_ref, x_ref, lg_ref, lb_ref, o_ref,
                     *, n_ctiles, tn, alpha):
    j = pl.program_id(1)
    mixed = jnp.dot(og_ref[...], wg_ref[...], preferred_element_type=F32)
    mixed = mixed + jnp.dot(om_ref[...], wm_ref[...], preferred_element_type=F32)
    c0 = pl.multiple_of(j * tn, tn)
    o_ref[:, pl.ds(c0, tn)] = alpha * x_ref[...] + mixed

    @pl.when(j == n_ctiles - 1)
    def _layer_norm():
        h = o_ref[...]
        mu = jnp.mean(h, axis=-1, keepdims=True)
        hc = h - mu
        var = jnp.mean(hc * hc, axis=-1, keepdims=True)
        o_ref[...] = hc * lax.rsqrt(var + NORM_EPS) * lg_ref[...] + lb_ref[...]


def _out_proj(og, om, w_out_bf, x, ln_g, ln_b, alpha, tm, tn):
    m, d = x.shape
    kg, km = og.shape[1], om.shape[1]
    assert kg == km and w_out_bf.shape[0] == kg + km
    nct = d // tn
    w_mode = dict(pipeline_mode=pl.Buffered(1)) if nct == 1 else {}
    return pl.pallas_call(
        functools.partial(_out_proj_kernel, n_ctiles=nct, tn=tn, alpha=alpha),
        grid=(m // tm, nct),
        in_specs=[pl.BlockSpec((tm, kg), lambda i, j: (i, 0)),
                  pl.BlockSpec((tm, km), lambda i, j: (i, 0)),
                  pl.BlockSpec((kg, tn), lambda i, j: (0, j), **w_mode),
                  pl.BlockSpec((km, tn), lambda i, j: (1, j), **w_mode),
                  pl.BlockSpec((tm, tn), lambda i, j: (i, j)),
                  pl.BlockSpec((1, d), lambda i, j: (0, 0)),
                  pl.BlockSpec((1, d), lambda i, j: (0, 0))],
        out_specs=pl.BlockSpec((tm, d), lambda i, j: (i, 0)),
        out_shape=jax.ShapeDtypeStruct((m, d), F32),
        compiler_params=_params("arbitrary", "arbitrary"),
        name="out_proj_ln",
    )(og, om, w_out_bf, w_out_bf, x, ln_g, ln_b)


_MOBA_DEC_SEQS = 2


def _moba_decode_kernel(pt_ref, q_ref, kn_ref, vn_ref, g_ref, *rest, n_pages, page):
    del pt_ref
    n_seq = _MOBA_DEC_SEQS
    o_ref = rest[2 * n_seq * n_pages]
    k_refs = [rest[sq * n_pages:(sq + 1) * n_pages] for sq in range(n_seq)]
    v_refs = [rest[(n_seq + sq) * n_pages:(n_seq + sq + 1) * n_pages] for sq in range(n_seq)]
    qs = [q_ref[sq] * (MOBA_HD ** -0.5) for sq in range(n_seq)]
    ss = [_moba_decode_scores(qs[sq], k_refs[sq], n_pages) for sq in range(n_seq)]
    ps = [_moba_decode_select(ss[sq], qs[sq], kn_ref[sq], n_pages, page) for sq in range(n_seq)]
    for sq in range(n_seq):
        pb, p_new, denom = ps[sq]
        o = _moba_decode_values(pb, p_new, denom, vn_ref[sq], v_refs[sq], n_pages, page)
        o_ref[sq] = (o * _silu(g_ref[sq])).astype(o_ref.dtype)


def _moba_decode_scores(q, k_refs, n_pages):
    qb = q.astype(BF16)
    return jnp.concatenate(
        [lax.dot_general(qb, k_refs[p][0, 0].astype(BF16), _NT, preferred_element_type=F32)
         for p in range(n_pages)], axis=1)


def _expand_kv_rows(x):
    return jnp.concatenate([jnp.broadcast_to(x[i:i + 1, :], (MOBA_GROUP, MOBA_HD))
                            for i in range(MOBA_KV_HEADS)], axis=0)


def _moba_decode_select(s, q, kn, n_pages, page):
    heads = MOBA_HEADS
    brow = MOBA_BLOCK * MOBA_KV_HEADS
    n_blk = (n_pages * page) // MOBA_BLOCK
    lane = lax.broadcasted_iota(jnp.int32, s.shape, 1)
    head_kv = lax.broadcasted_iota(jnp.int32, s.shape, 0) // MOBA_GROUP
    own_kv = (lane % MOBA_KV_HEADS) == head_kv
    s_own = jnp.where(own_kv, s, 0.0)

    assert n_blk <= LANE
    blk_lane = lax.broadcasted_iota(jnp.int32, (heads, LANE), 1)
    cur = jnp.full((heads, LANE), NEG_INF, F32)
    for j in range(n_blk):
        mean_j = jnp.sum(s_own[:, brow * j:brow * (j + 1)], axis=1, keepdims=True) * (1.0 / MOBA_BLOCK)
        cur = jnp.where(blk_lane == j, mean_j, cur)
    sel = jnp.zeros((heads, LANE), F32)
    for _ in range(min(MOBA_TOPK, n_blk)):
        mx = jnp.max(cur, axis=1, keepdims=True)
        first = jnp.min(jnp.where(cur == mx, blk_lane, n_blk), axis=1, keepdims=True)
        pick = blk_lane == first
        sel = jnp.where(pick, 1.0, sel)
        cur = jnp.where(pick, NEG_INF, cur)
    bias = jnp.concatenate(
        [jnp.broadcast_to(jnp.where(jnp.sum(jnp.where(blk_lane == j, sel, 0.0), axis=1, keepdims=True) > 0.5,
                                    0.0, NEG_INF), (heads, brow)) for j in range(n_blk)], axis=1)

    s_new = jnp.sum(q * _expand_kv_rows(kn), axis=1, keepdims=True)
    sm = jnp.where(own_kv, s + bias, NEG_INF)
    m = jnp.maximum(jnp.max(sm, axis=1, keepdims=True), s_new)
    p = jnp.exp(sm - m)
    p_new = jnp.exp(s_new - m)
    denom = jnp.sum(p, axis=1, keepdims=True) + p_new
    return p.astype(BF16), p_new, denom


def _moba_decode_values(pb, p_new, denom, vn, v_refs, n_pages, page):
    prow = page * MOBA_KV_HEADS
    o = p_new * _expand_kv_rows(vn)
    for pg in range(n_pages):
        o = o + jnp.dot(pb[:, prow * pg:prow * (pg + 1)], v_refs[pg][0, 0].astype(BF16),
                        preferred_element_type=F32)
    return o * (1.0 / denom)


def _moba_decode(page_table, q3, kn3, vn3, g3, cache_k, cache_v, layer):
    db, n_pages = page_table.shape
    depth, n_pool, page = cache_k.shape[0], cache_k.shape[1], cache_k.shape[2]
    assert (n_pages * page) % MOBA_BLOCK == 0
    prow = page * MOBA_KV_HEADS
    ck = cache_k.reshape(depth, n_pool, prow, MOBA_HD)
    cv = cache_v.reshape(depth, n_pool, prow, MOBA_HD)

    n_seq = _MOBA_DEC_SEQS
    assert db % n_seq == 0

    def page_spec(sq, pg):
        return pl.BlockSpec((1, 1, prow, MOBA_HD), lambda b, pt: (layer, pt[b * n_seq + sq, pg], 0, 0))

    def row_spec(r):
        return pl.BlockSpec((n_seq, r, MOBA_HD), lambda b, pt: (b, 0, 0))

    pages = [page_spec(sq, pg) for sq in range(n_seq) for pg in range(n_pages)]
    grid_spec = pltpu.PrefetchScalarGridSpec(
        num_scalar_prefetch=1,
        grid=(db // n_seq,),
        in_specs=[row_spec(MOBA_HEADS), row_spec(MOBA_KV_HEADS), row_spec(MOBA_KV_HEADS),
                  row_spec(MOBA_HEADS)] + pages + pages,
        out_specs=pl.BlockSpec((n_seq, MOBA_HEADS, MOBA_HD), lambda b, pt: (b, 0, 0)),
    )
    n_ops = n_seq * n_pages
    return pl.pallas_call(
        functools.partial(_moba_decode_kernel, n_pages=n_pages, page=page),
        grid_spec=grid_spec,
        out_shape=jax.ShapeDtypeStruct((db, MOBA_HEADS, MOBA_HD), BF16),
        compiler_params=_params("arbitrary"),
        name="moba_decode",
    )(page_table, q3, kn3, vn3, g3, *([ck] * n_ops), *([cv] * n_ops))


_GLA_DEC_GROUP = 8


def _gla_decode_kernel(q_ref, k_ref, la_ref, v_ref, gg_ref, gain_ref, s_ref, o_ref, sn_ref):
    grp = _GLA_DEC_GROUP

    def columns(x):
        xp = jnp.concatenate([x, jnp.zeros((LANE - grp, x.shape[1]), F32)], axis=0)
        return xp.T

    a_t = columns(jnp.exp(la_ref[...]))
    k_t = columns(k_ref[...])
    q_t = columns(q_ref[...] * (GLA_DK ** -0.5))
    gain = gain_ref[...]
    for i in range(grp):
        rows = []
        for h in range(GLA_HEADS):
            r = slice(GLA_DK * h, GLA_DK * (h + 1))
            v_h = jnp.broadcast_to(v_ref[i, h:h + 1, :], (GLA_DK, GLA_DV))
            s_new = a_t[r, i:i + 1] * s_ref[i, h] + k_t[r, i:i + 1] * v_h
            sn_ref[i, h] = s_new
            rows.append(jnp.sum(q_t[r, i:i + 1] * s_new, axis=0, keepdims=True))
        o = jnp.concatenate(rows, axis=0)
        ms = jnp.mean(o * o, axis=-1, keepdims=True)
        o = o * lax.rsqrt(ms + NORM_EPS) * gain
        o_ref[i] = (o * _silu(gg_ref[i])).astype(o_ref.dtype)


def _gla_decode(q2, k2, la2, v3, gg3, gain, state):
    db = q2.shape[0]
    grp = _GLA_DEC_GROUP
    assert db % grp == 0
    vec = pl.BlockSpec((grp, GLA_KW), lambda g: (g, 0))
    hd3 = pl.BlockSpec((grp, GLA_HEADS, GLA_DV), lambda g: (g, 0, 0))
    st = pl.BlockSpec((grp, GLA_HEADS, GLA_DK, GLA_DV), lambda g: (g, 0, 0, 0))
    return pl.pallas_call(
        _gla_decode_kernel,
        grid=(db // grp,),
        in_specs=[vec, vec, vec, hd3, hd3, pl.BlockSpec((1, GLA_DV), lambda g: (0, 0)), st],
        out_specs=[hd3, st],
        out_shape=[jax.ShapeDtypeStruct((db, GLA_HEADS, GLA_DV), BF16),
                   jax.ShapeDtypeStruct(state.shape, F32)],
        compiler_params=_params("arbitrary"),
        name="gla_decode",
    )(q2, k2, la2, v3, gg3, gain, state)


def _tile(n, prefs):
    for p in prefs:
        if n % p == 0:
            return p
    return n


def kernel(x_prompt, x_sample, cache_k, cache_v, state_gla, page_table,
           w_in, w_a2, b_a, gla_gain, w_out, ln_g, ln_b):
    bsz, t, d = x_prompt.shape
    db, dt, _ = x_sample.shape
    depth = w_in.shape[0]
    assert bsz == 1 and dt == 1
    alpha = (2.0 * depth) ** 0.25
    y_p = x_prompt.reshape(t, d)
    y_s = x_sample.reshape(db, d)
    kp_l, vp_l, sp_l, ks_l, vs_l, ss_l = [], [], [], [], [], []
    for l in range(depth):
        w_in_t = jnp.swapaxes(w_in[l], 0, 1)
        w_lr = jnp.zeros((d, LANE), BF16).at[:, :GLA_LOWRANK].set(w_in[l][:, OFF_GA:].astype(BF16))
        w_a2p = jnp.zeros((LANE, GLA_KW), F32).at[:GLA_LOWRANK, :].set(w_a2[l])
        w_out_bf = w_out[l].astype(BF16)
        gain = gla_gain[l].reshape(1, GLA_DV)
        lg, lb = ln_g[l].reshape(1, d), ln_b[l].reshape(1, d)
        x_all, ga = _stack_cast(y_p, y_s, w_lr, _tile(math.gcd(t, db), (128,)))
        m_all = t + db
        proj = _in_proj(x_all, w_in_t, MAIN_W, _tile(m_all, (640, 512, 256, 128)), 1024)
        ba2 = b_a[l].reshape(1, GLA_KW)
        decay = _log_decay(ga, w_a2p, ba2, 0, t, _tile(math.gcd(t, 512), (512,)), True)
        log_a_s = _log_decay(ga, w_a2p, ba2, t, db, db, False)
        o_m = _moba_prompt(proj, t)
        o_g, s_p = _gla_prompt(proj, decay, gain, t)
        k_rows, v_rows = _kv_rows(proj, t, _tile(t, (512, 256, 128)))
        kp_l.append(k_rows.reshape(bsz, t, MOBA_KV_HEADS, MOBA_HD))
        vp_l.append(v_rows.reshape(bsz, t, MOBA_KV_HEADS, MOBA_HD))
        sp_l.append(s_p.reshape(bsz, GLA_HEADS, GLA_DK, GLA_DV))
        y_p_new = _out_proj(o_g, o_m, w_out_bf, y_p, lg, lb, alpha, _tile(t, (256, 128)), d)
        proj_s = proj[t:]
        k_new = proj_s[:, OFF_MK:OFF_MV].reshape(db, MOBA_KV_HEADS, MOBA_HD)
        v_new = proj_s[:, OFF_MV:OFF_MG].reshape(db, MOBA_KV_HEADS, MOBA_HD)
        o_m_s = _moba_decode(page_table,
                             proj_s[:, OFF_MQ:OFF_MK].reshape(db, MOBA_HEADS, MOBA_HD),
                             k_new, v_new,
                             proj_s[:, OFF_MG:OFF_GQ].reshape(db, MOBA_HEADS, MOBA_HD),
                             cache_k, cache_v, l)
        o_g_s, s_s = _gla_decode(proj_s[:, OFF_GQ:OFF_GK], proj_s[:, OFF_GK:OFF_GV], log_a_s,
                                 proj_s[:, OFF_GV:OFF_GG].reshape(db, GLA_HEADS, GLA_DV),
                                 proj_s[:, OFF_GG:OFF_GA].reshape(db, GLA_HEADS, GLA_DV),
                                 gain, state_gla[l])
        ks_l.append(k_new.reshape(db, dt, MOBA_KV_HEADS, MOBA_HD))
        vs_l.append(v_new.reshape(db, dt, MOBA_KV_HEADS, MOBA_HD))
        ss_l.append(s_s)
        y_s = _out_proj(o_g_s.reshape(db, GLA_VW), o_m_s.reshape(db, MOBA_QW), w_out_bf, y_s, lg, lb,
                        alpha, _tile(db, (128,)), _tile(d, (1024, 512, 256, 128)))
        y_p = y_p_new
    return (y_p.reshape(bsz, t, d), y_s.reshape(db, dt, d),
            jnp.stack(kp_l), jnp.stack(vp_l), jnp.stack(sp_l),
            jnp.stack(ks_l), jnp.stack(vs_l), jnp.stack(ss_l))
```

```python
import functools
import math

import jax
import jax.numpy as jnp
from jax import lax
from jax.experimental import pallas as pl
from jax.experimental.pallas import tpu as pltpu

F32 = jnp.float32
BF16 = jnp.bfloat16
NEG_INF = float("-inf")
MASK_BIAS = -1e30
LOG2E = 1.4426950408889634

MOBA_HEADS = 16
MOBA_KV_HEADS = 4
MOBA_GROUP = MOBA_HEADS // MOBA_KV_HEADS
MOBA_HD = 128
MOBA_BLOCK = 256
MOBA_TOPK = 3
GLA_HEADS = 16
GLA_DK = 64
GLA_DV = 128
GLA_LOWRANK = 16
GLA_GATE_TAU = 16.0
GLA_SUB = 16
GLA_PAIRS = GLA_HEADS // 2
NORM_EPS = 1e-5

MOBA_QW = MOBA_HEADS * MOBA_HD
MOBA_KVW = MOBA_KV_HEADS * MOBA_HD
GLA_KW = GLA_HEADS * GLA_DK
GLA_VW = GLA_HEADS * GLA_DV
OFF_MQ = 0
OFF_MK = OFF_MQ + MOBA_QW
OFF_MV = OFF_MK + MOBA_KVW
OFF_MG = OFF_MV + MOBA_KVW
OFF_GQ = OFF_MG + MOBA_QW
OFF_GK = OFF_GQ + GLA_KW
OFF_GV = OFF_GK + GLA_KW
OFF_GG = OFF_GV + GLA_VW
OFF_GA = OFF_GG + GLA_VW
MAIN_W = OFF_GA

LANE = 128
BF16_SUBLANES = 16
VMEM_LIMIT_BYTES = 60 * 1024 * 1024

_NT = (((1,), (1,)), ((), ()))
_TN = (((0,), (0,)), ((), ()))


def _params(*sem):
    return pltpu.CompilerParams(dimension_semantics=sem, vmem_limit_bytes=VMEM_LIMIT_BYTES)


def _silu(x):
    return x * (1.0 / (1.0 + jnp.exp(-x)))


def _split_bf16(x):
    hi = x.astype(BF16)
    return hi, (x - hi.astype(F32)).astype(BF16)


def _stack_cast_kernel(xp_ref, xs_ref, wlr_ref, o_ref, ga_ref, *, n_prompt_tiles):
    i = pl.program_id(0)

    def emit(x):
        xb = x.astype(o_ref.dtype)
        o_ref[...] = xb
        ga_ref[...] = jnp.dot(xb, wlr_ref[...], preferred_element_type=F32)

    @pl.when(i < n_prompt_tiles)
    def _prompt_rows():
        emit(xp_ref[...])

    @pl.when(i >= n_prompt_tiles)
    def _decode_rows():
        emit(xs_ref[...])


def _stack_cast(xp, xs, w_lr, tr):
    (t, d), db = xp.shape, xs.shape[0]
    assert t % tr == 0 and db % tr == 0
    npt, nst = t // tr, db // tr
    return pl.pallas_call(
        functools.partial(_stack_cast_kernel, n_prompt_tiles=npt),
        grid=(npt + nst,),
        in_specs=[pl.BlockSpec((tr, d), lambda i: (jnp.minimum(i, npt - 1), 0)),
                  pl.BlockSpec((tr, d), lambda i: (jnp.maximum(i - npt, 0), 0)),
                  pl.BlockSpec((d, LANE), lambda i: (0, 0))],
        out_specs=[pl.BlockSpec((tr, d), lambda i: (i, 0)),
                   pl.BlockSpec((tr, LANE), lambda i: (i, 0))],
        out_shape=[jax.ShapeDtypeStruct((t + db, d), BF16),
                   jax.ShapeDtypeStruct((t + db, LANE), F32)],
        compiler_params=_params("arbitrary"),
        name="stack_cast",
    )(xp, xs, w_lr)


def _in_proj_kernel(x_ref, wt_ref, o_ref, wbf_ref):
    @pl.when(pl.program_id(1) == 0)
    def _cast_weight_tile():
        wbf_ref[...] = wt_ref[...].astype(BF16)

    o_ref[...] = lax.dot_general(x_ref[...], wbf_ref[...], _NT, preferred_element_type=F32)


def _in_proj(x, wt, n_out, tm, tn):
    m, k = x.shape
    assert n_out % tn == 0 and m % tm == 0 and wt.shape[0] >= n_out
    return pl.pallas_call(
        _in_proj_kernel,
        grid=(n_out // tn, m // tm),
        in_specs=[pl.BlockSpec((tm, k), lambda j, i: (i, 0)),
                  pl.BlockSpec((tn, k), lambda j, i: (j, 0))],
        out_specs=pl.BlockSpec((tm, tn), lambda j, i: (i, j)),
        out_shape=jax.ShapeDtypeStruct((m, n_out), F32),
        scratch_shapes=[pltpu.VMEM((tn, k), BF16)],
        compiler_params=_params("arbitrary", "arbitrary"),
        name="in_proj",
    )(x, wt)


def _kv_rows_kernel(k_ref, v_ref, ko_ref, vo_ref):
    tok = k_ref.shape[0]
    for h in range(MOBA_KV_HEADS):
        cols = slice(MOBA_HD * h, MOBA_HD * (h + 1))
        ko_ref[pl.ds(h, tok, stride=MOBA_KV_HEADS), :] = k_ref[:, cols]
        vo_ref[pl.ds(h, tok, stride=MOBA_KV_HEADS), :] = v_ref[:, cols]


def _kv_rows(proj, t, tr):
    assert t % tr == 0
    out = jax.ShapeDtypeStruct((t * MOBA_KV_HEADS, MOBA_HD), F32)
    return pl.pallas_call(
        _kv_rows_kernel,
        grid=(t // tr,),
        in_specs=[pl.BlockSpec((tr, MOBA_KVW), lambda i: (i, OFF_MK // MOBA_KVW)),
                  pl.BlockSpec((tr, MOBA_KVW), lambda i: (i, OFF_MV // MOBA_KVW))],
        out_specs=[pl.BlockSpec((tr * MOBA_KV_HEADS, MOBA_HD), lambda i: (i, 0)),
                   pl.BlockSpec((tr * MOBA_KV_HEADS, MOBA_HD), lambda i: (i, 0))],
        out_shape=[out, out],
        compiler_params=_params("arbitrary"),
        name="kv_rows",
    )(proj, proj)


def _log_decay_kernel(ga_ref, wa2_ref, ba_ref, o_ref, *, cumulative):
    ga_hi, ga_lo = _split_bf16(ga_ref[...])
    wa_hi, wa_lo = _split_bf16(wa2_ref[...])
    z = (jnp.dot(ga_hi, wa_hi, preferred_element_type=F32)
         + (jnp.dot(ga_lo, wa_hi, preferred_element_type=F32)
            + jnp.dot(ga_hi, wa_lo, preferred_element_type=F32))) + ba_ref[...]
    log_a = (jnp.minimum(z, 0.0) - jnp.log(1.0 + jnp.exp(-jnp.abs(z)))) * (1.0 / GLA_GATE_TAU)
    if cumulative:
        pos = lax.broadcasted_iota(jnp.int32, log_a.shape, 0) % GLA_SUB
        step = 1
        while step < GLA_SUB:
            log_a = log_a + jnp.where(pos >= step, pltpu.roll(log_a, step, axis=0), 0.0)
            step *= 2
    o_ref[...] = log_a


def _log_decay(ga, w_a2p, b_a, row0, n_rows, tm, cumulative):
    assert row0 % tm == 0 and n_rows % tm == 0 and tm % GLA_SUB == 0
    t0 = row0 // tm
    return pl.pallas_call(
        functools.partial(_log_decay_kernel, cumulative=cumulative),
        grid=(n_rows // tm,),
        in_specs=[pl.BlockSpec((tm, LANE), lambda i: (t0 + i, 0)),
                  pl.BlockSpec((LANE, GLA_KW), lambda i: (0, 0)),
                  pl.BlockSpec((1, GLA_KW), lambda i: (0, 0))],
        out_specs=pl.BlockSpec((tm, GLA_KW), lambda i: (i, 0)),
        out_shape=jax.ShapeDtypeStruct((n_rows, GLA_KW), F32),
        compiler_params=_params("arbitrary"),
        name="gla_log_decay",
    )(ga, w_a2p, b_a)


def _topk_bias(gate, n_valid, n_rows):
    blk = lax.broadcasted_iota(jnp.int32, gate.shape, 0)
    cur = jnp.where(blk < n_valid, gate, NEG_INF)
    sel = jnp.zeros(gate.shape, jnp.bool_)
    for _ in range(MOBA_TOPK):
        mx = jnp.max(cur, axis=0, keepdims=True)
        hit = jnp.logical_and(cur == mx, mx > NEG_INF)
        first = jnp.min(jnp.where(hit, blk, n_rows), axis=0, keepdims=True)
        pick = blk == first
        sel = jnp.logical_or(sel, pick)
        cur = jnp.where(pick, NEG_INF, cur)
    return jnp.where(sel, 0.0, MASK_BIAS)


_MOBA_KV_PER_STEP = 2


def _moba_prompt_kernel(q_ref, k_ref, v_ref, g_ref, o_ref,
                        kaug_ref, vt_ref, kmean_ref, qaug_ref, m_ref, acc_ref, sa_ref, sb_ref,
                        *, n_blocks):
    qi = pl.program_id(1)
    blk_sz = MOBA_BLOCK
    hd = MOBA_HD
    rows = MOBA_GROUP * blk_sz
    gw = MOBA_GROUP * hd
    chains = range(_MOBA_KV_PER_STEP)

    @pl.when(qi == 0)
    def _prepare_kv_heads():
        kmean_ref[...] = jnp.zeros(kmean_ref.shape, F32)
        vt_ref[:, hd:, :] = jnp.ones((vt_ref.shape[0], vt_ref.shape[1] - hd, vt_ref.shape[2]), BF16)
        lane_blk = lax.broadcasted_iota(jnp.int32, (blk_sz, LANE), 1)

        def body(j, carry):
            r0 = pl.multiple_of(j * blk_sz, blk_sz)
            for c in chains:
                kb = k_ref[pl.ds(r0, blk_sz), hd * c:hd * (c + 1)]
                kaug_ref[c, pl.ds(r0, blk_sz), 0:hd] = kb.astype(BF16)
                kaug_ref[c, pl.ds(r0, blk_sz), hd:hd + LANE] = jnp.where(lane_blk == j, 1.0, 0.0).astype(BF16)
                kmean_ref[c, pl.ds(j, 1), :] = jnp.sum(kb, axis=0, keepdims=True) * (1.0 / blk_sz)
                vt_ref[c, 0:hd, pl.ds(r0, blk_sz)] = v_ref[pl.ds(r0, blk_sz), hd * c:hd * (c + 1)].T.astype(BF16)
            return carry

        lax.fori_loop(0, n_blocks, body, 0)

    nb_pad = kmean_ref.shape[1]
    r_own = pl.multiple_of(qi * blk_sz, blk_sz)
    for c in chains:
        q = q_ref[:, gw * c:gw * (c + 1)]
        qs_t = jnp.concatenate([q[:, hd * g:hd * (g + 1)].T for g in range(MOBA_GROUP)], axis=1)
        gate = jnp.dot(kmean_ref[c], qs_t, precision=lax.Precision.HIGHEST,
                       preferred_element_type=F32)
        bias = _topk_bias(gate, qi, nb_pad)
        qaug_ref[c, 0:hd, :] = (qs_t * (hd ** -0.5 * LOG2E)).astype(BF16)
        qaug_ref[c, hd:hd + nb_pad, :] = bias.astype(BF16)
        qaug_ref[c, hd + nb_pad:, :] = jnp.full((LANE - nb_pad, rows), MASK_BIAS, BF16)
        sa_ref[c] = jnp.dot(kaug_ref[c, pl.ds(0, blk_sz), :], qaug_ref[c], preferred_element_type=F32)
        sb_ref[c] = jnp.dot(kaug_ref[c, pl.ds(r_own, blk_sz), 0:hd], qaug_ref[c, 0:hd, :],
                            preferred_element_type=F32)

    def scores(c, r0):
        return jnp.dot(kaug_ref[c, pl.ds(r0, blk_sz), :], qaug_ref[c],
                       preferred_element_type=F32)

    def accumulate(c, s, r0):
        m_prev = m_ref[c]
        m_new = jnp.maximum(m_prev, jnp.max(s, axis=0, keepdims=True))
        p = jnp.exp2(s - m_new).astype(BF16)
        pv = jnp.dot(vt_ref[c, :, pl.ds(r0, blk_sz)], p, preferred_element_type=F32)
        acc_ref[c] = jnp.exp2(m_prev - m_new) * acc_ref[c] + pv
        m_ref[c] = m_new

    key_t = lax.broadcasted_iota(jnp.int32, (blk_sz, rows), 0)
    row_t = lax.broadcasted_iota(jnp.int32, (blk_sz, rows), 1) % blk_sz
    for c in chains:
        s = jnp.where(key_t <= row_t, sb_ref[c], NEG_INF)
        m0 = jnp.max(s, axis=0, keepdims=True)
        m_ref[c] = m0
        acc_ref[c] = jnp.dot(vt_ref[c, :, pl.ds(r_own, blk_sz)], jnp.exp2(s - m0).astype(BF16),
                             preferred_element_type=F32)

    n_pairs = (qi + 1) // 2

    def past_pair(jj, carry):
        r0 = pl.multiple_of(jj * (2 * blk_sz), 2 * blk_sz)
        r1 = pl.multiple_of(r0 + blk_sz, blk_sz)
        r2 = pl.multiple_of(jnp.minimum(r0 + 2 * blk_sz, (n_blocks - 1) * blk_sz), blk_sz)
        for c in chains:
            sb_ref[c] = scores(c, r1)
            accumulate(c, sa_ref[c], r0)
        for c in chains:
            sa_ref[c] = scores(c, r2)
            accumulate(c, sb_ref[c], r1)
        return carry

    lax.fori_loop(0, n_pairs, past_pair, 0)

    for c in chains:
        acc = acc_ref[c]
        o = (acc[0:hd, :] * (1.0 / acc[hd:hd + 1, :])).T
        o = jnp.concatenate([o[blk_sz * g:blk_sz * (g + 1), :] for g in range(MOBA_GROUP)], axis=1)
        o_ref[:, gw * c:gw * (c + 1)] = (o * _silu(g_ref[:, gw * c:gw * (c + 1)])).astype(o_ref.dtype)


def _moba_prompt(proj, t):
    assert t % (2 * MOBA_BLOCK) == 0
    nb = t // MOBA_BLOCK
    nb_pad = -(-nb // BF16_SUBLANES) * BF16_SUBLANES
    assert nb_pad <= LANE
    rows = MOBA_GROUP * MOBA_BLOCK
    nc = _MOBA_KV_PER_STEP
    gw = MOBA_GROUP * MOBA_HD * nc
    kvw = MOBA_HD * nc
    assert MOBA_KV_HEADS % nc == 0
    assert OFF_MQ % gw == 0 and OFF_MG % gw == 0 and OFF_MK % kvw == 0 and OFF_MV % kvw == 0
    kv_spec = functools.partial(pl.BlockSpec, (t, kvw), pipeline_mode=pl.Buffered(1))
    return pl.pallas_call(
        functools.partial(_moba_prompt_kernel, n_blocks=nb),
        grid=(MOBA_KV_HEADS // nc, nb),
        in_specs=[pl.BlockSpec((MOBA_BLOCK, gw), lambda h, i: (i, OFF_MQ // gw + h)),
                  kv_spec(lambda h, i: (0, OFF_MK // kvw + h)),
                  kv_spec(lambda h, i: (0, OFF_MV // kvw + h)),
                  pl.BlockSpec((MOBA_BLOCK, gw), lambda h, i: (i, OFF_MG // gw + h))],
        out_specs=pl.BlockSpec((MOBA_BLOCK, gw), lambda h, i: (i, h)),
        out_shape=jax.ShapeDtypeStruct((t, MOBA_QW), BF16),
        scratch_shapes=[pltpu.VMEM((nc, t, MOBA_HD + LANE), BF16),
                        pltpu.VMEM((nc, MOBA_HD + BF16_SUBLANES, t), BF16),
                        pltpu.VMEM((nc, nb_pad, MOBA_HD), F32),
                        pltpu.VMEM((nc, MOBA_HD + LANE, rows), BF16),
                        pltpu.VMEM((nc, 1, rows), F32),
                        pltpu.VMEM((nc, MOBA_HD + BF16_SUBLANES, rows), F32),
                        pltpu.VMEM((nc, MOBA_BLOCK, rows), F32),
                        pltpu.VMEM((nc, MOBA_BLOCK, rows), F32)],
        compiler_params=_params("arbitrary", "arbitrary"),
        name="moba_prompt",
    )(proj, proj, proj, proj)


_GLA_GROUP = 8
_GLA_PAIRS_PER_STEP = 8
_GLA_PAIRS_PER_VCHUNK = 4


def _gla_prompt_kernel(q_ref, k_ref, b_ref, *rest, n_tblocks, tb):
    ti = pl.program_id(1)
    sub = GLA_SUB
    kw = 2 * GLA_DK
    vw = 2 * GLA_DV
    npp = _GLA_PAIRS_PER_STEP
    nvc = npp // _GLA_PAIRS_PER_VCHUNK
    v_refs, gg_refs = rest[:nvc], rest[nvc:2 * nvc]
    gain_ref, o_ref, sout_ref, s_ref, oacc_ref = rest[2 * nvc:]

    @pl.when(ti == 0)
    def _zero_state():
        s_ref[...] = jnp.zeros(s_ref.shape, F32)

    row_h = lax.broadcasted_iota(jnp.int32, (kw, vw), 0) // GLA_DK
    col_h = lax.broadcasted_iota(jnp.int32, (kw, vw), 1) // GLA_DV
    same_head = row_h == col_h
    head_sum = jnp.where(same_head, 1.0, 0.0).astype(BF16)
    j_idx = lax.broadcasted_iota(jnp.int32, (sub, kw), 0)
    grp = _GLA_GROUP
    lane_head = lax.broadcasted_iota(jnp.int32, (grp * sub, kw), 1) // GLA_DK
    gt = grp * sub
    half = sub // 2
    sub_rows = half * half + half * sub
    rs_in = lax.broadcasted_iota(jnp.int32, (sub, sub_rows), 1)
    rs_tok = jnp.where(rs_in < half * half, rs_in // half, half + (rs_in - half * half) // sub)
    row_sum = jnp.where(lax.broadcasted_iota(jnp.int32, (sub, sub_rows), 0) == rs_tok,
                        1.0, 0.0).astype(BF16)

    def pair_group(q, k, b, v, state):
        w_rows, v_rows = [], []
        for s in range(grp):
            lo = s * sub
            b_s, k_s, v_s = b[lo:lo + sub, :], k[lo:lo + sub, :], v[lo:lo + sub, :]
            for i in range(sub):
                nj = half if i < half else sub
                d = b_s[i:i + 1, :] - b_s[0:nj, :]
                dec = jnp.exp(jnp.where(j_idx[0:nj, :] < i, d, NEG_INF))
                w_rows.append(dec * k_s[0:nj, :] * q[lo + i:lo + i + 1, :])
                v_rows.append(v_s[0:nj, :])
        w = jnp.concatenate(w_rows, axis=0).astype(BF16)
        sc = jnp.dot(w, head_sum, preferred_element_type=F32)
        z = (sc * jnp.concatenate(v_rows, axis=0)).astype(BF16)
        o_diag = jnp.concatenate(
            [jnp.dot(row_sum, z[sub_rows * s:sub_rows * (s + 1), :], preferred_element_type=F32)
             for s in range(grp)], axis=0)
        qk = q * k
        o_self = jnp.concatenate(
            [jnp.sum(jnp.where(lane_head == h, qk, 0.0), axis=1, keepdims=True)
             * v[:, GLA_DV * h:GLA_DV * (h + 1)] for h in range(2)], axis=1)
        o_diag = o_diag + o_self
        q_dec = (q * jnp.exp(b)).astype(BF16)
        upds, decays = [], []
        for s in range(grp):
            lo = s * sub
            b_s, k_s, v_s = b[lo:lo + sub, :], k[lo:lo + sub, :], v[lo:lo + sub, :]
            b_last = b_s[sub - 1:sub, :]
            k_dec = (k_s * jnp.exp(b_last - b_s)).astype(BF16)
            upd = lax.dot_general(k_dec, v_s.astype(BF16), _TN, preferred_element_type=F32)
            upds.append(jnp.where(same_head, upd, 0.0))
            e_col = jnp.broadcast_to(jnp.exp(b_last), (kw, kw)).T
            decays.append(jnp.concatenate([e_col, e_col], axis=1))
        o_inter = []
        for s in range(grp):
            lo = s * sub
            o_inter.append(jnp.dot(q_dec[lo:lo + sub, :], state.astype(BF16),
                                   preferred_element_type=F32))
            state = state * decays[s] + upds[s]
        return jnp.concatenate(o_inter, axis=0) + o_diag, state

    def group(g, carry):
        r0 = pl.multiple_of(g * gt, gt)
        for pp in range(npp):
            ks, vs = slice(kw * pp, kw * (pp + 1)), slice(vw * pp, vw * (pp + 1))
            pc = pp % _GLA_PAIRS_PER_VCHUNK
            v_pair = v_refs[pp // _GLA_PAIRS_PER_VCHUNK][pl.ds(r0, gt), vw * pc:vw * (pc + 1)]
            o, state = pair_group(q_ref[pl.ds(r0, gt), ks] * (GLA_DK ** -0.5), k_ref[pl.ds(r0, gt), ks],
                                  b_ref[pl.ds(r0, gt), ks], v_pair, s_ref[pp])
            s_ref[pp] = state
            oacc_ref[pl.ds(r0, gt), vs] = o
        return carry

    lax.fori_loop(0, tb // gt, group, 0)

    o = oacc_ref[...]
    gain = gain_ref[...]
    heads = []
    for h in range(2 * npp):
        oh = o[:, GLA_DV * h:GLA_DV * (h + 1)]
        ms = jnp.mean(oh * oh, axis=-1, keepdims=True)
        heads.append(oh * lax.rsqrt(ms + NORM_EPS) * gain)
    o = jnp.concatenate(heads, axis=1)
    gate = jnp.concatenate([r[...] for r in gg_refs], axis=1)
    o_ref[...] = (o * _silu(gate)).astype(o_ref.dtype)

    @pl.when(ti == n_tblocks - 1)
    def _emit_state():
        for pp in range(npp):
            st = s_ref[pp]
            sout_ref[2 * pp] = st[0:GLA_DK, 0:GLA_DV]
            sout_ref[2 * pp + 1] = st[GLA_DK:2 * GLA_DK, GLA_DV:2 * GLA_DV]


def _gla_prompt(proj, decay, gain, t):
    tb = _tile(t, (512, 256))
    assert t % tb == 0 and tb % (GLA_SUB * _GLA_GROUP) == 0
    nt = t // tb
    npp = _GLA_PAIRS_PER_STEP
    kw, vw = 2 * GLA_DK * npp, 2 * GLA_DV * npp
    vcw = 2 * GLA_DV * _GLA_PAIRS_PER_VCHUNK
    nvc = npp // _GLA_PAIRS_PER_VCHUNK
    assert GLA_PAIRS % npp == 0 and npp % _GLA_PAIRS_PER_VCHUNK == 0
    assert OFF_GQ % kw == 0 and OFF_GK % kw == 0 and OFF_GV % vcw == 0 and OFF_GG % vcw == 0

    def vchunk(off, c):
        return pl.BlockSpec((tb, vcw), lambda p, i: (i, off // vcw + p * nvc + c))

    return pl.pallas_call(
        functools.partial(_gla_prompt_kernel, n_tblocks=nt, tb=tb),
        grid=(GLA_PAIRS // npp, nt),
        in_specs=[pl.BlockSpec((tb, kw), lambda p, i: (i, OFF_GQ // kw + p)),
                  pl.BlockSpec((tb, kw), lambda p, i: (i, OFF_GK // kw + p)),
                  pl.BlockSpec((tb, kw), lambda p, i: (i, p))]
                 + [vchunk(OFF_GV, c) for c in range(nvc)]
                 + [vchunk(OFF_GG, c) for c in range(nvc)]
                 + [pl.BlockSpec((1, GLA_DV), lambda p, i: (0, 0))],
        out_specs=[pl.BlockSpec((tb, vw), lambda p, i: (i, p)),
                   pl.BlockSpec((2 * npp, GLA_DK, GLA_DV), lambda p, i: (p, 0, 0))],
        out_shape=[jax.ShapeDtypeStruct((t, GLA_VW), BF16),
                   jax.ShapeDtypeStruct((GLA_HEADS, GLA_DK, GLA_DV), F32)],
        scratch_shapes=[pltpu.VMEM((npp, 2 * GLA_DK, 2 * GLA_DV), F32),
                        pltpu.VMEM((tb, vw), F32)],
        compiler_params=_params("arbitrary", "arbitrary"),
        name="gla_prompt",
    )(proj, proj, decay, *([proj] * (2 * nvc)), gain)


def _out_proj_kernel(og_ref, om_ref, wg_ref, wm_ref, x_ref, lg_ref, lb_ref, o_ref,
                     *, n_ctiles, tn, alpha):
    j = pl.program_id(1)
    mixed = jnp.dot(og_ref[...], wg_ref[...], preferred_element_type=F32)
    mixed = mixed + jnp.dot(om_ref[...], wm_ref[...], preferred_element_type=F32)
    c0 = pl.multiple_of(j * tn, tn)
    o_ref[:, pl.ds(c0, tn)] = alpha * x_ref[...] + mixed

    @pl.when(j == n_ctiles - 1)
    def _layer_norm():
        h = o_ref[...]
        mu = jnp.mean(h, axis=-1, keepdims=True)
        hc = h - mu
        var = jnp.mean(hc * hc, axis=-1, keepdims=True)
        o_ref[...] = hc * lax.rsqrt(var + NORM_EPS) * lg_ref[...] + lb_ref[...]


def _out_proj(og, om, w_out_bf, x, ln_g, ln_b, alpha, tm, tn):
    m, d = x.shape
    kg, km = og.shape[1], om.shape[1]
    assert kg == km and w_out_bf.shape[0] == kg + km
    nct = d // tn
    w_mode = dict(pipeline_mode=pl.Buffered(1)) if nct == 1 else {}
    return pl.pallas_call(
        functools.partial(_out_proj_kernel, n_ctiles=nct, tn=tn, alpha=alpha),
        grid=(m // tm, nct),
        in_specs=[pl.BlockSpec((tm, kg), lambda i, j: (i, 0)),
                  pl.BlockSpec((tm, km), lambda i, j: (i, 0)),
                  pl.BlockSpec((kg, tn), lambda i, j: (0, j), **w_mode),
                  pl.BlockSpec((km, tn), lambda i, j: (1, j), **w_mode),
                  pl.BlockSpec((tm, tn), lambda i, j: (i, j)),
                  pl.BlockSpec((1, d), lambda i, j: (0, 0)),
                  pl.BlockSpec((1, d), lambda i, j: (0, 0))],
        out_specs=pl.BlockSpec((tm, d), lambda i, j: (i, 0)),
        out_shape=jax.ShapeDtypeStruct((m, d), F32),
        compiler_params=_params("arbitrary", "arbitrary"),
        name="out_proj_ln",
    )(og, om, w_out_bf, w_out_bf, x, ln_g, ln_b)


_MOBA_DEC_SEQS = 2


def _moba_decode_kernel(pt_ref, q_ref, kn_ref, vn_ref, g_ref, *rest, n_pages, page):
    del pt_ref
    n_seq = _MOBA_DEC_SEQS
    o_ref = rest[2 * n_seq * n_pages]
    k_refs = [rest[sq * n_pages:(sq + 1) * n_pages] for sq in range(n_seq)]
    v_refs = [rest[(n_seq + sq) * n_pages:(n_seq + sq + 1) * n_pages] for sq in range(n_seq)]
    qs = [q_ref[sq] * (MOBA_HD ** -0.5) for sq in range(n_seq)]
    ss = [_moba_decode_scores(qs[sq], k_refs[sq], n_pages) for sq in range(n_seq)]
    ps = [_moba_decode_select(ss[sq], qs[sq], kn_ref[sq], n_pages, page) for sq in range(n_seq)]
    for sq in range(n_seq):
        pb, p_new, denom = ps[sq]
        o = _moba_decode_values(pb, p_new, denom, vn_ref[sq], v_refs[sq], n_pages, page)
        o_ref[sq] = (o * _silu(g_ref[sq])).astype(o_ref.dtype)


def _moba_decode_scores(q, k_refs, n_pages):
    qb = q.astype(BF16)
    return jnp.concatenate(
        [lax.dot_general(qb, k_refs[p][0, 0].astype(BF16), _NT, preferred_element_type=F32)
         for p in range(n_pages)], axis=1)


def _expand_kv_rows(x):
    return jnp.concatenate([jnp.broadcast_to(x[i:i + 1, :], (MOBA_GROUP, MOBA_HD))
                            for i in range(MOBA_KV_HEADS)], axis=0)


def _moba_decode_select(s, q, kn, n_pages, page):
    heads = MOBA_HEADS
    brow = MOBA_BLOCK * MOBA_KV_HEADS
    n_blk = (n_pages * page) // MOBA_BLOCK
    lane = lax.broadcasted_iota(jnp.int32, s.shape, 1)
    head_kv = lax.broadcasted_iota(jnp.int32, s.shape, 0) // MOBA_GROUP
    own_kv = (lane % MOBA_KV_HEADS) == head_kv
    s_own = jnp.where(own_kv, s, 0.0)

    assert n_blk <= LANE
    blk_lane = lax.broadcasted_iota(jnp.int32, (heads, LANE), 1)
    cur = jnp.full((heads, LANE), NEG_INF, F32)
    for j in range(n_blk):
        mean_j = jnp.sum(s_own[:, brow * j:brow * (j + 1)], axis=1, keepdims=True) * (1.0 / MOBA_BLOCK)
        cur = jnp.where(blk_lane == j, mean_j, cur)
    sel = jnp.zeros((heads, LANE), F32)
    for _ in range(min(MOBA_TOPK, n_blk)):
        mx = jnp.max(cur, axis=1, keepdims=True)
        first = jnp.min(jnp.where(cur == mx, blk_lane, n_blk), axis=1, keepdims=True)
        pick = blk_lane == first
        sel = jnp.where(pick, 1.0, sel)
        cur = jnp.where(pick, NEG_INF, cur)
    bias = jnp.concatenate(
        [jnp.broadcast_to(jnp.where(jnp.sum(jnp.where(blk_lane == j, sel, 0.0), axis=1, keepdims=True) > 0.5,
                                    0.0, NEG_INF), (heads, brow)) for j in range(n_blk)], axis=1)

    s_new = jnp.sum(q * _expand_kv_rows(kn), axis=1, keepdims=True)
    sm = jnp.where(own_kv, s + bias, NEG_INF)
    m = jnp.maximum(jnp.max(sm, axis=1, keepdims=True), s_new)
    p = jnp.exp(sm - m)
    p_new = jnp.exp(s_new - m)
    denom = jnp.sum(p, axis=1, keepdims=True) + p_new
    return p.astype(BF16), p_new, denom


def _moba_decode_values(pb, p_new, denom, vn, v_refs, n_pages, page):
    prow = page * MOBA_KV_HEADS
    o = p_new * _expand_kv_rows(vn)
    for pg in range(n_pages):
        o = o + jnp.dot(pb[:, prow * pg:prow * (pg + 1)], v_refs[pg][0, 0].astype(BF16),
                        preferred_element_type=F32)
    return o * (1.0 / denom)


def _moba_decode(page_table, q3, kn3, vn3, g3, cache_k, cache_v, layer):
    db, n_pages = page_table.shape
    depth, n_pool, page = cache_k.shape[0], cache_k.shape[1], cache_k.shape[2]
    assert (n_pages * page) % MOBA_BLOCK == 0
    prow = page * MOBA_KV_HEADS
    ck = cache_k.reshape(depth, n_pool, prow, MOBA_HD)
    cv = cache_v.reshape(depth, n_pool, prow, MOBA_HD)

    n_seq = _MOBA_DEC_SEQS
    assert db % n_seq == 0

    def page_spec(sq, pg):
        return pl.BlockSpec((1, 1, prow, MOBA_HD), lambda b, pt: (layer, pt[b * n_seq + sq, pg], 0, 0))

    def row_spec(r):
        return pl.BlockSpec((n_seq, r, MOBA_HD), lambda b, pt: (b, 0, 0))

    pages = [page_spec(sq, pg) for sq in range(n_seq) for pg in range(n_pages)]
    grid_spec = pltpu.PrefetchScalarGridSpec(
        num_scalar_prefetch=1,
        grid=(db // n_seq,),
        in_specs=[row_spec(MOBA_HEADS), row_spec(MOBA_KV_HEADS), row_spec(MOBA_KV_HEADS),
                  row_spec(MOBA_HEADS)] + pages + pages,
        out_specs=pl.BlockSpec((n_seq, MOBA_HEADS, MOBA_HD), lambda b, pt: (b, 0, 0)),
    )
    n_ops = n_seq * n_pages
    return pl.pallas_call(
        functools.partial(_moba_decode_kernel, n_pages=n_pages, page=page),
        grid_spec=grid_spec,
        out_shape=jax.ShapeDtypeStruct((db, MOBA_HEADS, MOBA_HD), BF16),
        compiler_params=_params("arbitrary"),
        name="moba_decode",
    )(page_table, q3, kn3, vn3, g3, *([ck] * n_ops), *([cv] * n_ops))


_GLA_DEC_GROUP = 8


def _gla_decode_kernel(q_ref, k_ref, la_ref, v_ref, gg_ref, gain_ref, s_ref, o_ref, sn_ref):
    grp = _GLA_DEC_GROUP

    def columns(x):
        xp = jnp.concatenate([x, jnp.zeros((LANE - grp, x.shape[1]), F32)], axis=0)
        return xp.T

    a_t = columns(jnp.exp(la_ref[...]))
    k_t = columns(k_ref[...])
    q_t = columns(q_ref[...] * (GLA_DK ** -0.5))
    gain = gain_ref[...]
    for i in range(grp):
        rows = []
        for h in range(GLA_HEADS):
            r = slice(GLA_DK * h, GLA_DK * (h + 1))
            v_h = jnp.broadcast_to(v_ref[i, h:h + 1, :], (GLA_DK, GLA_DV))
            s_new = a_t[r, i:i + 1] * s_ref[i, h] + k_t[r, i:i + 1] * v_h
            sn_ref[i, h] = s_new
            rows.append(jnp.sum(q_t[r, i:i + 1] * s_new, axis=0, keepdims=True))
        o = jnp.concatenate(rows, axis=0)
        ms = jnp.mean(o * o, axis=-1, keepdims=True)
        o = o * lax.rsqrt(ms + NORM_EPS) * gain
        o_ref[i] = (o * _silu(gg_ref[i])).astype(o_ref.dtype)


def _gla_decode(q2, k2, la2, v3, gg3, gain, state):
    db = q2.shape[0]
    grp = _GLA_DEC_GROUP
    assert db % grp == 0
    vec = pl.BlockSpec((grp, GLA_KW), lambda g: (g, 0))
    hd3 = pl.BlockSpec((grp, GLA_HEADS, GLA_DV), lambda g: (g, 0, 0))
    st = pl.BlockSpec((grp, GLA_HEADS, GLA_DK, GLA_DV), lambda g: (g, 0, 0, 0))
    return pl.pallas_call(
        _gla_decode_kernel,
        grid=(db // grp,),
        in_specs=[vec, vec, vec, hd3, hd3, pl.BlockSpec((1, GLA_DV), lambda g: (0, 0)), st],
        out_specs=[hd3, st],
        out_shape=[jax.ShapeDtypeStruct((db, GLA_HEADS, GLA_DV), BF16),
                   jax.ShapeDtypeStruct(state.shape, F32)],
        compiler_params=_params("arbitrary"),
        name="gla_decode",
    )(q2, k2, la2, v3, gg3, gain, state)


def _tile(n, prefs):
    for p in prefs:
        if n % p == 0:
            return p
    return n


def kernel(x_prompt, x_sample, cache_k, cache_v, state_gla, page_table,
           w_in, w_a2, b_a, gla_gain, w_out, ln_g, ln_b):
    bsz, t, d = x_prompt.shape
    db, dt, _ = x_sample.shape
    depth = w_in.shape[0]
    assert bsz == 1 and dt == 1
    alpha = (2.0 * depth) ** 0.25
    y_p = x_prompt.reshape(t, d)
    y_s = x_sample.reshape(db, d)
    kp_l, vp_l, sp_l, ks_l, vs_l, ss_l = [], [], [], [], [], []
    for l in range(depth):
        w_in_t = jnp.swapaxes(w_in[l], 0, 1)
        w_lr = jnp.zeros((d, LANE), BF16).at[:, :GLA_LOWRANK].set(w_in[l][:, OFF_GA:].astype(BF16))
        w_a2p = jnp.zeros((LANE, GLA_KW), F32).at[:GLA_LOWRANK, :].set(w_a2[l])
        w_out_bf = w_out[l].astype(BF16)
        gain = gla_gain[l].reshape(1, GLA_DV)
        lg, lb = ln_g[l].reshape(1, d), ln_b[l].reshape(1, d)
        x_all, ga = _stack_cast(y_p, y_s, w_lr, _tile(math.gcd(t, db), (128,)))
        m_all = t + db
        proj = _in_proj(x_all, w_in_t, MAIN_W, _tile(m_all, (640, 512, 256, 128)), 1024)
        ba2 = b_a[l].reshape(1, GLA_KW)
        decay = _log_decay(ga, w_a2p, ba2, 0, t, _tile(math.gcd(t, 512), (512,)), True)
        log_a_s = _log_decay(ga, w_a2p, ba2, t, db, db, False)
        o_m = _moba_prompt(proj, t)
        o_g, s_p = _gla_prompt(proj, decay, gain, t)
        k_rows, v_rows = _kv_rows(proj, t, _tile(t, (512, 256, 128)))
        kp_l.append(k_rows.reshape(bsz, t, MOBA_KV_HEADS, MOBA_HD))
        vp_l.append(v_rows.reshape(bsz, t, MOBA_KV_HEADS, MOBA_HD))
        sp_l.append(s_p.reshape(bsz, GLA_HEADS, GLA_DK, GLA_DV))
        y_p_new = _out_proj(o_g, o_m, w_out_bf, y_p, lg, lb, alpha, _tile(t, (256, 128)), d)
        proj_s = proj[t:]
        k_new = proj_s[:, OFF_MK:OFF_MV].reshape(db, MOBA_KV_HEADS, MOBA_HD)
        v_new = proj_s[:, OFF_MV:OFF_MG].reshape(db, MOBA_KV_HEADS, MOBA_HD)
        o_m_s = _moba_decode(page_table,
                             proj_s[:, OFF_MQ:OFF_MK].reshape(db, MOBA_HEADS, MOBA_HD),
                             k_new, v_new,
                             proj_s[:, OFF_MG:OFF_GQ].reshape(db, MOBA_HEADS, MOBA_HD),
                             cache_k, cache_v, l)
        o_g_s, s_s = _gla_decode(proj_s[:, OFF_GQ:OFF_GK], proj_s[:, OFF_GK:OFF_GV], log_a_s,
                                 proj_s[:, OFF_GV:OFF_GG].reshape(db, GLA_HEADS, GLA_DV),
                                 proj_s[:, OFF_GG:OFF_GA].reshape(db, GLA_HEADS, GLA_DV),
                                 gain, state_gla[l])
        ks_l.append(k_new.reshape(db, dt, MOBA_KV_HEADS, MOBA_HD))
        vs_l.append(v_new.reshape(db, dt, MOBA_KV_HEADS, MOBA_HD))
        ss_l.append(s_s)
        y_s = _out_proj(o_g_s.reshape(db, GLA_VW), o_m_s.reshape(db, MOBA_QW), w_out_bf, y_s, lg, lb,
                        alpha, _tile(db, (128,)), _tile(d, (1024, 512, 256, 128)))
        y_p = y_p_new
    return (y_p.reshape(bsz, t, d), y_s.reshape(db, dt, d),
            jnp.stack(kp_l), jnp.stack(vp_l), jnp.stack(sp_l),
            jnp.stack(ks_l), jnp.stack(vs_l), jnp.stack(ss_l))
```
